```python
import math
import jax
import jax.numpy as jnp
from jax import lax
import numpy as np

D_MODEL = 1024
BATCH = 2
SEQ = 8192
DEPTH = 4
DEC_BATCH = 128
DEC_SEQ = 1
PAST_LEN = 2048
PAGE_SIZE = 128

N_MIXERS = 4
HEAD_DIM = 64
SCALE = HEAD_DIM ** -0.5
Q_BLOCK = 128
A_HEADS = D_MODEL // HEAD_DIM
A_BLOCK = 256
A_TOPK = 3
A_QBLOCK = 32
B_HEADS = D_MODEL // (2 * HEAD_DIM)
B_VDIM = 2 * HEAD_DIM
C_HEADS = D_MODEL // HEAD_DIM
C_GROUPS = 2
C_RATIO = C_HEADS // C_GROUPS
C_CMP_BLOCK = 32
C_SEL_BLOCK = 64
C_TOPK = 4
C_WINDOW = 512
D_HEADS = D_MODEL // HEAD_DIM
ALPHA = (2 * DEPTH) ** 0.25
BETA = (8 * DEPTH) ** -0.25
LN_EPS = 1e-5
NEG = -1e30
TINY = 1e-30
F32 = jnp.float32

kernel_name = 'hybrid_moba_diff_nsa_fox_decode_step'


def layer_norm(x, g, b):
    xf = x.astype(F32)
    xc = xf - jnp.mean(xf, -1, keepdims=True)
    var = jnp.mean(xc * xc, -1, keepdims=True)
    return (xc * lax.rsqrt(var + LN_EPS) * g.astype(F32) + b.astype(F32)).astype(x.dtype)


def rms_norm(x, g):
    xf = x.astype(F32)
    return xf * lax.rsqrt(jnp.mean(xf * xf, -1, keepdims=True) + LN_EPS) * g.astype(F32)


def alibi_slopes(n):
    return jnp.exp2(-8.0 * jnp.arange(1, n + 1, dtype=F32) / n)


def gated_out(o, z, w_out):
    B, L = z.shape[:2]
    return (o.reshape(B, L, -1).astype(z.dtype) * jax.nn.silu(z)) @ w_out


def pad_rows(a, mult):
    L = a.shape[1]
    Lp = -(-L // mult) * mult
    return jnp.pad(a, ((0, 0), (0, Lp - L)) + ((0, 0),) * (a.ndim - 2))


def fetch_dense(rows, pos):
    B, L, G = rows.shape[:3]
    b = jnp.arange(B)[:, None, None, None]
    g = jnp.arange(G)[None, None, :, None]
    return rows[b, jnp.clip(pos, 0, L - 1), g]


def fetch_paged(pool, page_table, new, pos):
    B, n_pages = page_table.shape
    past = n_pages * PAGE_SIZE
    G = pool.shape[2]
    b = jnp.arange(B)[:, None, None, None]
    g = jnp.arange(G)[None, None, :, None]
    pos = jnp.maximum(pos, 0)
    phys = page_table[b, jnp.minimum(pos // PAGE_SIZE, n_pages - 1)]
    old = pool[phys, pos % PAGE_SIZE, g]
    fresh = new[b, jnp.clip(pos - past, 0, new.shape[1] - 1), g]
    return jnp.where((pos < past)[..., None], old, fresh)


def mask_past_blocks(scores, q_pos, block):
    nb = scores.shape[-1]
    cand = jnp.arange(nb)[None, :] < (q_pos // block)[:, None]
    return jnp.where(cand[None, :, None, :], scores, NEG)


def block_positions(scores, q_pos, block, topk):
    nb = scores.shape[-1]
    top_s, idx = lax.top_k(scores, min(topk, nb))
    offs = jnp.arange(block)
    sel = idx[..., None] * block + offs
    own = (q_pos // block * block)[:, None] + offs
    own = jnp.broadcast_to(own[None, :, None, None, :], idx.shape[:3] + (1, block))
    pos = jnp.concatenate([sel, own], axis=3).reshape(idx.shape[:3] + (-1,))
    ok = jnp.concatenate([jnp.broadcast_to((top_s > 0.5 * NEG)[..., None], sel.shape),
                          jnp.ones(own.shape, bool)], axis=3).reshape(pos.shape)
    return pos, ok


def online_attention(q, xs, page_fn, k_new, v_new, bias_new):
    B, T, G, _ = q.shape
    qf = q.astype(F32) * SCALE

    def update(carry, k, v, bias):
        m, l, acc = carry
        s = jnp.einsum('btgd,bkgd->bgtk', qf, k.astype(F32)) + bias
        m_new = jnp.maximum(m, s.max(-1))
        corr = jnp.exp(m - m_new)
        p = jnp.exp(s - m_new[..., None])
        return (m_new, l * corr + p.sum(-1),
                acc * corr[..., None] + jnp.einsum('bgtk,bkge->bgte', p, v.astype(F32)))

    init = (jnp.full((B, G, T), NEG, F32), jnp.zeros((B, G, T), F32),
            jnp.zeros((B, G, T, v_new.shape[-1]), F32))
    carry, _ = lax.scan(lambda c, x: (update(c, *page_fn(x)), None), init, xs)
    m, l, acc = update(carry, k_new, v_new, bias_new)
    return acc / l[..., None]


def moba_attend(q, blk_mean, q_pos, fetch, slopes):
    qf = q.astype(F32)
    s_blk = mask_past_blocks(jnp.einsum('bqhd,bnhd->bqhn', qf, blk_mean), q_pos, A_BLOCK)
    pos, ok = block_positions(s_blk, q_pos, A_BLOCK, A_TOPK)
    k, v = fetch(pos)
    s = jnp.einsum('bqhd,bqhnd->bqhn', qf, k.astype(F32)) * SCALE
    dist = (q_pos[None, :, None, None] - pos).astype(F32)
    s = jnp.where(ok & (dist >= 0), s - slopes[:, None] * dist, NEG)
    return jnp.einsum('bqhn,bqhnd->bqhd', jax.nn.softmax(s, -1), v.astype(F32))


def moba_prompt(q, k, v, slopes):
    B, S, H, d = q.shape
    blk_mean = pad_rows(k.astype(F32), A_BLOCK).reshape(B, -1, A_BLOCK, H, d).sum(2) / A_BLOCK
    fetch = lambda pos: (fetch_dense(k, pos), fetch_dense(v, pos))

    def step(i):
        s0 = i * A_QBLOCK
        return moba_attend(lax.dynamic_slice_in_dim(q, s0, A_QBLOCK, 1), blk_mean,
                           s0 + jnp.arange(A_QBLOCK), fetch, slopes)

    o = lax.map(step, jnp.arange(S // A_QBLOCK))
    return o.transpose(1, 0, 2, 3, 4).reshape(B, S, H, d)


def moba_sample(q, k, v, pool_k, pool_v, page_table, slopes):
    B, T, H, d = q.shape
    past = page_table.shape[1] * PAGE_SIZE
    page_sum = lax.map(lambda ids: pool_k[ids].astype(F32).sum(1), page_table.T)
    new_sum = pad_rows(k.astype(F32), PAGE_SIZE).reshape(B, -1, PAGE_SIZE, H, d).sum(2)
    sums = jnp.concatenate([page_sum.transpose(1, 0, 2, 3), new_sum], 1)
    ppb = A_BLOCK // PAGE_SIZE
    blk_mean = pad_rows(sums, ppb).reshape(B, -1, ppb, H, d).sum(2) / A_BLOCK
    fetch = lambda pos: (fetch_paged(pool_k, page_table, k, pos), fetch_paged(pool_v, page_table, v, pos))
    return moba_attend(q, blk_mean, past + jnp.arange(T), fetch, slopes)


def mixer_moba(x_p, x_s, cache_k, cache_v, page_table, w_in, w_out):
    H, d = A_HEADS, HEAD_DIM
    slopes = alibi_slopes(H)

    def proj(x):
        B, L, _ = x.shape
        q, k, v, z = jnp.split(x @ w_in, 4, axis=-1)
        return q.reshape(B, L, H, d), k.reshape(B, L, H, d), v.reshape(B, L, H, d), z

    qp, kp, vp, zp = proj(x_p)
    qs, ks, vs, zs = proj(x_s)
    y_p = gated_out(moba_prompt(qp, kp, vp, slopes), zp, w_out)
    y_s = gated_out(moba_sample(qs, ks, vs, cache_k, cache_v, page_table, slopes), zs, w_out)
    return y_p, y_s, (kp, vp, ks, vs)


def diff_prompt(q, k, v, lam, slopes):
    B, S, G, d = q.shape
    H = G // 2
    sl = jnp.repeat(slopes, 2)[:, None, None]
    k_pos = jnp.arange(S)
    kf, vf = k.astype(F32), v.astype(F32)

    def step(i):
        s0 = i * Q_BLOCK
        qi = lax.dynamic_slice_in_dim(q, s0, Q_BLOCK, 1).astype(F32)
        dist = (s0 + jnp.arange(Q_BLOCK))[:, None] - k_pos[None, :]
        s = jnp.einsum('bqgd,bkgd->bgqk', qi, kf) * SCALE - sl * dist.astype(F32)
        p = jax.nn.softmax(jnp.where(dist >= 0, s, NEG), -1).reshape(B, H, 2, Q_BLOCK, S)
        return jnp.einsum('bhqk,bkhe->bqhe', p[:, :, 0] - lam * p[:, :, 1], vf)

    o = lax.map(step, jnp.arange(S // Q_BLOCK))
    return o.transpose(1, 0, 2, 3, 4).reshape(B, S, H, -1)


def diff_sample(q, k, v, pool_k, pool_v, page_table, lam, slopes):
    B, T, G, d = q.shape
    H = G // 2
    n_pages = page_table.shape[1]
    sl = jnp.repeat(slopes, 2)[None, :, None, None]
    q_pos = n_pages * PAGE_SIZE + jnp.arange(T)

    def page(x):
        p, ids = x
        dist = (q_pos[:, None] - (p * PAGE_SIZE + jnp.arange(PAGE_SIZE))[None, :]).astype(F32)
        return pool_k[ids], jnp.repeat(pool_v[ids], 2, axis=2), -sl * dist

    dist_new = q_pos[:, None] - q_pos[None, :]
    bias_new = jnp.where(dist_new >= 0, -sl * dist_new.astype(F32), NEG)
    o = online_attention(q, (jnp.arange(n_pages), page_table.T), page, k,
                         jnp.repeat(v, 2, axis=2), bias_new)
    o = o.reshape(B, H, 2, T, -1)
    return (o[:, :, 0] - lam * o[:, :, 1]).transpose(0, 2, 1, 3)


def mixer_diff(x_p, x_s, cache_k, cache_v, page_table, w_in, lam_q1, lam_k1, lam_q2, lam_k2,
               subln_g, w_out, layer):
    H, d, e = B_HEADS, HEAD_DIM, B_VDIM
    lam_init = 0.8 - 0.6 * math.exp(-0.3 * layer)
    lam = (jnp.exp(jnp.sum(lam_q1.astype(F32) * lam_k1.astype(F32)))
           - jnp.exp(jnp.sum(lam_q2.astype(F32) * lam_k2.astype(F32))) + lam_init)
    slopes = alibi_slopes(H)
    cuts = [2 * H * d, 4 * H * d, 4 * H * d + H * e]

    def proj(x):
        B, L, _ = x.shape
        q, k, v, z = jnp.split(x @ w_in, cuts, axis=-1)
        return q.reshape(B, L, 2 * H, d), k.reshape(B, L, 2 * H, d), v.reshape(B, L, H, e), z

    def head_out(o):
        return rms_norm(o, subln_g) * (1.0 - lam_init)

    qp, kp, vp, zp = proj(x_p)
    qs, ks, vs, zs = proj(x_s)
    y_p = gated_out(head_out(diff_prompt(qp, kp, vp, lam, slopes)), zp, w_out)
    y_s = gated_out(head_out(diff_sample(qs, ks, vs, cache_k, cache_v, page_table, lam, slopes)), zs, w_out)
    return y_p, y_s, (kp, vp, ks, vs)


def compress(rows, w):
    B, L, G, d = rows.shape
    blk = rows.reshape(B, L // C_CMP_BLOCK, C_CMP_BLOCK, G, d).transpose(0, 1, 3, 2, 4)
    return blk.reshape(B, L // C_CMP_BLOCK, G, C_CMP_BLOCK * d) @ w


def nsa_core(q, q_pos, ck, cv, fetch_sel, kw, vw, w_pos, gates, slopes):
    qf = q.astype(F32) * SCALE
    sl = slopes[:, :, None]
    nc = ck.shape[1]
    dist_c = q_pos[:, None] - ((jnp.arange(nc) + 1) * C_CMP_BLOCK - 1)[None, :]
    ok_c = (dist_c >= 0)[None, :, None, None, :]
    s = (jnp.einsum('bqgrd,bngd->bqgrn', qf, ck.astype(F32))
         - sl * dist_c[:, None, None, :].astype(F32))
    s = jnp.where(ok_c, s, NEG)
    pc = jnp.where(ok_c, jnp.exp(s - s.max(-1, keepdims=True)), 0.0)
    pc = pc / jnp.maximum(pc.sum(-1, keepdims=True), TINY)
    o_cmp = jnp.einsum('bqgrn,bngd->bqgrd', pc, cv.astype(F32))
    per = C_SEL_BLOCK // C_CMP_BLOCK
    imp = pc.sum(3).reshape(pc.shape[:3] + (nc // per, per)).sum(-1)
    pos, ok = block_positions(mask_past_blocks(imp, q_pos, C_SEL_BLOCK), q_pos, C_SEL_BLOCK, C_TOPK)
    ks, vs = fetch_sel(pos)
    dist_s = q_pos[None, :, None, None] - pos
    s = (jnp.einsum('bqgrd,bqgnd->bqgrn', qf, ks.astype(F32))
         - sl * dist_s[:, :, :, None, :].astype(F32))
    s = jnp.where((ok & (dist_s >= 0))[:, :, :, None, :], s, NEG)
    o_sel = jnp.einsum('bqgrn,bqgnd->bqgrd', jax.nn.softmax(s, -1), vs.astype(F32))
    dist_w = q_pos[:, None] - w_pos[None, :]
    ok_w = ((dist_w >= 0) & (dist_w <= C_WINDOW) & (w_pos >= 0)[None, :])[:, None, None, :]
    s = (jnp.einsum('bqgrd,bkgd->bqgrk', qf, kw.astype(F32))
         - sl * dist_w[:, None, None, :].astype(F32))
    o_win = jnp.einsum('bqgrk,bkgd->bqgrd', jax.nn.softmax(jnp.where(ok_w, s, NEG), -1), vw.astype(F32))
    g = jax.nn.sigmoid(gates.astype(F32))
    return g[..., 0:1] * o_cmp + g[..., 1:2] * o_sel + g[..., 2:3] * o_win


def nsa_prompt(q, kv, gates, w_ck, w_cv, slopes):
    kc, vc, ks, vs, kw, vw = kv
    B, S = q.shape[:2]
    ck = compress(pad_rows(kc, C_SEL_BLOCK), w_ck)
    cv = compress(pad_rows(vc, C_SEL_BLOCK), w_cv)
    front = ((0, 0), (C_WINDOW, 0), (0, 0), (0, 0))
    kw_pad, vw_pad = jnp.pad(kw, front), jnp.pad(vw, front)
    fetch = lambda pos: (fetch_dense(ks, pos), fetch_dense(vs, pos))

    def step(i):
        s0 = i * Q_BLOCK
        cut = lambda a, n: lax.dynamic_slice_in_dim(a, s0, n, 1)
        return nsa_core(cut(q, Q_BLOCK), s0 + jnp.arange(Q_BLOCK), ck, cv, fetch,
                        cut(kw_pad, C_WINDOW + Q_BLOCK), cut(vw_pad, C_WINDOW + Q_BLOCK),
                        s0 - C_WINDOW + jnp.arange(C_WINDOW + Q_BLOCK), cut(gates, Q_BLOCK), slopes)

    o = lax.map(step, jnp.arange(S // Q_BLOCK))
    return o.transpose(1, 0, 2, 3, 4, 5).reshape(B, S, -1)


def nsa_sample(q, kv, gates, pools, buf_kw, buf_vw, page_table, w_ck, w_cv, slopes):
    kc, vc, ks, vs, kw, vw = kv
    pool_kc, pool_vc, pool_ks, pool_vs = pools
    B, T = q.shape[:2]
    past = page_table.shape[1] * PAGE_SIZE
    ckp, cvp = lax.map(lambda ids: (compress(pool_kc[ids], w_ck), compress(pool_vc[ids], w_cv)),
                       page_table.T)
    merge = lambda a: a.transpose(1, 0, 2, 3, 4).reshape(B, -1, a.shape[3], a.shape[4])
    ck = jnp.concatenate([merge(ckp), compress(pad_rows(kc, C_SEL_BLOCK), w_ck)], 1)
    cv = jnp.concatenate([merge(cvp), compress(pad_rows(vc, C_SEL_BLOCK), w_cv)], 1)
    kw_all = jnp.concatenate([buf_kw, kw], 1)
    vw_all = jnp.concatenate([buf_vw, vw], 1)
    w_pos = past - buf_kw.shape[1] + jnp.arange(kw_all.shape[1])
    fetch = lambda pos: (fetch_paged(pool_ks, page_table, ks, pos), fetch_paged(pool_vs, page_table, vs, pos))
    o = nsa_core(q, past + jnp.arange(T), ck, cv, fetch, kw_all, vw_all, w_pos, gates, slopes)
    keep = min(C_WINDOW, kw_all.shape[1])
    return o.reshape(B, T, -1), kw_all[:, -keep:], vw_all[:, -keep:]


def mixer_nsa(x_p, x_s, cache_kc, cache_vc, cache_ks, cache_vs, state_kw, state_vw, page_table,
              w_in, w_ck, w_cv, w_out):
    G, R, d = C_GROUPS, C_RATIO, HEAD_DIM
    slopes = alibi_slopes(C_HEADS).reshape(G, R)
    cuts = np.cumsum([C_HEADS * d] + [G * d] * 6 + [C_HEADS * d]).tolist()

    def proj(x):
        B, L, _ = x.shape
        parts = jnp.split(x @ w_in, cuts, axis=-1)
        kv = tuple(t.reshape(B, L, G, d) for t in parts[1:7])
        return parts[0].reshape(B, L, G, R, d), kv, parts[7], parts[8].reshape(B, L, G, R, 3)

    qp, kvp, zp, gp = proj(x_p)
    qs, kvs, zs, gs = proj(x_s)
    y_p = gated_out(nsa_prompt(qp, kvp, gp, w_ck, w_cv, slopes), zp, w_out)
    o_s, kw_new, vw_new = nsa_sample(qs, kvs, gs, (cache_kc, cache_vc, cache_ks, cache_vs),
                                     state_kw, state_vw, page_table, w_ck, w_cv, slopes)
    y_s = gated_out(o_s, zs, w_out)
    keep = min(C_WINDOW, x_p.shape[1])
    return y_p, y_s, (kvp[0], kvp[1], kvp[2], kvp[3], kvp[4][:, -keep:], kvp[5][:, -keep:],
                      kvs[0], kvs[1], kvs[2], kvs[3], kw_new, vw_new)


def fox_prompt(q, k, v, c):
    B, S, H, d = q.shape
    kf, vf = k.astype(F32), v.astype(F32)
    ck = c.transpose(0, 2, 1)[:, :, None, :]
    k_pos = jnp.arange(S)

    def step(i):
        s0 = i * Q_BLOCK
        qi = lax.dynamic_slice_in_dim(q, s0, Q_BLOCK, 1).astype(F32)
        cq = lax.dynamic_slice_in_dim(c, s0, Q_BLOCK, 1).transpose(0, 2, 1)[..., None]
        causal = (s0 + jnp.arange(Q_BLOCK))[:, None] >= k_pos[None, :]
        s = jnp.einsum('bqhd,bkhd->bhqk', qi, kf) * SCALE + (cq - ck)
        return jnp.einsum('bhqk,bkhd->bqhd', jax.nn.softmax(jnp.where(causal, s, NEG), -1), vf)

    o = lax.map(step, jnp.arange(S // Q_BLOCK))
    return o.transpose(1, 0, 2, 3, 4).reshape(B, S, H, d)


def fox_sample(q, k, v, logf, pool_k, pool_v, pool_logf, page_table):
    B, T, H, d = q.shape
    n_pages = page_table.shape[1]
    past = n_pages * PAGE_SIZE
    c = jnp.cumsum(jnp.concatenate([pool_logf[page_table].reshape(B, past, H).astype(F32), logf], 1), axis=1)
    c_new = c[:, past:].transpose(0, 2, 1)
    cq = c_new[..., None]
    c_pages = c[:, :past].reshape(B, n_pages, PAGE_SIZE, H).transpose(1, 0, 3, 2)

    def page(x):
        ids, cp = x
        return pool_k[ids], pool_v[ids], cq - cp[:, :, None, :]

    causal = jnp.arange(T)[:, None] >= jnp.arange(T)[None, :]
    bias_new = jnp.where(causal, cq - c_new[:, :, None, :], NEG)
    o = online_attention(q, (page_table.T, c_pages), page, k, v, bias_new)
    return o.transpose(0, 2, 1, 3)


def mixer_fox(x_p, x_s, cache_k, cache_v, cache_logf, page_table, w_in, b_f, w_out):
    H, d = D_HEADS, HEAD_DIM
    cuts = [H * d, 2 * H * d, 3 * H * d, 4 * H * d]

    def proj(x):
        B, L, _ = x.shape
        q, k, v, z, f = jnp.split(x @ w_in, cuts, axis=-1)
        logf = jax.nn.log_sigmoid(f.astype(F32) + b_f.astype(F32))
        return q.reshape(B, L, H, d), k.reshape(B, L, H, d), v.reshape(B, L, H, d), z, logf

    qp, kp, vp, zp, lp = proj(x_p)
    qs, ks, vs, zs, ls = proj(x_s)
    y_p = gated_out(fox_prompt(qp, kp, vp, jnp.cumsum(lp, axis=1)), zp, w_out)
    y_s = gated_out(fox_sample(qs, ks, vs, ls, cache_k, cache_v, cache_logf, page_table), zs, w_out)
    return y_p, y_s, (kp, vp, lp, ks, vs, ls)


def setup_inputs(seed: int = 0) -> dict:
    key = jax.random.key(seed)
    keys = iter(jax.random.split(key, 48))

    def nrm(shape, scale=1.0):
        return jax.random.normal(next(keys), shape, F32) * scale

    d = HEAD_DIM
    n_pages = PAST_LEN // PAGE_SIZE
    used = DEC_BATCH * n_pages
    n_pool = used + used // 4
    w_buf = min(C_WINDOW, PAST_LEN)
    page_table = jax.random.permutation(next(keys), n_pool)[:used].reshape(DEC_BATCH, n_pages).astype(jnp.int32)

    def w_in(width):
        return nrm((D_MODEL, width), D_MODEL ** -0.5)

    def w_out(width):
        return nrm((width, D_MODEL), BETA * width ** -0.5)

    def gain(n):
        return 1.0 + nrm((n,), 0.02)

    def bias(n):
        return nrm((n,), 0.02)

    a_w = A_HEADS * d
    b_w = B_HEADS * B_VDIM
    c_w = C_HEADS * d
    d_w = D_HEADS * d
    cmp_in = C_CMP_BLOCK * d
    return {
        'x_prompt': nrm((BATCH, SEQ, D_MODEL)),
        'x_sample': nrm((DEC_BATCH, DEC_SEQ, D_MODEL)),
        'cache_a_k': nrm((n_pool, PAGE_SIZE, A_HEADS, d)),
        'cache_a_v': nrm((n_pool, PAGE_SIZE, A_HEADS, d)),
        'cache_b_k': nrm((n_pool, PAGE_SIZE, 2 * B_HEADS, d)),
        'cache_b_v': nrm((n_pool, PAGE_SIZE, B_HEADS, B_VDIM)),
        'cache_c_kc': nrm((n_pool, PAGE_SIZE, C_GROUPS, d)),
        'cache_c_vc': nrm((n_pool, PAGE_SIZE, C_GROUPS, d)),
        'cache_c_ks': nrm((n_pool, PAGE_SIZE, C_GROUPS, d)),
        'cache_c_vs': nrm((n_pool, PAGE_SIZE, C_GROUPS, d)),
        'state_c_kw': nrm((DEC_BATCH, w_buf, C_GROUPS, d)),
        'state_c_vw': nrm((DEC_BATCH, w_buf, C_GROUPS, d)),
        'cache_d_k': nrm((n_pool, PAGE_SIZE, D_HEADS, d)),
        'cache_d_v': nrm((n_pool, PAGE_SIZE, D_HEADS, d)),
        'cache_d_logf': jax.nn.log_sigmoid(2.0 + nrm((n_pool, PAGE_SIZE, D_HEADS))),
        'page_table': page_table,
        'w_in_0': w_in(4 * a_w),
        'w_out_0': w_out(a_w),
        'ln_g_0': gain(D_MODEL),
        'ln_b_0': bias(D_MODEL),
        'w_in_1': w_in(4 * B_HEADS * d + 2 * b_w),
        'lam_q1_1': nrm((d,), 0.1),
        'lam_k1_1': nrm((d,), 0.1),
        'lam_q2_1': nrm((d,), 0.1),
        'lam_k2_1': nrm((d,), 0.1),
        'subln_g_1': gain(B_VDIM),
        'w_out_1': w_out(b_w),
        'ln_g_1': gain(D_MODEL),
        'ln_b_1': bias(D_MODEL),
        'w_in_2': w_in(2 * c_w + 6 * C_GROUPS * d + 3 * C_HEADS),
        'w_cmp_k_2': nrm((cmp_in, d), cmp_in ** -0.5),
        'w_cmp_v_2': nrm((cmp_in, d), cmp_in ** -0.5),
        'w_out_2': w_out(c_w),
        'ln_g_2': gain(D_MODEL),
        'ln_b_2': bias(D_MODEL),
        'w_in_3': w_in(4 * d_w + D_HEADS),
        'b_f_3': jax.random.uniform(next(keys), (D_HEADS,), F32, 1.0, 4.0),
        'w_out_3': w_out(d_w),
        'ln_g_3': gain(D_MODEL),
        'ln_b_3': bias(D_MODEL),
    }


def reference(x_prompt, x_sample, cache_a_k, cache_a_v, cache_b_k, cache_b_v,
              cache_c_kc, cache_c_vc, cache_c_ks, cache_c_vs, state_c_kw, state_c_vw,
              cache_d_k, cache_d_v, cache_d_logf, page_table,
              w_in_0, w_out_0, ln_g_0, ln_b_0,
              w_in_1, lam_q1_1, lam_k1_1, lam_q2_1, lam_k2_1, subln_g_1, w_out_1, ln_g_1, ln_b_1,
              w_in_2, w_cmp_k_2, w_cmp_v_2, w_out_2, ln_g_2, ln_b_2,
              w_in_3, b_f_3, w_out_3, ln_g_3, ln_b_3):
    mixers = (
        lambda xp, xs, layer: mixer_moba(xp, xs, cache_a_k, cache_a_v, page_table, w_in_0, w_out_0),
        lambda xp, xs, layer: mixer_diff(xp, xs, cache_b_k, cache_b_v, page_table, w_in_1, lam_q1_1, lam_k1_1,
                                         lam_q2_1, lam_k2_1, subln_g_1, w_out_1, layer),
        lambda xp, xs, layer: mixer_nsa(xp, xs, cache_c_kc, cache_c_vc, cache_c_ks, cache_c_vs,
                                        state_c_kw, state_c_vw, page_table, w_in_2, w_cmp_k_2, w_cmp_v_2, w_out_2),
        lambda xp, xs, layer: mixer_fox(xp, xs, cache_d_k, cache_d_v, cache_d_logf, page_table, w_in_3, b_f_3, w_out_3),
    )
    norms = ((ln_g_0, ln_b_0), (ln_g_1, ln_b_1), (ln_g_2, ln_b_2), (ln_g_3, ln_b_3))
    x_p, x_s = x_prompt, x_sample
    states = []
    for i in range(DEPTH):
        y_p, y_s, st = mixers[i % N_MIXERS](x_p, x_s, i)
        g, b = norms[i]
        x_p = layer_norm(ALPHA * x_p + y_p, g, b)
        x_s = layer_norm(ALPHA * x_s + y_s, g, b)
        states.append(st)
    (a_k_p, a_v_p, a_k_s, a_v_s) = states[0]
    (b_k_p, b_v_p, b_k_s, b_v_s) = states[1]
    (c_kc_p, c_vc_p, c_ks_p, c_vs_p, c_kw_p, c_vw_p,
     c_kc_s, c_vc_s, c_ks_s, c_vs_s, c_kw_s, c_vw_s) = states[2]
    (d_k_p, d_v_p, d_logf_p, d_k_s, d_v_s, d_logf_s) = states[3]
    return (x_p, x_s,
            a_k_p, a_v_p, a_k_s, a_v_s,
            b_k_p, b_v_p, b_k_s, b_v_s,
            c_kc_p, c_vc_p, c_ks_p, c_vs_p, c_kw_p, c_vw_p,
            c_kc_s, c_vc_s, c_ks_s, c_vs_s, c_kw_s, c_vw_s,
            d_k_p, d_v_p, d_logf_p, d_k_s, d_v_s, d_logf_s)
```

```python
import functools
import math

import numpy as np
import jax
import jax.numpy as jnp
from jax import lax
from jax.experimental import pallas as pl
from jax.experimental.pallas import tpu as pltpu

F32 = jnp.float32
BF16 = jnp.bfloat16
HIGHEST = lax.Precision.HIGHEST

D_MODEL = 1024
HEAD_DIM = 64
N_HEADS = 16
LANES = 128
SCALE = HEAD_DIM ** -0.5
PAGE = 128
DEPTH = 4
ALPHA = (2 * DEPTH) ** 0.25
LN_EPS = 1e-5
NEG = -1e30
TINY = 1e-30
A_BLOCK = 256
A_TOPK = 3
B_HEADS = 8
C_GROUPS = 2
C_RATIO = 8
C_CMP_BLOCK = 32
C_SEL_BLOCK = 64
C_TOPK = 4
C_WINDOW = 512
VMEM_LIMIT = 56 * 1024 * 1024


def _dot_nt(a, b):
    return lax.dot_general(a, b, (((1,), (1,)), ((), ())), preferred_element_type=F32)


def _dot(a, b):
    return jnp.dot(a, b, preferred_element_type=F32)


def _iota(shape, dim):
    return lax.broadcasted_iota(jnp.int32, shape, dim)


def _params(*sem):
    return pltpu.CompilerParams(dimension_semantics=sem, vmem_limit_bytes=VMEM_LIMIT)


def _alibi_slopes(n):
    return jnp.exp2(-8.0 * jnp.arange(1, n + 1, dtype=F32) / n)


def _smem():
    return pl.BlockSpec(memory_space=pltpu.SMEM)


def _mm_kernel(x_ref, w_ref, o_ref):
    o_ref[...] = _dot(x_ref[...].astype(BF16), w_ref[...])


def _pick_tn(n):
    best = LANES
    for t in range(LANES, 1536 + 1, LANES):
        if n % t == 0:
            best = t
    return best


def _matmul(x, w, tm):
    m, k = x.shape
    n = w.shape[1]
    tm = min(tm, m)
    assert m % tm == 0
    tn = _pick_tn(n)
    return pl.pallas_call(
        _mm_kernel,
        grid=(m // tm, n // tn),
        in_specs=[pl.BlockSpec((tm, k), lambda i, j: (i, 0)),
                  pl.BlockSpec((k, tn), lambda i, j: (0, j))],
        out_specs=pl.BlockSpec((tm, tn), lambda i, j: (i, j)),
        out_shape=jax.ShapeDtypeStruct((m, n), F32),
        compiler_params=_params("parallel", "arbitrary"),
    )(x, w)


def _out_ln_kernel(o_ref, z_ref, x_ref, w_ref, g_ref, b_ref, y_ref):
    z = z_ref[...]
    a = (o_ref[...] * (z * jax.nn.sigmoid(z))).astype(BF16)
    h = ALPHA * x_ref[...] + _dot(a, w_ref[...])
    hc = h - jnp.mean(h, axis=-1, keepdims=True)
    var = jnp.mean(hc * hc, axis=-1, keepdims=True)
    y_ref[...] = hc * lax.rsqrt(var + LN_EPS) * g_ref[...] + b_ref[...]


def _out_ln(o, p, z_col, x, w, g, b, tm):
    m = x.shape[0]
    row = lambda i: (i, 0)
    return pl.pallas_call(
        _out_ln_kernel,
        grid=(m // tm,),
        in_specs=[pl.BlockSpec((tm, D_MODEL), row),
                  pl.BlockSpec((tm, D_MODEL), lambda i: (i, z_col)),
                  pl.BlockSpec((tm, D_MODEL), row),
                  pl.BlockSpec((D_MODEL, D_MODEL), lambda i: (0, 0)),
                  pl.BlockSpec((1, D_MODEL), lambda i: (0, 0)),
                  pl.BlockSpec((1, D_MODEL), lambda i: (0, 0))],
        out_specs=pl.BlockSpec((tm, D_MODEL), row),
        out_shape=jax.ShapeDtypeStruct((m, D_MODEL), F32),
        compiler_params=_params("parallel"),
    )(o, p, x, w, g.reshape(1, -1), b.reshape(1, -1))


def _online(s, h, m_scr, l_scr):
    m_prev = m_scr[h]
    m_new = jnp.maximum(m_prev, jnp.max(s, axis=1, keepdims=True))
    p = jnp.exp(s - m_new)
    corr = jnp.exp(m_prev - m_new)
    l_scr[h] = corr * l_scr[h] + jnp.sum(p, axis=1, keepdims=True)
    m_scr[h] = m_new
    return p, corr


def _topk_mask(s, cand, lane_f, k):
    s = jnp.where(cand, s, NEG)
    sel = jnp.zeros(s.shape, F32)
    for _ in range(k):
        mx = jnp.max(s, axis=1, keepdims=True)
        idx = jnp.min(jnp.where(s == mx, lane_f, 1e9), axis=1, keepdims=True)
        pick = lane_f == idx
        valid = jnp.where(mx > 0.5 * NEG, 1.0, 0.0)
        sel = jnp.where(pick, valid, sel)
        s = jnp.where(pick, -3e38, s)
    return sel


def _causal_tables(nq):
    it, jt = [], []
    for i in range(nq):
        for j in range(i + 1):
            it.append(i)
            jt.append(j)
    return np.asarray(it, np.int32), np.asarray(jt, np.int32)


def _flash_scratch(t, acc_mult=1):
    return [pltpu.VMEM((N_HEADS, t, 1), F32), pltpu.VMEM((N_HEADS, t, 1), F32),
            pltpu.VMEM((acc_mult * t, D_MODEL), F32)]


def _init_flash(m_scr, l_scr, acc_scr):
    m_scr[...] = jnp.full(m_scr.shape, NEG, F32)
    l_scr[...] = jnp.zeros(l_scr.shape, F32)
    acc_scr[...] = jnp.zeros(acc_scr.shape, F32)


def _block_mean_kernel(k_ref, o_ref):
    n = pl.program_id(1)

    @pl.when(n == 0)
    def _():
        o_ref[...] = jnp.zeros(o_ref.shape, F32)

    o_ref[0, pl.ds(n, 1), :] = jnp.sum(k_ref[...], axis=0, keepdims=True) / A_BLOCK


def _block_mean(p, batch, k_col):
    nb = p.shape[0] // batch // A_BLOCK
    return pl.pallas_call(
        _block_mean_kernel,
        grid=(batch, nb),
        in_specs=[pl.BlockSpec((A_BLOCK, D_MODEL), lambda b, n: (b * nb + n, k_col))],
        out_specs=pl.BlockSpec((1, LANES, D_MODEL), lambda b, n: (b, 0, 0)),
        out_shape=jax.ShapeDtypeStruct((batch, LANES, D_MODEL), F32),
        compiler_params=_params("parallel", "arbitrary"),
    )(p)


def _moba_kernel(it_ref, jt_ref, sl_ref, q_ref, k_ref, v_ref, bm_ref, o_ref,
                 m_scr, l_scr, acc_scr, sel_scr, *, t):
    st = pl.program_id(1)
    i = it_ref[st]
    j = jt_ref[st]
    lane = _iota((t, LANES), 1)
    lo = lane < HEAD_DIM

    @pl.when(j == 0)
    def _():
        _init_flash(m_scr, l_scr, acc_scr)
        cand = lane < i
        lane_f = lane.astype(F32)
        bm_lo = _iota((LANES, LANES), 1) < HEAD_DIM
        for hp in range(N_HEADS // 2):
            cs = slice(hp * LANES, (hp + 1) * LANES)
            qp = q_ref[:, cs].astype(BF16)
            bm = bm_ref[0, :, cs]
            for e in range(2):
                bme = jnp.where(bm_lo if e == 0 else ~bm_lo, bm, 0.0).astype(BF16)
                sel_scr[2 * hp + e] = _topk_mask(_dot_nt(qp, bme), cand, lane_f, A_TOPK)

    def tile(diag):
        rel = (_iota((1, t), 1) + (j - i) * t).astype(F32)
        if diag:
            causal = _iota((t, t), 1) <= _iota((t, t), 0)
        for hp in range(N_HEADS // 2):
            cs = slice(hp * LANES, (hp + 1) * LANES)
            qp = (q_ref[:, cs] * SCALE).astype(BF16)
            kp = k_ref[:, cs]
            vp = v_ref[:, cs]
            pv = None
            corrs = []
            for e in range(2):
                h = 2 * hp + e
                half = lo if e == 0 else ~lo
                ke = jnp.where(half, kp, 0.0).astype(BF16)
                ve = jnp.where(half, vp, 0.0).astype(BF16)
                s = _dot_nt(qp, ke) + sl_ref[h] * rel
                if diag:
                    s = jnp.where(causal, s, NEG)
                else:
                    picked = jnp.max(jnp.where(lane == j, sel_scr[h], 0.0), axis=1, keepdims=True)
                    s = s + jnp.where(picked > 0.5, 0.0, NEG)
                p, corr = _online(s, h, m_scr, l_scr)
                d = _dot(p.astype(BF16), ve)
                pv = d if pv is None else pv + d
                corrs.append(corr)
            acc_scr[:, cs] = acc_scr[:, cs] * jnp.where(lo, corrs[0], corrs[1]) + pv

    @pl.when(j < i)
    def _():
        tile(False)

    @pl.when(j == i)
    def _():
        tile(True)
        for hp in range(N_HEADS // 2):
            cs = slice(hp * LANES, (hp + 1) * LANES)
            o_ref[:, cs] = acc_scr[:, cs] / jnp.where(lo, l_scr[2 * hp], l_scr[2 * hp + 1])


def _moba_prompt(p, batch, slopes):
    t = A_BLOCK
    nq = p.shape[0] // batch // t
    it, jt = _causal_tables(nq)
    bm = _block_mean(p, batch, 1)
    blk = lambda col, tab: pl.BlockSpec(
        (t, D_MODEL), lambda b, s, it_, jt_: (b * nq + (it_ if tab == 0 else jt_)[s], col))
    return pl.pallas_call(
        functools.partial(_moba_kernel, t=t),
        grid_spec=pltpu.PrefetchScalarGridSpec(
            num_scalar_prefetch=2,
            grid=(batch, len(it)),
            in_specs=[_smem(), blk(0, 0), blk(1, 1), blk(2, 1),
                      pl.BlockSpec((1, LANES, D_MODEL), lambda b, s, it_, jt_: (b, 0, 0))],
            out_specs=blk(0, 0),
            scratch_shapes=_flash_scratch(t) + [pltpu.VMEM((N_HEADS, t, LANES), F32)]),
        out_shape=jax.ShapeDtypeStruct((p.shape[0], D_MODEL), F32),
        compiler_params=_params("parallel", "arbitrary"),
    )(it, jt, slopes, p, p, p, bm)


def _lambda(lam_ref, lam_init):
    a = lam_ref[...]
    return (jnp.exp(jnp.sum(a[0:1] * a[1:2], axis=1, keepdims=True))
            - jnp.exp(jnp.sum(a[2:3] * a[3:4], axis=1, keepdims=True)) + lam_init)


def _diff_kernel(it_ref, jt_ref, sl_ref, q_ref, k_ref, v_ref, lam_ref, g_ref, o_ref,
                 m_scr, l_scr, acc_scr, *, t, lam_init):
    st = pl.program_id(1)
    i = it_ref[st]
    j = jt_ref[st]
    lo = _iota((t, LANES), 1) < HEAD_DIM

    @pl.when(j == 0)
    def _():
        _init_flash(m_scr, l_scr, acc_scr)

    def tile(diag):
        rel = (_iota((1, t), 1) + (j - i) * t).astype(F32)
        if diag:
            causal = _iota((t, t), 1) <= _iota((t, t), 0)
        for h in range(B_HEADS):
            cs = slice(h * LANES, (h + 1) * LANES)
            qp = (q_ref[:, cs] * SCALE).astype(BF16)
            kp = k_ref[:, cs]
            vh = v_ref[:, cs].astype(BF16)
            for e in range(2):
                ke = jnp.where(lo if e == 0 else ~lo, kp, 0.0).astype(BF16)
                s = _dot_nt(qp, ke) + sl_ref[h] * rel
                if diag:
                    s = jnp.where(causal, s, NEG)
                p, corr = _online(s, 2 * h + e, m_scr, l_scr)
                rows = slice(e * t, (e + 1) * t)
                acc_scr[rows, cs] = acc_scr[rows, cs] * corr + _dot(p.astype(BF16), vh)

    @pl.when(j < i)
    def _():
        tile(False)

    @pl.when(j == i)
    def _():
        tile(True)
        lam = _lambda(lam_ref, lam_init)
        for h in range(B_HEADS):
            cs = slice(h * LANES, (h + 1) * LANES)
            o = acc_scr[0:t, cs] / l_scr[2 * h] - lam * (acc_scr[t:2 * t, cs] / l_scr[2 * h + 1])
            ms = jnp.mean(o * o, axis=1, keepdims=True)
            o_ref[:, cs] = o * lax.rsqrt(ms + LN_EPS) * g_ref[...] * (1.0 - lam_init)


def _diff_prompt(p, batch, slopes, lamv, subln_g, lam_init, t):
    nq = p.shape[0] // batch // t
    it, jt = _causal_tables(nq)
    blk = lambda col, tab: pl.BlockSpec(
        (t, D_MODEL), lambda b, s, it_, jt_: (b * nq + (it_ if tab == 0 else jt_)[s], col))
    full = lambda shape: pl.BlockSpec(shape, lambda b, s, it_, jt_: (0, 0))
    return pl.pallas_call(
        functools.partial(_diff_kernel, t=t, lam_init=lam_init),
        grid_spec=pltpu.PrefetchScalarGridSpec(
            num_scalar_prefetch=2,
            grid=(batch, len(it)),
            in_specs=[_smem(), blk(0, 0), blk(1, 1), blk(2, 1), full((4, HEAD_DIM)), full((1, LANES))],
            out_specs=blk(0, 0),
            scratch_shapes=_flash_scratch(t, 2)),
        out_shape=jax.ShapeDtypeStruct((p.shape[0], D_MODEL), F32),
        compiler_params=_params("parallel", "arbitrary"),
    )(it, jt, slopes, p, p, p, lamv, subln_g.reshape(1, -1))


def _logf_kernel(f_ref, b_ref, o_ref):
    x = f_ref[:, 0:N_HEADS] + b_ref[...]
    o_ref[...] = jnp.minimum(x, 0.0) - jnp.log1p(jnp.exp(-jnp.abs(x)))


def _logf(p, f_col, b_f, tm):
    m = p.shape[0]
    return pl.pallas_call(
        _logf_kernel,
        grid=(m // tm,),
        in_specs=[pl.BlockSpec((tm, LANES), lambda i: (i, f_col)),
                  pl.BlockSpec((1, N_HEADS), lambda i: (0, 0))],
        out_specs=pl.BlockSpec((tm, N_HEADS), lambda i: (i, 0)),
        out_shape=jax.ShapeDtypeStruct((m, N_HEADS), F32),
        compiler_params=_params("parallel"),
    )(p, b_f.reshape(1, -1))


def _cumsum_kernel(x_ref, o_ref, carry_scr, *, t):
    @pl.when(pl.program_id(1) == 0)
    def _():
        carry_scr[...] = jnp.zeros(carry_scr.shape, F32)

    x = x_ref[0]
    tri = (_iota((t, t), 0) <= _iota((t, t), 1)).astype(F32)
    c = jnp.dot(x, tri, precision=HIGHEST, preferred_element_type=F32) + carry_scr[...]
    o_ref[0] = c
    carry_scr[...] = c[:, t - 1:t]


def _cumsum_lanes(x, t):
    b, h, s = x.shape
    return pl.pallas_call(
        functools.partial(_cumsum_kernel, t=t),
        grid=(b, s // t),
        in_specs=[pl.BlockSpec((1, h, t), lambda bi, n: (bi, 0, n))],
        out_specs=pl.BlockSpec((1, h, t), lambda bi, n: (bi, 0, n)),
        out_shape=jax.ShapeDtypeStruct(x.shape, F32),
        scratch_shapes=[pltpu.VMEM((h, 1), F32)],
        compiler_params=_params("parallel", "arbitrary"),
    )(x)


def _fox_kernel(it_ref, jt_ref, q_ref, k_ref, v_ref, c_ref, o_ref, m_scr, l_scr, acc_scr, *, t):
    st = pl.program_id(1)
    i = it_ref[st]
    j = jt_ref[st]
    lo = _iota((t, LANES), 1) < HEAD_DIM

    @pl.when(j == 0)
    def _():
        _init_flash(m_scr, l_scr, acc_scr)

    def tile(diag):
        if diag:
            causal = _iota((t, t), 1) <= _iota((t, t), 0)
        for hp in range(N_HEADS // 2):
            cs = slice(hp * LANES, (hp + 1) * LANES)
            qp = (q_ref[:, cs] * SCALE).astype(BF16)
            kp = k_ref[:, cs]
            vp = v_ref[:, cs]
            pv = None
            corrs = []
            for e in range(2):
                h = 2 * hp + e
                half = lo if e == 0 else ~lo
                ke = jnp.where(half, kp, 0.0).astype(BF16)
                ve = jnp.where(half, vp, 0.0).astype(BF16)
                s = _dot_nt(qp, ke) - c_ref[0, h:h + 1, :]
                if diag:
                    s = jnp.where(causal, s, NEG)
                p, corr = _online(s, h, m_scr, l_scr)
                d = _dot(p.astype(BF16), ve)
                pv = d if pv is None else pv + d
                corrs.append(corr)
            acc_scr[:, cs] = acc_scr[:, cs] * jnp.where(lo, corrs[0], corrs[1]) + pv

    @pl.when(j < i)
    def _():
        tile(False)

    @pl.when(j == i)
    def _():
        tile(True)
        for hp in range(N_HEADS // 2):
            cs = slice(hp * LANES, (hp + 1) * LANES)
            o_ref[:, cs] = acc_scr[:, cs] / jnp.where(lo, l_scr[2 * hp], l_scr[2 * hp + 1])


def _fox_prompt(p, batch, c_t, t):
    nq = p.shape[0] // batch // t
    it, jt = _causal_tables(nq)
    blk = lambda col, tab: pl.BlockSpec(
        (t, D_MODEL), lambda b, s, it_, jt_: (b * nq + (it_ if tab == 0 else jt_)[s], col))
    return pl.pallas_call(
        functools.partial(_fox_kernel, t=t),
        grid_spec=pltpu.PrefetchScalarGridSpec(
            num_scalar_prefetch=2,
            grid=(batch, len(it)),
            in_specs=[blk(0, 0), blk(1, 1), blk(2, 1),
                      pl.BlockSpec((1, N_HEADS, t), lambda b, s, it_, jt_: (b, 0, jt_[s]))],
            out_specs=blk(0, 0),
            scratch_shapes=_flash_scratch(t)),
        out_shape=jax.ShapeDtypeStruct((p.shape[0], D_MODEL), F32),
        compiler_params=_params("parallel", "arbitrary"),
    )(it, jt, p, p, p, c_t)


C_Z_COL = 1
C_KV_COL = 16
C_GATE_COL = 22
C_WIDTH = 3072


def _compress_kernel(kc_ref, vc_ref, wk_ref, wv_ref, ck_ref, cv_ref, *, nch):
    stride = 2 * C_CMP_BLOCK
    for src, w_ref, dst in ((kc_ref, wk_ref, ck_ref), (vc_ref, wv_ref, cv_ref)):
        for parity in range(2):
            acc = jnp.zeros((nch, LANES), F32)
            for tt in range(C_CMP_BLOCK):
                rows = src[pl.ds(parity * C_CMP_BLOCK + tt, nch, stride=stride), :]
                acc = acc + _dot(rows.astype(BF16), w_ref[tt])
            dst[0, parity * nch:(parity + 1) * nch, :] = acc


def _compress_prompt(p, batch, w2k, w2v):
    s = p.shape[0] // batch
    nch = s // (2 * C_CMP_BLOCK)
    col = lambda c: pl.BlockSpec((s, LANES), lambda b: (b, c))
    wspec = pl.BlockSpec((C_CMP_BLOCK, LANES, LANES), lambda b: (0, 0, 0))
    ospec = pl.BlockSpec((1, 2 * nch, LANES), lambda b: (b, 0, 0))
    oshape = jax.ShapeDtypeStruct((batch, 2 * nch, LANES), F32)
    return pl.pallas_call(
        functools.partial(_compress_kernel, nch=nch),
        grid=(batch,),
        in_specs=[col(C_KV_COL), col(C_KV_COL + 1), wspec, wspec],
        out_specs=[ospec, ospec],
        out_shape=[oshape, oshape],
        compiler_params=_params("parallel"),
    )(p, p, w2k, w2v)


def _group_halves(x, g):
    lo = _iota(x.shape, 1) < HEAD_DIM
    base = jnp.where(lo if g == 0 else ~lo, x, 0.0)
    other = pltpu.roll(base, HEAD_DIM, 1)
    pair = (base, other) if g == 0 else (other, base)
    return pair[0].astype(BF16), pair[1].astype(BF16)


def _nsa_cmp_kernel(sl_ref, q_ref, g_ref, ck_ref, cv_ref, o_ref, sel_ref, *, tq, nch):
    q0 = pl.program_id(1) * tq
    nl = 2 * nch
    lane = _iota((tq, nl), 1)
    qpos = q0 + _iota((tq, nl), 0)
    tok = jnp.where(lane < nch, 2 * lane, 2 * (lane - nch) + 1)
    endp = (tok + 1) * C_CMP_BLOCK - 1
    okc = endp <= qpos
    relc = (endp[0:1, :] - q0).astype(F32)
    lo = _iota((tq, LANES), 1) < HEAD_DIM
    lane_s = _iota((tq, nch), 1)
    qblk = (q0 + _iota((tq, nch), 0)) // C_SEL_BLOCK
    lane_sf = lane_s.astype(F32)
    for g in range(C_GROUPS):
        k_lo, k_hi = _group_halves(ck_ref[0], g)
        v_lo, v_hi = _group_halves(cv_ref[0], g)
        imp = jnp.zeros((tq, nl), F32)
        for hp in range(g * C_RATIO // 2, (g + 1) * C_RATIO // 2):
            cs = slice(hp * LANES, (hp + 1) * LANES)
            qp = (q_ref[:, cs] * SCALE).astype(BF16)
            o_pair = None
            for e in range(2):
                h = 2 * hp + e
                s = _dot_nt(qp, k_lo if e == 0 else k_hi) + sl_ref[h] * relc
                s = jnp.where(okc, s, NEG)
                pc = jnp.where(okc, jnp.exp(s - jnp.max(s, axis=1, keepdims=True)), 0.0)
                pc = pc / jnp.maximum(jnp.sum(pc, axis=1, keepdims=True), TINY)
                imp = imp + pc
                gate = jax.nn.sigmoid(g_ref[:, 3 * h:3 * h + 1])
                o = _dot(pc.astype(BF16), v_lo if e == 0 else v_hi) * gate
                o_pair = o if o_pair is None else o_pair + o
            o_ref[:, cs] = o_pair
        imp_sel = imp[:, 0:nch] + imp[:, nch:nl]
        sel = _topk_mask(imp_sel, lane_s < qblk, lane_sf, C_TOPK)
        sel_ref[:, g * nch:(g + 1) * nch] = jnp.where(lane_s == qblk, 1.0, sel)


def _nsa_cmp(p, batch, slopes, ck, cv, tq):
    nq = p.shape[0] // batch // tq
    nch = ck.shape[1] // 2
    row = lambda width, col: pl.BlockSpec((tq, width), lambda b, i: (b * nq + i, col))
    cspec = pl.BlockSpec((1, 2 * nch, LANES), lambda b, i: (b, 0, 0))
    return pl.pallas_call(
        functools.partial(_nsa_cmp_kernel, tq=tq, nch=nch),
        grid=(batch, nq),
        in_specs=[_smem(), row(D_MODEL, 0), row(LANES, C_GATE_COL), cspec, cspec],
        out_specs=[row(D_MODEL, 0), row(C_GROUPS * nch, 0)],
        out_shape=[jax.ShapeDtypeStruct((p.shape[0], D_MODEL), F32),
                   jax.ShapeDtypeStruct((p.shape[0], C_GROUPS * nch), F32)],
        compiler_params=_params("parallel", "parallel"),
    )(slopes, p, p, ck, cv)


def _nsa_flash_kernel(it_ref, jt_ref, ft_ref, sl_ref, q_ref, k_ref, v_ref, g_ref, prev_ref, *rest,
                      t, mode, nch):
    if mode == "sel":
        sel_ref, o_ref, m_scr, l_scr, acc_scr = rest
        gate_idx = 1
    else:
        o_ref, m_scr, l_scr, acc_scr = rest
        gate_idx = 2
    st = pl.program_id(1)
    i = it_ref[st]
    j = jt_ref[st]
    lo = _iota((t, LANES), 1) < HEAD_DIM

    @pl.when(ft_ref[st] == 1)
    def _():
        _init_flash(m_scr, l_scr, acc_scr)

    diff = _iota((t, t), 1) - _iota((t, t), 0)
    rel = (_iota((1, t), 1) + (j - i) * t).astype(F32)
    if mode == "sel":
        in_range = diff <= jnp.where(j == i, 0, t)
        blocks = t // C_SEL_BLOCK
        expand = (_iota((nch, t), 0) == j * blocks + _iota((nch, t), 1) // C_SEL_BLOCK).astype(BF16)
    else:
        in_range = (diff <= jnp.where(j == i, 0, t)) & (diff >= jnp.where(j == i, -t, 0))
    for g in range(C_GROUPS):
        if mode == "sel":
            picked = _dot(sel_ref[:, g * nch:(g + 1) * nch].astype(BF16), expand)
            allowed = in_range & (picked > 0.5)
        else:
            allowed = in_range
        k_lo, k_hi = _group_halves(k_ref[...], g)
        v_lo, v_hi = _group_halves(v_ref[...], g)
        for hp in range(g * C_RATIO // 2, (g + 1) * C_RATIO // 2):
            cs = slice(hp * LANES, (hp + 1) * LANES)
            qp = (q_ref[:, cs] * SCALE).astype(BF16)
            pv = None
            corrs = []
            for e in range(2):
                h = 2 * hp + e
                s = _dot_nt(qp, k_lo if e == 0 else k_hi) + sl_ref[h] * rel
                s = jnp.where(allowed, s, NEG)
                p, corr = _online(s, h, m_scr, l_scr)
                d = _dot(p.astype(BF16), v_lo if e == 0 else v_hi)
                pv = d if pv is None else pv + d
                corrs.append(corr)
            acc_scr[:, cs] = acc_scr[:, cs] * jnp.where(lo, corrs[0], corrs[1]) + pv

    @pl.when(j == i)
    def _():
        for hp in range(N_HEADS // 2):
            cs = slice(hp * LANES, (hp + 1) * LANES)
            g0 = jax.nn.sigmoid(g_ref[:, 6 * hp + gate_idx:6 * hp + gate_idx + 1])
            g1 = jax.nn.sigmoid(g_ref[:, 6 * hp + 3 + gate_idx:6 * hp + 4 + gate_idx])
            o = acc_scr[:, cs] / jnp.where(lo, l_scr[2 * hp], l_scr[2 * hp + 1])
            o_ref[:, cs] = prev_ref[:, cs] + o * jnp.where(lo, g0, g1)


def _nsa_flash(p, batch, slopes, prev, sel, mode, t):
    nq = p.shape[0] // batch // t
    if mode == "sel":
        it, jt = _causal_tables(nq)
        kcol = C_KV_COL + 2
    else:
        assert t == C_WINDOW
        it = np.asarray([i for i in range(nq) for j in (i - 1, i) if j >= 0], np.int32)
        jt = np.asarray([j for i in range(nq) for j in (i - 1, i) if j >= 0], np.int32)
        kcol = C_KV_COL + 4
    ft = np.asarray([1 if (s == 0 or it[s] != it[s - 1]) else 0 for s in range(len(it))], np.int32)
    qrow = lambda width, col: pl.BlockSpec(
        (t, width), lambda b, s, it_, jt_, ft_: (b * nq + it_[s], col))
    krow = lambda col: pl.BlockSpec((t, LANES), lambda b, s, it_, jt_, ft_: (b * nq + jt_[s], col))
    in_specs = [_smem(), qrow(D_MODEL, 0), krow(kcol), krow(kcol + 1), qrow(LANES, C_GATE_COL),
                qrow(D_MODEL, 0)]
    args = [it, jt, ft, slopes, p, p, p, p, prev]
    nch = 0
    if mode == "sel":
        nch = sel.shape[1] // C_GROUPS
        in_specs.append(qrow(C_GROUPS * nch, 0))
        args.append(sel)
    return pl.pallas_call(
        functools.partial(_nsa_flash_kernel, t=t, mode=mode, nch=nch),
        grid_spec=pltpu.PrefetchScalarGridSpec(
            num_scalar_prefetch=3,
            grid=(batch, len(it)),
            in_specs=in_specs,
            out_specs=qrow(D_MODEL, 0),
            scratch_shapes=_flash_scratch(t)),
        out_shape=jax.ShapeDtypeStruct((p.shape[0], D_MODEL), F32),
        compiler_params=_params("parallel", "arbitrary"),
    )(*args)


def _decode_kernel(pt_ref, q_ref, kn_ref, vn_ref, *rest, n_pages, mode, lam_init):
    del pt_ref
    if mode == "moba":
        slc_ref, kpool_ref, vpool_ref, o_ref, qbd_scr, st_scr, p_scr, acc_scr, bm_scr = rest
    elif mode == "diff":
        slc_ref, lam_ref, g_ref, kpool_ref, vpool_ref, o_ref, qbd_scr, st_scr, p_scr, acc_scr = rest
    else:
        lft_ref, lfn_ref, kpool_ref, vpool_ref, o_ref, qbd_scr, st_scr, p_scr, acc_scr, bias_scr, carry_scr = rest
    s = pl.program_id(1)
    past = n_pages * PAGE
    row = _iota((N_HEADS, D_MODEL), 0)
    lane = _iota((N_HEADS, D_MODEL), 1)

    @pl.when(s == 0)
    def _():
        qbd_scr[...] = jnp.where(lane // HEAD_DIM == row, jnp.broadcast_to(q_ref[0], (N_HEADS, D_MODEL)), 0.0)
        if mode == "moba":
            bm_scr[...] = jnp.zeros(bm_scr.shape, F32)
        if mode == "fox":
            carry_scr[...] = jnp.zeros(carry_scr.shape, F32)

    @pl.when(s < n_pages)
    def _():
        kp = kpool_ref[0]
        off = pl.multiple_of(s * PAGE, PAGE)
        st_scr[:, pl.ds(off, PAGE)] = _dot_nt((qbd_scr[...] * SCALE).astype(BF16), kp.astype(BF16))
        if mode == "moba":
            blk = pl.ds(s // (A_BLOCK // PAGE), 1)
            bm_scr[blk, :] = bm_scr[blk, :] + jnp.sum(kp, axis=0, keepdims=True)
        if mode == "fox":
            lf = lft_ref[0]
            tri = (_iota((PAGE, PAGE), 0) <= _iota((PAGE, PAGE), 1)).astype(F32)
            c = jnp.dot(lf, tri, precision=HIGHEST, preferred_element_type=F32) + carry_scr[...]
            bias_scr[:, pl.ds(off, PAGE)] = -c
            carry_scr[...] = c[:, PAGE - 1:PAGE]

    @pl.when(s == n_pages - 1)
    def _():
        qb = qbd_scr[...]
        s_new = jnp.sum(qb * SCALE * jnp.broadcast_to(kn_ref[0], qb.shape), axis=1, keepdims=True)
        if mode == "fox":
            s_all = st_scr[...] + bias_scr[...]
            s_new = s_new - (carry_scr[...] + lfn_ref[0])
        else:
            kpos = _iota((1, past), 1)
            s_all = st_scr[...] + slc_ref[...] * (kpos - past).astype(F32)
        if mode == "moba":
            own = past // A_BLOCK
            bm_scr[own:own + 1, :] = bm_scr[own:own + 1, :] + kn_ref[0]
            lane_b = _iota((N_HEADS, LANES), 1)
            sb = _dot_nt(qb.astype(BF16), (bm_scr[...] / A_BLOCK).astype(BF16))
            sel = _topk_mask(sb, lane_b < past // A_BLOCK, lane_b.astype(F32), A_TOPK)
            st_scr[...] = s_all
            for n in range(past // A_BLOCK):
                cols = slice(n * A_BLOCK, (n + 1) * A_BLOCK)
                st_scr[:, cols] = st_scr[:, cols] + jnp.where(sel[:, n:n + 1] > 0.5, 0.0, NEG)
            s_all = st_scr[...]
        m = jnp.maximum(jnp.max(s_all, axis=1, keepdims=True), s_new)
        p = jnp.exp(s_all - m)
        pn = jnp.exp(s_new - m)
        l = jnp.sum(p, axis=1, keepdims=True) + pn
        p_scr[...] = p / l
        acc_scr[...] = (pn / l) * jnp.broadcast_to(vn_ref[0], (N_HEADS, D_MODEL))

    @pl.when(s >= n_pages)
    def _():
        off = pl.multiple_of((s - n_pages) * PAGE, PAGE)
        acc_scr[...] += _dot(p_scr[:, pl.ds(off, PAGE)].astype(BF16), vpool_ref[0].astype(BF16))

    @pl.when(s == 2 * n_pages - 1)
    def _():
        acc = acc_scr[...]
        if mode == "diff":
            lam = _lambda(lam_ref, lam_init)
            signed = jnp.where(row % 2 == 0, acc, -lam * acc)
            signed = jnp.where(lane // LANES == row // 2, signed, 0.0)
            o8 = jnp.where(row % 2 == 0, signed + pltpu.roll(signed, N_HEADS - 1, 0), 0.0)
            ms = jnp.sum(o8 * o8, axis=1, keepdims=True) / LANES
            o8 = o8 * lax.rsqrt(ms + LN_EPS)
            o_ref[0] = jnp.sum(o8, axis=0, keepdims=True) * g_ref[...] * (1.0 - lam_init)
        else:
            o_ref[0] = jnp.sum(jnp.where(lane // HEAD_DIM == row, acc, 0.0), axis=0, keepdims=True)


def _decode(ps3, page_table, kpool, vpool, mode, extras, lam_init=0.0):
    b = ps3.shape[0]
    n_pages = page_table.shape[1]
    assert n_pages % 2 == 0
    past = n_pages * PAGE
    pt = page_table.reshape(-1)
    rowspec = lambda col: pl.BlockSpec((1, 1, D_MODEL), lambda bi, s, pt_: (bi, 0, col))
    full = lambda shape: pl.BlockSpec(shape, lambda bi, s, pt_: tuple(0 for _ in shape))
    kspec = pl.BlockSpec((1, PAGE, D_MODEL),
                         lambda bi, s, pt_: (pt_[bi * n_pages + jnp.minimum(s, n_pages - 1)], 0, 0))
    vspec = pl.BlockSpec((1, PAGE, D_MODEL),
                         lambda bi, s, pt_: (pt_[bi * n_pages + jnp.maximum(s - n_pages, 0)], 0, 0))
    scratch = [pltpu.VMEM((N_HEADS, D_MODEL), F32), pltpu.VMEM((N_HEADS, past), F32),
               pltpu.VMEM((N_HEADS, past), F32), pltpu.VMEM((N_HEADS, D_MODEL), F32)]
    if mode == "moba":
        in_specs = [full((N_HEADS, 1))]
        scratch += [pltpu.VMEM((LANES, D_MODEL), F32)]
    elif mode == "diff":
        in_specs = [full((N_HEADS, 1)), full((4, HEAD_DIM)), full((1, D_MODEL))]
    else:
        in_specs = [pl.BlockSpec((1, N_HEADS, PAGE),
                                 lambda bi, s, pt_: (pt_[bi * n_pages + jnp.minimum(s, n_pages - 1)], 0, 0)),
                    pl.BlockSpec((1, N_HEADS, 1), lambda bi, s, pt_: (bi, 0, 0))]
        scratch += [pltpu.VMEM((N_HEADS, past), F32), pltpu.VMEM((N_HEADS, 1), F32)]
    return pl.pallas_call(
        functools.partial(_decode_kernel, n_pages=n_pages, mode=mode, lam_init=lam_init),
        grid_spec=pltpu.PrefetchScalarGridSpec(
            num_scalar_prefetch=1,
            grid=(b, 2 * n_pages),
            in_specs=[rowspec(0), rowspec(1), rowspec(2)] + in_specs + [kspec, vspec],
            out_specs=pl.BlockSpec((1, 1, D_MODEL), lambda bi, s, pt_: (bi, 0, 0)),
            scratch_shapes=scratch),
        out_shape=jax.ShapeDtypeStruct((b, 1, D_MODEL), F32),
        compiler_params=_params("parallel", "arbitrary"),
    )(pt, ps3, ps3, ps3, *extras, kpool, vpool)


C_HALF = 64


def _softmax_new(s_all, s_new, v_all, v_new):
    m = jnp.maximum(jnp.max(s_all, axis=1, keepdims=True), s_new)
    p = jnp.exp(s_all - m)
    pn = jnp.exp(s_new - m)
    l = jnp.sum(p, axis=1, keepdims=True) + pn
    return (_dot(p.astype(BF16), v_all.astype(BF16)) + pn * v_new) / l


def _nsa_decode_kernel(pt_ref, qbd_ref, g_ref, slc_ref, kcn_ref, vcn_ref, ksn_ref, vsn_ref, kwn_ref, vwn_ref,
                       ckp_ref, cvp_ref, ksp_ref, vsp_ref, kw_ref, vw_ref, wk0_ref, wv0_ref, o_ref,
                       ck_scr, cv_scr, ks_scr, vs_scr, *, n_pages):
    del pt_ref
    s = pl.program_id(1)
    past = n_pages * PAGE
    per_page = PAGE // C_CMP_BLOCK // 2

    @pl.when(s == 0)
    def _():
        ck_scr[...] = jnp.zeros(ck_scr.shape, F32)
        cv_scr[...] = jnp.zeros(cv_scr.shape, F32)

    off = pl.multiple_of(s * PAGE, PAGE)
    ks_scr[pl.ds(off, PAGE), :] = ksp_ref[0]
    vs_scr[pl.ds(off, PAGE), :] = vsp_ref[0]
    for u in range(2 * per_page):
        dst = (u % 2) * C_HALF + per_page * s + u // 2
        ck_scr[pl.ds(dst, 1), :] = ckp_ref[0, u:u + 1, :]
        cv_scr[pl.ds(dst, 1), :] = cvp_ref[0, u:u + 1, :]

    @pl.when(s == n_pages - 1)
    def _():
        qf = qbd_ref[0] * SCALE
        qb = qf.astype(BF16)
        slc = slc_ref[...]
        new_tok = per_page * n_pages
        ck_scr[new_tok:new_tok + 1, :] = _dot(jnp.broadcast_to(kcn_ref[0], (8, LANES)).astype(BF16), wk0_ref[...])[0:1]
        cv_scr[new_tok:new_tok + 1, :] = _dot(jnp.broadcast_to(vcn_ref[0], (8, LANES)).astype(BF16), wv0_ref[...])[0:1]
        lane = _iota((N_HEADS, LANES), 1)
        row = _iota((N_HEADS, LANES), 0)
        tok = jnp.where(lane < C_HALF, 2 * lane, 2 * (lane - C_HALF) + 1)
        endp = (tok + 1) * C_CMP_BLOCK - 1
        okc = endp <= past
        sc = _dot_nt(qb, ck_scr[...].astype(BF16)) + slc * (endp - past).astype(F32)
        sc = jnp.where(okc, sc, NEG)
        pc = jnp.where(okc, jnp.exp(sc - jnp.max(sc, axis=1, keepdims=True)), 0.0)
        pc = pc / jnp.maximum(jnp.sum(pc, axis=1, keepdims=True), TINY)
        o_cmp = _dot(pc.astype(BF16), cv_scr[...].astype(BF16))
        imp = jnp.where(row < C_RATIO, jnp.sum(pc[0:C_RATIO], axis=0, keepdims=True),
                        jnp.sum(pc[C_RATIO:N_HEADS], axis=0, keepdims=True))
        imp = imp + pltpu.roll(imp, C_HALF, 1)
        sel = _topk_mask(imp, lane < past // C_SEL_BLOCK, lane.astype(F32), C_TOPK)
        expand = (_iota((LANES, past), 0) == _iota((LANES, past), 1) // C_SEL_BLOCK).astype(BF16)
        picked = _dot(sel.astype(BF16), expand)
        kpos = _iota((1, past), 1)
        ss = _dot_nt(qb, ks_scr[...].astype(BF16)) + slc * (kpos - past).astype(F32)
        ss = jnp.where(picked > 0.5, ss, NEG)
        ss_new = jnp.sum(qf * ksn_ref[0], axis=1, keepdims=True)
        o_sel = _softmax_new(ss, ss_new, vs_scr[...], vsn_ref[0])
        wb = kw_ref.shape[1]
        wpos = past - wb + _iota((1, wb), 1)
        okw = (past - wpos <= C_WINDOW) & (wpos >= 0)
        sw = _dot_nt(qb, kw_ref[0].astype(BF16)) + slc * (wpos - past).astype(F32)
        sw = jnp.where(okw, sw, NEG)
        sw_new = jnp.sum(qf * kwn_ref[0], axis=1, keepdims=True)
        o_win = _softmax_new(sw, sw_new, vw_ref[0], vwn_ref[0])
        gate = jax.nn.sigmoid(g_ref[0])
        o_ref[0] = gate[:, 0:1] * o_cmp + gate[:, 1:2] * o_sel + gate[:, 2:3] * o_win


def _nsa_decode(ps3, qbd, gates, slc, page_table, ck_pool, cv_pool, ks_pool, vs_pool, kw_buf, vw_buf, wk0, wv0):
    b = ps3.shape[0]
    n_pages = page_table.shape[1]
    past = n_pages * PAGE
    assert 2 * n_pages + 1 <= C_HALF
    pt = page_table.reshape(-1)
    per_b = lambda shape: pl.BlockSpec((1,) + shape, lambda bi, s, pt_: (bi,) + tuple(0 for _ in shape))
    full = lambda shape: pl.BlockSpec(shape, lambda bi, s, pt_: tuple(0 for _ in shape))
    newrow = lambda col: pl.BlockSpec((1, 1, LANES), lambda bi, s, pt_: (bi, 0, col))
    paged = lambda rows: pl.BlockSpec((1, rows, LANES), lambda bi, s, pt_: (pt_[bi * n_pages + s], 0, 0))
    return pl.pallas_call(
        functools.partial(_nsa_decode_kernel, n_pages=n_pages),
        grid_spec=pltpu.PrefetchScalarGridSpec(
            num_scalar_prefetch=1,
            grid=(b, n_pages),
            in_specs=[per_b((N_HEADS, LANES)), per_b((N_HEADS, 3)), full((N_HEADS, 1))]
                     + [newrow(C_KV_COL + c) for c in range(6)]
                     + [paged(PAGE // C_CMP_BLOCK), paged(PAGE // C_CMP_BLOCK), paged(PAGE), paged(PAGE),
                        per_b(kw_buf.shape[1:]), per_b(vw_buf.shape[1:]), full((LANES, LANES)), full((LANES, LANES))],
            out_specs=per_b((N_HEADS, LANES)),
            scratch_shapes=[pltpu.VMEM((2 * C_HALF, LANES), F32), pltpu.VMEM((2 * C_HALF, LANES), F32),
                            pltpu.VMEM((past, LANES), F32), pltpu.VMEM((past, LANES), F32)]),
        out_shape=jax.ShapeDtypeStruct((b, N_HEADS, LANES), F32),
        compiler_params=_params("parallel", "arbitrary"),
    )(pt, qbd, gates, slc, ps3, ps3, ps3, ps3, ps3, ps3, ck_pool, cv_pool, ks_pool, vs_pool, kw_buf, vw_buf, wk0, wv0)


def _pad_cols(w, width):
    return jnp.pad(w, ((0, 0), (0, width - w.shape[1])))


def _block_diag_cmp(w):
    w3 = w.reshape(C_CMP_BLOCK, HEAD_DIM, HEAD_DIM)
    z = jnp.zeros_like(w3)
    return jnp.concatenate([jnp.concatenate([w3, z], 2), jnp.concatenate([z, w3], 2)], 1).astype(BF16)


def kernel(x_prompt, x_sample, cache_a_k, cache_a_v, cache_b_k, cache_b_v, cache_c_kc, cache_c_vc, cache_c_ks, cache_c_vs, state_c_kw, state_c_vw, cache_d_k, cache_d_v, cache_d_logf, page_table, w_in_0, w_out_0, ln_g_0, ln_b_0, w_in_1, lam_q1_1, lam_k1_1, lam_q2_1, lam_k2_1, subln_g_1, w_out_1, ln_g_1, ln_b_1, w_in_2, w_cmp_k_2, w_cmp_v_2, w_out_2, ln_g_2, ln_b_2, w_in_3, b_f_3, w_out_3, ln_g_3, ln_b_3):
    bp, sp, _ = x_prompt.shape
    bs = x_sample.shape[0]
    assert x_sample.shape[1] == 1 and sp % C_WINDOW == 0
    n_pool = cache_a_k.shape[0]
    mp = bp * sp
    tm = 512
    t_dense = 512
    xp = x_prompt.reshape(mp, D_MODEL)
    xs = x_sample.reshape(bs, D_MODEL)
    heads = lambda a, b, l, h: a.reshape(b, l, h, -1)
    pool2 = lambda c: c.reshape(n_pool, PAGE, -1)

    w = w_in_0.astype(BF16)
    slopes16 = _alibi_slopes(N_HEADS)
    pp = _matmul(xp, w, tm)
    ps = _matmul(xs, w, bs)
    a_k_p, a_v_p = heads(pp[:, 1024:2048], bp, sp, 16), heads(pp[:, 2048:3072], bp, sp, 16)
    a_k_s, a_v_s = heads(ps[:, 1024:2048], bs, 1, 16), heads(ps[:, 2048:3072], bs, 1, 16)
    o_p = _moba_prompt(pp, bp, slopes16)
    o_s = _decode(ps.reshape(bs, 1, -1), page_table, pool2(cache_a_k), pool2(cache_a_v), "moba",
                  [slopes16.reshape(-1, 1)]).reshape(bs, D_MODEL)
    w_o = w_out_0.astype(BF16)
    xp = _out_ln(o_p, pp, 3, xp, w_o, ln_g_0, ln_b_0, tm)
    xs = _out_ln(o_s, ps, 3, xs, w_o, ln_g_0, ln_b_0, bs)

    lam_init = 0.8 - 0.6 * math.exp(-0.3 * 1)
    w = w_in_1.astype(BF16)
    slopes8 = _alibi_slopes(B_HEADS)
    lamv = jnp.stack([lam_q1_1, lam_k1_1, lam_q2_1, lam_k2_1])
    pp = _matmul(xp, w, tm)
    ps = _matmul(xs, w, bs)
    b_k_p, b_v_p = heads(pp[:, 1024:2048], bp, sp, 16), heads(pp[:, 2048:3072], bp, sp, 8)
    b_k_s, b_v_s = heads(ps[:, 1024:2048], bs, 1, 16), heads(ps[:, 2048:3072], bs, 1, 8)
    o_p = _diff_prompt(pp, bp, slopes8, lamv, subln_g_1, lam_init, t_dense)
    o_s = _decode(ps.reshape(bs, 1, -1), page_table, pool2(cache_b_k), pool2(cache_b_v), "diff",
                  [jnp.repeat(slopes8, 2).reshape(-1, 1), lamv, jnp.tile(subln_g_1, B_HEADS).reshape(1, -1)],
                  lam_init).reshape(bs, D_MODEL)
    w_o = w_out_1.astype(BF16)
    xp = _out_ln(o_p, pp, 3, xp, w_o, ln_g_1, ln_b_1, tm)
    xs = _out_ln(o_s, ps, 3, xs, w_o, ln_g_1, ln_b_1, bs)

    kv0, z0, g0 = D_MODEL, D_MODEL + 6 * LANES, 2 * D_MODEL + 6 * LANES
    w = _pad_cols(jnp.concatenate([w_in_2[:, :kv0], w_in_2[:, z0:g0], w_in_2[:, kv0:z0], w_in_2[:, g0:]], 1),
                  C_WIDTH).astype(BF16)
    w2k, w2v = _block_diag_cmp(w_cmp_k_2), _block_diag_cmp(w_cmp_v_2)
    pp = _matmul(xp, w, tm)
    ps = _matmul(xs, w, bs)
    kvcol = lambda p, c: p[:, 2 * D_MODEL + c * LANES:2 * D_MODEL + (c + 1) * LANES]
    c_p = [heads(kvcol(pp, c), bp, sp, 2) for c in range(6)]
    c_s = [heads(kvcol(ps, c), bs, 1, 2) for c in range(6)]
    keep = min(C_WINDOW, sp)
    ck, cv = _compress_prompt(pp, bp, w2k, w2v)
    o_p, sel = _nsa_cmp(pp, bp, slopes16, ck, cv, 256)
    o_p = _nsa_flash(pp, bp, slopes16, o_p, sel, "sel", t_dense)
    o_p = _nsa_flash(pp, bp, slopes16, o_p, None, "win", C_WINDOW)
    tokens = PAGE // C_CMP_BLOCK
    flat = lambda c: c.reshape(n_pool * tokens, C_CMP_BLOCK * LANES)
    ck_pool = _matmul(flat(cache_c_kc), w2k.reshape(-1, LANES), 512).reshape(n_pool, tokens, LANES)
    cv_pool = _matmul(flat(cache_c_vc), w2v.reshape(-1, LANES), 512).reshape(n_pool, tokens, LANES)
    q_s = ps[:, :D_MODEL].reshape(bs, N_HEADS, 1, HEAD_DIM)
    in_group = (jnp.arange(N_HEADS) // C_RATIO)[:, None] == jnp.arange(C_GROUPS)[None, :]
    qbd = jnp.where(in_group[None, :, :, None], q_s, 0.0).reshape(bs, N_HEADS, LANES)
    gates_s = ps[:, C_GATE_COL * LANES:C_GATE_COL * LANES + 3 * N_HEADS].reshape(bs, N_HEADS, 3)
    wb = state_c_kw.shape[1]
    o16 = _nsa_decode(ps.reshape(bs, 1, -1), qbd, gates_s, slopes16.reshape(-1, 1), page_table, ck_pool, cv_pool,
                      pool2(cache_c_ks), pool2(cache_c_vs), state_c_kw.reshape(bs, wb, LANES),
                      state_c_vw.reshape(bs, wb, LANES), w2k[0], w2v[0])
    o16 = o16.reshape(bs, N_HEADS, C_GROUPS, HEAD_DIM)
    o_s = jnp.where(in_group[None, :, :, None], o16, 0.0).sum(2).reshape(bs, D_MODEL)
    c_kw_s = jnp.concatenate([state_c_kw, c_s[4]], 1)[:, -min(C_WINDOW, wb + 1):]
    c_vw_s = jnp.concatenate([state_c_vw, c_s[5]], 1)[:, -min(C_WINDOW, wb + 1):]
    w_o = w_out_2.astype(BF16)
    xp = _out_ln(o_p, pp, C_Z_COL, xp, w_o, ln_g_2, ln_b_2, tm)
    xs = _out_ln(o_s, ps, C_Z_COL, xs, w_o, ln_g_2, ln_b_2, bs)

    f_col = 4 * D_MODEL // LANES
    w = _pad_cols(w_in_3, 4 * D_MODEL + LANES).astype(BF16)
    pp = _matmul(xp, w, tm)
    ps = _matmul(xs, w, bs)
    d_k_p, d_v_p = heads(pp[:, 1024:2048], bp, sp, 16), heads(pp[:, 2048:3072], bp, sp, 16)
    d_k_s, d_v_s = heads(ps[:, 1024:2048], bs, 1, 16), heads(ps[:, 2048:3072], bs, 1, 16)
    lf_p = _logf(pp, f_col, b_f_3, tm).reshape(bp, sp, N_HEADS)
    lf_s = _logf(ps, f_col, b_f_3, bs).reshape(bs, 1, N_HEADS)
    c_t = _cumsum_lanes(lf_p.transpose(0, 2, 1), t_dense)
    o_p = _fox_prompt(pp, bp, c_t, t_dense)
    o_s = _decode(ps.reshape(bs, 1, -1), page_table, pool2(cache_d_k), pool2(cache_d_v), "fox",
                  [cache_d_logf.transpose(0, 2, 1), lf_s.reshape(bs, N_HEADS, 1)]).reshape(bs, D_MODEL)
    w_o = w_out_3.astype(BF16)
    xp = _out_ln(o_p, pp, 3, xp, w_o, ln_g_3, ln_b_3, tm)
    xs = _out_ln(o_s, ps, 3, xs, w_o, ln_g_3, ln_b_3, bs)

    return (xp.reshape(bp, sp, D_MODEL), xs.reshape(bs, 1, D_MODEL),
            a_k_p, a_v_p, a_k_s, a_v_s, b_k_p, b_v_p, b_k_s, b_v_s,
            c_p[0], c_p[1], c_p[2], c_p[3], c_p[4][:, -keep:], c_p[5][:, -keep:],
            c_s[0], c_s[1], c_s[2], c_s[3], c_kw_s, c_vw_s,
            d_k_p, d_v_p, lf_p, d_k_s, d_v_s, lf_s)
```

```python
import functools
import math

import numpy as np
import jax
import jax.numpy as jnp
from jax import lax
from jax.experimental import pallas as pl
from jax.experimental.pallas import tpu as pltpu

F32 = jnp.float32
BF16 = jnp.bfloat16
HIGHEST = lax.Precision.HIGHEST

D_MODEL = 1024
HEAD_DIM = 64
N_HEADS = 16
LANES = 128
SCALE = HEAD_DIM ** -0.5
PAGE = 128
DEPTH = 4
ALPHA = (2 * DEPTH) ** 0.25
LN_EPS = 1e-5
NEG = -1e30
TINY = 1e-30
A_BLOCK = 256
A_TOPK = 3
B_HEADS = 8
C_GROUPS = 2
C_RATIO = 8
C_CMP_BLOCK = 32
C_SEL_BLOCK = 64
C_TOPK = 4
C_WINDOW = 512
PAGES_PER_STEP = 4
VMEM_LIMIT = 56 * 1024 * 1024


def _dot_nt(a, b):
    return lax.dot_general(a, b, (((1,), (1,)), ((), ())), preferred_element_type=F32)


def _dot(a, b):
    return jnp.dot(a, b, preferred_element_type=F32)


def _iota(shape, dim):
    return lax.broadcasted_iota(jnp.int32, shape, dim)


def _params(*sem):
    return pltpu.CompilerParams(dimension_semantics=sem, vmem_limit_bytes=VMEM_LIMIT)


def _alibi_slopes(n):
    return jnp.exp2(-8.0 * jnp.arange(1, n + 1, dtype=F32) / n)


def _smem():
    return pl.BlockSpec(memory_space=pltpu.SMEM)


def _mm_kernel(x_ref, w_ref, o_ref):
    o_ref[...] = _dot(x_ref[...].astype(BF16), w_ref[...])


def _pick_tn(n):
    best = LANES
    for t in range(LANES, 1536 + 1, LANES):
        if n % t == 0:
            best = t
    return best


def _matmul(x, w, tm):
    m, k = x.shape
    n = w.shape[1]
    tm = min(tm, m)
    assert m % tm == 0
    tn = _pick_tn(n)
    return pl.pallas_call(
        _mm_kernel,
        grid=(m // tm, n // tn),
        in_specs=[pl.BlockSpec((tm, k), lambda i, j: (i, 0)),
                  pl.BlockSpec((k, tn), lambda i, j: (0, j))],
        out_specs=pl.BlockSpec((tm, tn), lambda i, j: (i, j)),
        out_shape=jax.ShapeDtypeStruct((m, n), F32),
        compiler_params=_params("parallel", "arbitrary"),
    )(x, w)


def _vt_kernel(x_ref, w_ref, o_ref):
    o_ref[...] = _dot_nt(w_ref[...], x_ref[...].astype(BF16)).astype(BF16)


def _matmul_t(x, wt, tm):
    m, k = x.shape
    n = wt.shape[0]
    return pl.pallas_call(
        _vt_kernel,
        grid=(m // tm,),
        in_specs=[pl.BlockSpec((tm, k), lambda i: (i, 0)), pl.BlockSpec((n, k), lambda i: (0, 0))],
        out_specs=pl.BlockSpec((n, tm), lambda i: (0, i)),
        out_shape=jax.ShapeDtypeStruct((n, m), BF16),
        compiler_params=_params("parallel"),
    )(x, wt)


def _out_ln_kernel(o_ref, z_ref, x_ref, w_ref, g_ref, b_ref, y_ref):
    z = z_ref[...]
    a = (o_ref[...] * (z * jax.nn.sigmoid(z))).astype(BF16)
    h = ALPHA * x_ref[...] + _dot(a, w_ref[...])
    hc = h - jnp.mean(h, axis=-1, keepdims=True)
    var = jnp.mean(hc * hc, axis=-1, keepdims=True)
    y_ref[...] = hc * lax.rsqrt(var + LN_EPS) * g_ref[...] + b_ref[...]


def _out_ln(o, p, z_col, x, w, g, b, tm):
    m = x.shape[0]
    row = lambda i: (i, 0)
    return pl.pallas_call(
        _out_ln_kernel,
        grid=(m // tm,),
        in_specs=[pl.BlockSpec((tm, D_MODEL), row),
                  pl.BlockSpec((tm, D_MODEL), lambda i: (i, z_col)),
                  pl.BlockSpec((tm, D_MODEL), row),
                  pl.BlockSpec((D_MODEL, D_MODEL), lambda i: (0, 0)),
                  pl.BlockSpec((1, D_MODEL), lambda i: (0, 0)),
                  pl.BlockSpec((1, D_MODEL), lambda i: (0, 0))],
        out_specs=pl.BlockSpec((tm, D_MODEL), row),
        out_shape=jax.ShapeDtypeStruct((m, D_MODEL), F32),
        compiler_params=_params("parallel"),
    )(o, p, x, w, g.reshape(1, -1), b.reshape(1, -1))


def _topk_mask(s, cand, lane_f, k):
    s = jnp.where(cand, s, NEG)
    sel = jnp.zeros(s.shape, F32)
    for _ in range(k):
        mx = jnp.max(s, axis=1, keepdims=True)
        idx = jnp.min(jnp.where(s == mx, lane_f, 1e9), axis=1, keepdims=True)
        pick = lane_f == idx
        valid = jnp.where(mx > 0.5 * NEG, 1.0, 0.0)
        sel = jnp.where(pick, valid, sel)
        s = jnp.where(pick, -3e38, s)
    return sel


def _lambda(lam_ref, lam_init):
    a = lam_ref[...]
    return (jnp.exp(jnp.sum(a[0:1] * a[1:2], axis=1, keepdims=True))
            - jnp.exp(jnp.sum(a[2:3] * a[3:4], axis=1, keepdims=True)) + lam_init)


def _block_mean_kernel(k_ref, o_ref):
    n = pl.program_id(1)

    @pl.when(n == 0)
    def _():
        o_ref[...] = jnp.zeros(o_ref.shape, F32)

    o_ref[0, pl.ds(n, 1), :] = jnp.sum(k_ref[...], axis=0, keepdims=True) / A_BLOCK


def _block_mean(p, batch, k_col):
    nb = p.shape[0] // batch // A_BLOCK
    return pl.pallas_call(
        _block_mean_kernel,
        grid=(batch, nb),
        in_specs=[pl.BlockSpec((A_BLOCK, D_MODEL), lambda b, n: (b * nb + n, k_col))],
        out_specs=pl.BlockSpec((1, LANES, D_MODEL), lambda b, n: (b, 0, 0)),
        out_shape=jax.ShapeDtypeStruct((batch, LANES, D_MODEL), F32),
        compiler_params=_params("parallel", "arbitrary"),
    )(p)


LOG2E = math.log2(math.e)
AUX0 = HEAD_DIM
SEL0 = 96
Q_SCALE = SCALE * LOG2E


def _split3(x):
    hi = x.astype(BF16).astype(F32)
    mid = (x - hi).astype(BF16).astype(F32)
    return hi, mid, x - hi - mid


def _lane_pick(lane, base, vals):
    out = jnp.zeros(lane.shape, F32)
    for idx, v in enumerate(vals):
        out = jnp.where(lane == base + idx, v, out)
    return out


def _slope_lanes(lane, sl_ref, h):
    parts = [sl_ref[h, c] for c in range(3)]
    return _lane_pick(lane, AUX0, parts + parts)


def _pos_lanes(lane, kpos):
    hi = kpos.astype(BF16).astype(F32)
    lo = kpos - hi
    return _lane_pick(lane, AUX0, [hi, hi, hi, lo, lo, lo])


def _head_cols(ref, h):
    x = ref[:, (h // 2) * LANES:(h // 2 + 1) * LANES]
    return x if h % 2 == 0 else pltpu.roll(x, HEAD_DIM, 1)


def _prep_moba_kernel(sl_ref, q_ref, k_ref, bm_ref, qa_ref, ka_ref, *, tm):
    i = pl.program_id(1)
    lane = _iota((tm, LANES), 1)
    lane_f = lane.astype(F32)
    low = lane < HEAD_DIM
    kpos = (i * tm + _iota((tm, LANES), 0)).astype(F32)
    k_aux = jnp.where(lane >= SEL0, jnp.where(lane == SEL0 + i, 1.0, 0.0), _pos_lanes(lane, kpos))
    bm_low = _iota((LANES, LANES), 1) < HEAD_DIM
    for h in range(N_HEADS):
        cs = slice(h * LANES, (h + 1) * LANES)
        q = _head_cols(q_ref, h)
        bm = bm_ref[0, :, (h // 2) * LANES:(h // 2 + 1) * LANES]
        bme = jnp.where(bm_low if h % 2 == 0 else ~bm_low, bm, 0.0).astype(BF16)
        qsel = q_ref[:, (h // 2) * LANES:(h // 2 + 1) * LANES].astype(BF16)
        sel = _topk_mask(_dot_nt(qsel, bme), lane < i, lane_f, A_TOPK)
        penalty = pltpu.roll(jnp.where((sel > 0.5) | (lane == i), 0.0, NEG), SEL0, 1)
        q_aux = jnp.where(lane >= SEL0, penalty, _slope_lanes(lane, sl_ref, h))
        qa_ref[:, cs] = jnp.where(low, q * Q_SCALE, q_aux).astype(BF16)
        ka_ref[:, cs] = jnp.where(low, _head_cols(k_ref, h), k_aux).astype(BF16)


def _prep_moba(p, batch, sl3, bm):
    tm = A_BLOCK
    nt = p.shape[0] // batch // tm
    assert nt <= LANES - SEL0
    row = lambda col: pl.BlockSpec((tm, D_MODEL), lambda b, i: (b * nt + i, col))
    aug = pl.BlockSpec((tm, N_HEADS * LANES), lambda b, i: (b * nt + i, 0))
    shape = jax.ShapeDtypeStruct((p.shape[0], N_HEADS * LANES), BF16)
    return pl.pallas_call(
        functools.partial(_prep_moba_kernel, tm=tm),
        grid=(batch, nt),
        in_specs=[_smem(), row(0), row(1), pl.BlockSpec((1, LANES, D_MODEL), lambda b, i: (b, 0, 0))],
        out_specs=[aug, aug],
        out_shape=[shape, shape],
        compiler_params=_params("parallel", "parallel"),
    )(sl3, p, p, bm)


def _prep_kernel(*refs, tm, mode):
    if mode == "diff":
        sl_ref, q_ref, k_ref, qa_ref, ka_ref = refs
    elif mode == "fox":
        q_ref, k_ref, c_ref, qa_ref, ka_ref = refs
    elif mode == "win":
        sl_ref, q_ref, k_ref, qa_ref, ka_ref = refs
    else:
        sl_ref, q_ref, k_ref, sel_ref, qa_ref, ka_ref = refs
    i = pl.program_id(1)
    lane = _iota((tm, LANES), 1)
    low = lane < HEAD_DIM
    kpos = (i * tm + _iota((tm, LANES), 0)).astype(F32)
    cw = 2 * LANES if mode == "sel" else LANES
    for h in range(N_HEADS):
        if mode == "fox":
            q_aux = _lane_pick(lane, AUX0, [-1.0, -1.0, -1.0])
        else:
            q_aux = _slope_lanes(lane, sl_ref, h)
        qa_ref[:, h * cw:h * cw + LANES] = jnp.where(low, _head_cols(q_ref, h) * Q_SCALE, q_aux).astype(BF16)
        if mode == "sel":
            g = h // C_RATIO
            picked = sel_ref[:, g * LANES:(g + 1) * LANES]
            qa_ref[:, h * cw + LANES:(h + 1) * cw] = jnp.where(picked > 0.5, 0.0, NEG).astype(BF16)
        if mode == "diff":
            ka_ref[:, h * cw:(h + 1) * cw] = jnp.where(low, _head_cols(k_ref, h), _pos_lanes(lane, kpos)).astype(BF16)
        if mode == "fox":
            c = jnp.broadcast_to(c_ref[:, h:h + 1], (tm, LANES)) * LOG2E
            ka_ref[:, h * cw:(h + 1) * cw] = jnp.where(low, _head_cols(k_ref, h), _lane_pick(lane, AUX0, _split3(c))).astype(BF16)
    if mode in ("win", "sel"):
        for g in range(C_GROUPS):
            ka_ref[:, g * cw:g * cw + LANES] = jnp.where(low, _head_cols(k_ref, g), _pos_lanes(lane, kpos)).astype(BF16)
            if mode == "sel":
                own = (i * tm + _iota((tm, LANES), 0)) // C_SEL_BLOCK
                ka_ref[:, g * cw + LANES:(g + 1) * cw] = jnp.where(lane == own, 1.0, 0.0).astype(BF16)


def _prep(p, batch, mode, k_col, extras, tm):
    nt = p.shape[0] // batch // tm
    cw = 2 * LANES if mode == "sel" else LANES
    nk = C_GROUPS if mode in ("win", "sel") else N_HEADS
    kw = LANES if mode in ("win", "sel") else D_MODEL
    row = lambda width, col: pl.BlockSpec((tm, width), lambda b, i: (b * nt + i, col))
    in_specs = ([] if mode == "fox" else [_smem()]) + [row(D_MODEL, 0), row(kw, k_col)]
    if mode == "fox":
        in_specs.append(row(N_HEADS, 0))
    if mode == "sel":
        assert extras[-1].shape[1] == C_GROUPS * LANES
        in_specs.append(row(C_GROUPS * LANES, 0))
    return pl.pallas_call(
        functools.partial(_prep_kernel, tm=tm, mode=mode),
        grid=(batch, nt),
        in_specs=in_specs,
        out_specs=[row(N_HEADS * cw, 0), row(nk * cw, 0)],
        out_shape=[jax.ShapeDtypeStruct((p.shape[0], N_HEADS * cw), BF16),
                   jax.ShapeDtypeStruct((p.shape[0], nk * cw), BF16)],
        compiler_params=_params("parallel", "parallel"),
    )(*(extras[:1] if mode != "fox" else []), p, p, *(extras if mode == "fox" else extras[1:]))


def _sweep_tables(nq, band):
    it, jt, ft, mt = [], [], [], []
    for i in range(nq):
        js = [j for j in (i - 1, i) if j >= 0] if band else list(range(i + 1))
        for n, j in enumerate(js):
            it.append(i)
            jt.append(j)
            ft.append(1 if n == 0 else 0)
            mt.append(1 if j == i else (2 if band else 0))
    return [np.asarray(a, np.int32) for a in (it, jt, ft, mt)]


def _flash_t_kernel(it_ref, jt_ref, ft_ref, mt_ref, q_ref, k_ref, v_ref, *rest, t, cw, kdiv, vdiv, dv, fin, modes,
                    lam_init, gate_idx):
    if fin == "diff":
        lam_ref, g_ref, o_ref, m_scr, l_scr, acc_scr = rest
    elif fin == "nsa":
        g_ref, prev_ref, o_ref, m_scr, l_scr, acc_scr = rest
    else:
        o_ref, m_scr, l_scr, acc_scr = rest
    st = pl.program_id(1)

    @pl.when(ft_ref[st] == 1)
    def _():
        m_scr[...] = jnp.full(m_scr.shape, NEG, F32)
        l_scr[...] = jnp.zeros(l_scr.shape, F32)
        acc_scr[...] = jnp.zeros(acc_scr.shape, F32)

    def tile(mode):
        if mode:
            diff = _iota((t, t), 0) - _iota((t, t), 1)
            allowed = diff <= 0 if mode == 1 else diff >= 0

        def scores(h):
            kh = k_ref[:, (h // kdiv) * cw:(h // kdiv + 1) * cw]
            return _dot_nt(kh, q_ref[:, h * cw:(h + 1) * cw])

        s_next = scores(0)
        for h in range(N_HEADS):
            s = s_next
            if h + 1 < N_HEADS:
                s_next = scores(h + 1)
            if mode:
                s = jnp.where(allowed, s, NEG)
            m_prev = m_scr[h:h + 1, :]
            m_new = jnp.maximum(m_prev, jnp.max(s, axis=0, keepdims=True))
            p = jnp.exp2(s - m_new)
            corr = jnp.exp2(m_prev - m_new)
            l_scr[h:h + 1, :] = corr * l_scr[h:h + 1, :] + jnp.sum(p, axis=0, keepdims=True)
            m_scr[h:h + 1, :] = m_new
            rows = slice(h * dv, (h + 1) * dv)
            vh = v_ref[(h // vdiv) * dv:(h // vdiv + 1) * dv, :]
            acc_scr[rows, :] = acc_scr[rows, :] * corr + _dot(vh, p.astype(BF16))

    for mode in modes:
        pl.when(mt_ref[st] == mode)(functools.partial(tile, mode))

    @pl.when(it_ref[st] == jt_ref[st])
    def _():
        lo = _iota((t, LANES), 1) < HEAD_DIM
        for hp in range(N_HEADS // 2):
            cs = slice(hp * LANES, (hp + 1) * LANES)
            if fin == "diff":
                lam = _lambda(lam_ref, lam_init)
                o = (acc_scr[2 * hp * dv:(2 * hp + 1) * dv, :] / l_scr[2 * hp:2 * hp + 1, :]
                     - lam * (acc_scr[(2 * hp + 1) * dv:(2 * hp + 2) * dv, :] / l_scr[2 * hp + 1:2 * hp + 2, :]))
                o = o * lax.rsqrt(jnp.mean(o * o, axis=0, keepdims=True) + LN_EPS)
                o_ref[:, cs] = o.T * g_ref[...] * (1.0 - lam_init)
            else:
                o = jnp.concatenate(
                    [acc_scr[(2 * hp + e) * dv:(2 * hp + e + 1) * dv, :] / l_scr[2 * hp + e:2 * hp + e + 1, :]
                     for e in range(2)], axis=0).T
                if fin == "nsa":
                    g0 = jax.nn.sigmoid(g_ref[:, 6 * hp + gate_idx:6 * hp + gate_idx + 1])
                    g1 = jax.nn.sigmoid(g_ref[:, 6 * hp + 3 + gate_idx:6 * hp + 4 + gate_idx])
                    o = prev_ref[:, cs] + o * jnp.where(lo, g0, g1)
                o_ref[:, cs] = o


def _flash_t(qa, ka, vt, batch, *, fin, band=False, extras=(), lam_init=0.0, gate_idx=0, t=512):
    m = qa.shape[0]
    nq = m // batch // t
    cw = qa.shape[1] // N_HEADS
    kdiv = N_HEADS // (ka.shape[1] // cw)
    dv = LANES if fin == "diff" else HEAD_DIM
    vdiv = N_HEADS // (vt.shape[0] // dv)
    tabs = _sweep_tables(nq, band)
    modes = (1, 2) if band else (0, 1)
    assert not band or t == C_WINDOW
    imap = lambda f: (lambda b, s, it_, jt_, ft_, mt_: f(b, s, it_, jt_))
    qrow = lambda width: pl.BlockSpec((t, width), imap(lambda b, s, it_, jt_: (b * nq + it_[s], 0)))
    in_specs = [qrow(qa.shape[1]),
                pl.BlockSpec((t, ka.shape[1]), imap(lambda b, s, it_, jt_: (b * nq + jt_[s], 0))),
                pl.BlockSpec((vt.shape[0], t), imap(lambda b, s, it_, jt_: (0, b * nq + jt_[s])))]
    if fin == "diff":
        in_specs += [pl.BlockSpec((4, HEAD_DIM), imap(lambda b, s, it_, jt_: (0, 0))),
                     pl.BlockSpec((1, LANES), imap(lambda b, s, it_, jt_: (0, 0)))]
    if fin == "nsa":
        in_specs += [pl.BlockSpec((t, LANES), imap(lambda b, s, it_, jt_: (b * nq + it_[s], C_GATE_COL))),
                     qrow(D_MODEL)]
    return pl.pallas_call(
        functools.partial(_flash_t_kernel, t=t, cw=cw, kdiv=kdiv, vdiv=vdiv, dv=dv, fin=fin, modes=modes,
                          lam_init=lam_init, gate_idx=gate_idx),
        grid_spec=pltpu.PrefetchScalarGridSpec(
            num_scalar_prefetch=4,
            grid=(batch, len(tabs[0])),
            in_specs=in_specs,
            out_specs=qrow(D_MODEL),
            scratch_shapes=[pltpu.VMEM((N_HEADS, t), F32), pltpu.VMEM((N_HEADS, t), F32),
                            pltpu.VMEM((N_HEADS * dv, t), F32)]),
        out_shape=jax.ShapeDtypeStruct((m, D_MODEL), F32),
        compiler_params=_params("parallel", "arbitrary"),
    )(*tabs, qa, ka, vt, *extras)


def _logf_kernel(f_ref, b_ref, o_ref):
    x = f_ref[:, 0:N_HEADS] + b_ref[...]
    o_ref[...] = jnp.minimum(x, 0.0) - jnp.log1p(jnp.exp(-jnp.abs(x)))


def _logf(p, f_col, b_f, tm):
    m = p.shape[0]
    return pl.pallas_call(
        _logf_kernel,
        grid=(m // tm,),
        in_specs=[pl.BlockSpec((tm, LANES), lambda i: (i, f_col)),
                  pl.BlockSpec((1, N_HEADS), lambda i: (0, 0))],
        out_specs=pl.BlockSpec((tm, N_HEADS), lambda i: (i, 0)),
        out_shape=jax.ShapeDtypeStruct((m, N_HEADS), F32),
        compiler_params=_params("parallel"),
    )(p, b_f.reshape(1, -1))


def _cumsum_kernel(x_ref, o_ref, carry_scr, *, t):
    @pl.when(pl.program_id(1) == 0)
    def _():
        carry_scr[...] = jnp.zeros(carry_scr.shape, F32)

    x = x_ref[0]
    tri = (_iota((t, t), 0) <= _iota((t, t), 1)).astype(F32)
    c = jnp.dot(x, tri, precision=HIGHEST, preferred_element_type=F32) + carry_scr[...]
    o_ref[0] = c
    carry_scr[...] = c[:, t - 1:t]


def _cumsum_lanes(x, t):
    b, h, s = x.shape
    return pl.pallas_call(
        functools.partial(_cumsum_kernel, t=t),
        grid=(b, s // t),
        in_specs=[pl.BlockSpec((1, h, t), lambda bi, n: (bi, 0, n))],
        out_specs=pl.BlockSpec((1, h, t), lambda bi, n: (bi, 0, n)),
        out_shape=jax.ShapeDtypeStruct(x.shape, F32),
        scratch_shapes=[pltpu.VMEM((h, 1), F32)],
        compiler_params=_params("parallel", "arbitrary"),
    )(x)


C_Z_COL = 1
C_KV_COL = 16
C_GATE_COL = 22
C_WIDTH = 3072


def _compress_kernel(kc_ref, vc_ref, wk_ref, wv_ref, ck_ref, cv_ref, *, nch):
    stride = 2 * C_CMP_BLOCK
    for src, w_ref, dst in ((kc_ref, wk_ref, ck_ref), (vc_ref, wv_ref, cv_ref)):
        for parity in range(2):
            acc = jnp.zeros((nch, LANES), F32)
            for tt in range(C_CMP_BLOCK):
                rows = src[pl.ds(parity * C_CMP_BLOCK + tt, nch, stride=stride), :]
                acc = acc + _dot(rows.astype(BF16), w_ref[tt])
            dst[0, parity * nch:(parity + 1) * nch, :] = acc


def _compress_prompt(p, batch, w2k, w2v):
    s = p.shape[0] // batch
    nch = s // (2 * C_CMP_BLOCK)
    col = lambda c: pl.BlockSpec((s, LANES), lambda b: (b, c))
    wspec = pl.BlockSpec((C_CMP_BLOCK, LANES, LANES), lambda b: (0, 0, 0))
    ospec = pl.BlockSpec((1, 2 * nch, LANES), lambda b: (b, 0, 0))
    oshape = jax.ShapeDtypeStruct((batch, 2 * nch, LANES), F32)
    return pl.pallas_call(
        functools.partial(_compress_kernel, nch=nch),
        grid=(batch,),
        in_specs=[col(C_KV_COL), col(C_KV_COL + 1), wspec, wspec],
        out_specs=[ospec, ospec],
        out_shape=[oshape, oshape],
        compiler_params=_params("parallel"),
    )(p, p, w2k, w2v)


def _group_halves(x, g):
    lo = _iota(x.shape, 1) < HEAD_DIM
    base = jnp.where(lo if g == 0 else ~lo, x, 0.0)
    other = pltpu.roll(base, HEAD_DIM, 1)
    pair = (base, other) if g == 0 else (other, base)
    return pair[0].astype(BF16), pair[1].astype(BF16)


def _nsa_cmp_kernel(sl_ref, q_ref, g_ref, ck_ref, cv_ref, o_ref, sel_ref, *, tq, nch):
    q0 = pl.program_id(1) * tq
    nl = 2 * nch
    lane = _iota((tq, nl), 1)
    qpos = q0 + _iota((tq, nl), 0)
    tok = jnp.where(lane < nch, 2 * lane, 2 * (lane - nch) + 1)
    endp = (tok + 1) * C_CMP_BLOCK - 1
    okc = endp <= qpos
    relc = (endp[0:1, :] - q0).astype(F32)
    lane_s = _iota((tq, nch), 1)
    qblk = (q0 + _iota((tq, nch), 0)) // C_SEL_BLOCK
    lane_sf = lane_s.astype(F32)
    for g in range(C_GROUPS):
        k_lo, k_hi = _group_halves(ck_ref[0], g)
        v_lo, v_hi = _group_halves(cv_ref[0], g)
        imp = jnp.zeros((tq, nl), F32)
        for hp in range(g * C_RATIO // 2, (g + 1) * C_RATIO // 2):
            cs = slice(hp * LANES, (hp + 1) * LANES)
            qp = (q_ref[:, cs] * SCALE).astype(BF16)
            o_pair = None
            for e in range(2):
                h = 2 * hp + e
                s = _dot_nt(qp, k_lo if e == 0 else k_hi) + sl_ref[h] * relc
                s = jnp.where(okc, s, NEG)
                pc = jnp.where(okc, jnp.exp(s - jnp.max(s, axis=1, keepdims=True)), 0.0)
                pc = pc / jnp.maximum(jnp.sum(pc, axis=1, keepdims=True), TINY)
                imp = imp + pc
                gate = jax.nn.sigmoid(g_ref[:, 3 * h:3 * h + 1])
                o = _dot(pc.astype(BF16), v_lo if e == 0 else v_hi) * gate
                o_pair = o if o_pair is None else o_pair + o
            o_ref[:, cs] = o_pair
        imp_sel = imp[:, 0:nch] + imp[:, nch:nl]
        sel = _topk_mask(imp_sel, lane_s < qblk, lane_sf, C_TOPK)
        sel = jnp.where(lane_s == qblk, 1.0, sel)
        if nch < LANES:
            sel = jnp.concatenate([sel, jnp.zeros((tq, LANES - nch), F32)], axis=1)
        sel_ref[:, g * LANES:(g + 1) * LANES] = sel


def _nsa_cmp(p, batch, slopes, ck, cv, tq):
    nq = p.shape[0] // batch // tq
    nch = ck.shape[1] // 2
    assert nch <= LANES
    row = lambda width, col: pl.BlockSpec((tq, width), lambda b, i: (b * nq + i, col))
    cspec = pl.BlockSpec((1, 2 * nch, LANES), lambda b, i: (b, 0, 0))
    return pl.pallas_call(
        functools.partial(_nsa_cmp_kernel, tq=tq, nch=nch),
        grid=(batch, nq),
        in_specs=[_smem(), row(D_MODEL, 0), row(LANES, C_GATE_COL), cspec, cspec],
        out_specs=[row(D_MODEL, 0), row(C_GROUPS * LANES, 0)],
        out_shape=[jax.ShapeDtypeStruct((p.shape[0], D_MODEL), F32),
                   jax.ShapeDtypeStruct((p.shape[0], C_GROUPS * LANES), F32)],
        compiler_params=_params("parallel", "parallel"),
    )(slopes, p, p, ck, cv)


def _decode_kernel(pt_ref, q_ref, kn_ref, vn_ref, *rest, n_pages, mode, lam_init):
    del pt_ref
    gp = PAGES_PER_STEP
    if mode == "moba":
        slc_ref, rest = rest[0], rest[1:]
    elif mode == "diff":
        slc_ref, lam_ref, g_ref, rest = rest[0], rest[1], rest[2], rest[3:]
    else:
        lfn_ref, lft_refs, rest = rest[0], rest[1:1 + gp], rest[1 + gp:]
    kpool_refs, vpool_refs, rest = rest[:gp], rest[gp:2 * gp], rest[2 * gp:]
    if mode == "moba":
        o_ref, qbd_scr, st_scr, p_scr, acc_scr, bm_scr = rest
    elif mode == "diff":
        o_ref, qbd_scr, st_scr, p_scr, acc_scr = rest
    else:
        o_ref, qbd_scr, st_scr, p_scr, acc_scr, bias_scr, carry_scr = rest
    s = pl.program_id(1)
    n_steps = n_pages // gp
    past = n_pages * PAGE
    row = _iota((N_HEADS, D_MODEL), 0)
    lane = _iota((N_HEADS, D_MODEL), 1)

    @pl.when(s == 0)
    def _():
        qbd_scr[...] = jnp.where(lane // HEAD_DIM == row, jnp.broadcast_to(q_ref[0], (N_HEADS, D_MODEL)), 0.0)
        if mode == "moba":
            bm_scr[...] = jnp.zeros(bm_scr.shape, F32)
        if mode == "fox":
            carry_scr[...] = jnp.zeros(carry_scr.shape, F32)

    @pl.when(s < n_steps)
    def _():
        qb = (qbd_scr[...] * SCALE).astype(BF16)
        for g in range(gp):
            page = s * gp + g
            kp = kpool_refs[g][0]
            off = pl.multiple_of(page * PAGE, PAGE)
            st_scr[:, pl.ds(off, PAGE)] = _dot_nt(qb, kp.astype(BF16))
            if mode == "moba":
                blk = pl.ds(page // (A_BLOCK // PAGE), 1)
                bm_scr[blk, :] = bm_scr[blk, :] + jnp.sum(kp, axis=0, keepdims=True)
            if mode == "fox":
                tri = (_iota((PAGE, PAGE), 0) <= _iota((PAGE, PAGE), 1)).astype(F32)
                c = jnp.dot(lft_refs[g][0], tri, precision=HIGHEST, preferred_element_type=F32) + carry_scr[...]
                bias_scr[:, pl.ds(off, PAGE)] = -c
                carry_scr[...] = c[:, PAGE - 1:PAGE]

    @pl.when(s == n_steps - 1)
    def _():
        qb = qbd_scr[...]
        s_new = jnp.sum(qb * SCALE * jnp.broadcast_to(kn_ref[0], qb.shape), axis=1, keepdims=True)
        if mode == "fox":
            s_all = st_scr[...] + bias_scr[...]
            s_new = s_new - (carry_scr[...] + lfn_ref[0])
        else:
            kpos = _iota((1, past), 1)
            s_all = st_scr[...] + slc_ref[...] * (kpos - past).astype(F32)
        if mode == "moba":
            own = past // A_BLOCK
            bm_scr[own:own + 1, :] = bm_scr[own:own + 1, :] + kn_ref[0]
            lane_b = _iota((N_HEADS, LANES), 1)
            sb = _dot_nt(qb.astype(BF16), (bm_scr[...] / A_BLOCK).astype(BF16))
            sel = _topk_mask(sb, lane_b < past // A_BLOCK, lane_b.astype(F32), A_TOPK)
            st_scr[...] = s_all
            for n in range(past // A_BLOCK):
                cols = slice(n * A_BLOCK, (n + 1) * A_BLOCK)
                st_scr[:, cols] = st_scr[:, cols] + jnp.where(sel[:, n:n + 1] > 0.5, 0.0, NEG)
            s_all = st_scr[...]
        m = jnp.maximum(jnp.max(s_all, axis=1, keepdims=True), s_new)
        p = jnp.exp(s_all - m)
        pn = jnp.exp(s_new - m)
        l = jnp.sum(p, axis=1, keepdims=True) + pn
        p_scr[...] = p / l
        acc_scr[...] = (pn / l) * jnp.broadcast_to(vn_ref[0], (N_HEADS, D_MODEL))

    @pl.when(s >= n_steps)
    def _():
        acc = acc_scr[...]
        for g in range(gp):
            off = pl.multiple_of(((s - n_steps) * gp + g) * PAGE, PAGE)
            acc = acc + _dot(p_scr[:, pl.ds(off, PAGE)].astype(BF16), vpool_refs[g][0].astype(BF16))
        acc_scr[...] = acc

    @pl.when(s == 2 * n_steps - 1)
    def _():
        acc = acc_scr[...]
        if mode == "diff":
            lam = _lambda(lam_ref, lam_init)
            signed = jnp.where(row % 2 == 0, acc, -lam * acc)
            signed = jnp.where(lane // LANES == row // 2, signed, 0.0)
            o8 = jnp.where(row % 2 == 0, signed + pltpu.roll(signed, N_HEADS - 1, 0), 0.0)
            ms = jnp.sum(o8 * o8, axis=1, keepdims=True) / LANES
            o8 = o8 * lax.rsqrt(ms + LN_EPS)
            o_ref[0] = jnp.sum(o8, axis=0, keepdims=True) * g_ref[...] * (1.0 - lam_init)
        else:
            o_ref[0] = jnp.sum(jnp.where(lane // HEAD_DIM == row, acc, 0.0), axis=0, keepdims=True)


def _decode(ps3, page_table, kpool, vpool, mode, extras, lam_init=0.0):
    b = ps3.shape[0]
    n_pages = page_table.shape[1]
    gp = PAGES_PER_STEP
    assert n_pages % 2 == 0 and n_pages % gp == 0
    n_steps = n_pages // gp
    past = n_pages * PAGE
    pt = page_table.reshape(-1)
    rowspec = lambda col: pl.BlockSpec((1, 1, D_MODEL), lambda bi, s, pt_: (bi, 0, col))
    full = lambda shape: pl.BlockSpec(shape, lambda bi, s, pt_: tuple(0 for _ in shape))
    kpage = lambda g: (lambda bi, s, pt_: (pt_[bi * n_pages + jnp.minimum(s, n_steps - 1) * gp + g], 0, 0))
    vpage = lambda g: (lambda bi, s, pt_: (pt_[bi * n_pages + jnp.maximum(s - n_steps, 0) * gp + g], 0, 0))
    kspecs = [pl.BlockSpec((1, PAGE, D_MODEL), kpage(g)) for g in range(gp)]
    vspecs = [pl.BlockSpec((1, PAGE, D_MODEL), vpage(g)) for g in range(gp)]
    scratch = [pltpu.VMEM((N_HEADS, D_MODEL), F32), pltpu.VMEM((N_HEADS, past), F32),
               pltpu.VMEM((N_HEADS, past), F32), pltpu.VMEM((N_HEADS, D_MODEL), F32)]
    args = list(extras)
    if mode == "moba":
        in_specs = [full((N_HEADS, 1))]
        scratch += [pltpu.VMEM((LANES, D_MODEL), F32)]
    elif mode == "diff":
        in_specs = [full((N_HEADS, 1)), full((4, HEAD_DIM)), full((1, D_MODEL))]
    else:
        lft, lfn = extras
        in_specs = ([pl.BlockSpec((1, N_HEADS, 1), lambda bi, s, pt_: (bi, 0, 0))]
                    + [pl.BlockSpec((1, N_HEADS, PAGE), kpage(g)) for g in range(gp)])
        args = [lfn] + [lft] * gp
        scratch += [pltpu.VMEM((N_HEADS, past), F32), pltpu.VMEM((N_HEADS, 1), F32)]
    return pl.pallas_call(
        functools.partial(_decode_kernel, n_pages=n_pages, mode=mode, lam_init=lam_init),
        grid_spec=pltpu.PrefetchScalarGridSpec(
            num_scalar_prefetch=1,
            grid=(b, 2 * n_steps),
            in_specs=[rowspec(0), rowspec(1), rowspec(2)] + in_specs + kspecs + vspecs,
            out_specs=pl.BlockSpec((1, 1, D_MODEL), lambda bi, s, pt_: (bi, 0, 0)),
            scratch_shapes=scratch),
        out_shape=jax.ShapeDtypeStruct((b, 1, D_MODEL), F32),
        compiler_params=_params("parallel", "arbitrary"),
    )(pt, ps3, ps3, ps3, *args, *([kpool] * gp), *([vpool] * gp))


C_HALF = 64


def _softmax_new(s_all, s_new, v_all, v_new):
    m = jnp.maximum(jnp.max(s_all, axis=1, keepdims=True), s_new)
    p = jnp.exp(s_all - m)
    pn = jnp.exp(s_new - m)
    l = jnp.sum(p, axis=1, keepdims=True) + pn
    return (_dot(p.astype(BF16), v_all.astype(BF16)) + pn * v_new) / l


def _nsa_decode_kernel(pt_ref, qbd_ref, g_ref, slc_ref, kcn_ref, vcn_ref, ksn_ref, vsn_ref, kwn_ref, vwn_ref,
                       ckp_ref, cvp_ref, ksp_ref, vsp_ref, kw_ref, vw_ref, wk0_ref, wv0_ref, o_ref,
                       ck_scr, cv_scr, ks_scr, vs_scr, *, n_pages):
    del pt_ref
    s = pl.program_id(1)
    past = n_pages * PAGE
    per_page = PAGE // C_CMP_BLOCK // 2

    @pl.when(s == 0)
    def _():
        ck_scr[...] = jnp.zeros(ck_scr.shape, F32)
        cv_scr[...] = jnp.zeros(cv_scr.shape, F32)

    off = pl.multiple_of(s * PAGE, PAGE)
    ks_scr[pl.ds(off, PAGE), :] = ksp_ref[0]
    vs_scr[pl.ds(off, PAGE), :] = vsp_ref[0]
    for u in range(2 * per_page):
        dst = (u % 2) * C_HALF + per_page * s + u // 2
        ck_scr[pl.ds(dst, 1), :] = ckp_ref[0, u:u + 1, :]
        cv_scr[pl.ds(dst, 1), :] = cvp_ref[0, u:u + 1, :]

    @pl.when(s == n_pages - 1)
    def _():
        qf = qbd_ref[0] * SCALE
        qb = qf.astype(BF16)
        slc = slc_ref[...]
        new_tok = per_page * n_pages
        ck_scr[new_tok:new_tok + 1, :] = _dot(jnp.broadcast_to(kcn_ref[0], (8, LANES)).astype(BF16), wk0_ref[...])[0:1]
        cv_scr[new_tok:new_tok + 1, :] = _dot(jnp.broadcast_to(vcn_ref[0], (8, LANES)).astype(BF16), wv0_ref[...])[0:1]
        lane = _iota((N_HEADS, LANES), 1)
        row = _iota((N_HEADS, LANES), 0)
        tok = jnp.where(lane < C_HALF, 2 * lane, 2 * (lane - C_HALF) + 1)
        endp = (tok + 1) * C_CMP_BLOCK - 1
        okc = endp <= past
        sc = _dot_nt(qb, ck_scr[...].astype(BF16)) + slc * (endp - past).astype(F32)
        sc = jnp.where(okc, sc, NEG)
        pc = jnp.where(okc, jnp.exp(sc - jnp.max(sc, axis=1, keepdims=True)), 0.0)
        pc = pc / jnp.maximum(jnp.sum(pc, axis=1, keepdims=True), TINY)
        o_cmp = _dot(pc.astype(BF16), cv_scr[...].astype(BF16))
        imp = jnp.where(row < C_RATIO, jnp.sum(pc[0:C_RATIO], axis=0, keepdims=True),
                        jnp.sum(pc[C_RATIO:N_HEADS], axis=0, keepdims=True))
        imp = imp + pltpu.roll(imp, C_HALF, 1)
        sel = _topk_mask(imp, lane < past // C_SEL_BLOCK, lane.astype(F32), C_TOPK)
        expand = (_iota((LANES, past), 0) == _iota((LANES, past), 1) // C_SEL_BLOCK).astype(BF16)
        picked = _dot(sel.astype(BF16), expand)
        kpos = _iota((1, past), 1)
        ss = _dot_nt(qb, ks_scr[...].astype(BF16)) + slc * (kpos - past).astype(F32)
        ss = jnp.where(picked > 0.5, ss, NEG)
        ss_new = jnp.sum(qf * ksn_ref[0], axis=1, keepdims=True)
        o_sel = _softmax_new(ss, ss_new, vs_scr[...], vsn_ref[0])
        wb = kw_ref.shape[1]
        wpos = past - wb + _iota((1, wb), 1)
        okw = (past - wpos <= C_WINDOW) & (wpos >= 0)
        sw = _dot_nt(qb, kw_ref[0].astype(BF16)) + slc * (wpos - past).astype(F32)
        sw = jnp.where(okw, sw, NEG)
        sw_new = jnp.sum(qf * kwn_ref[0], axis=1, keepdims=True)
        o_win = _softmax_new(sw, sw_new, vw_ref[0], vwn_ref[0])
        gate = jax.nn.sigmoid(g_ref[0])
        o_ref[0] = gate[:, 0:1] * o_cmp + gate[:, 1:2] * o_sel + gate[:, 2:3] * o_win


def _nsa_decode(ps3, qbd, gates, slc, page_table, ck_pool, cv_pool, ks_pool, vs_pool, kw_buf, vw_buf, wk0, wv0):
    b = ps3.shape[0]
    n_pages = page_table.shape[1]
    past = n_pages * PAGE
    assert 2 * n_pages + 1 <= C_HALF
    pt = page_table.reshape(-1)
    per_b = lambda shape: pl.BlockSpec((1,) + shape, lambda bi, s, pt_: (bi,) + tuple(0 for _ in shape))
    full = lambda shape: pl.BlockSpec(shape, lambda bi, s, pt_: tuple(0 for _ in shape))
    newrow = lambda col: pl.BlockSpec((1, 1, LANES), lambda bi, s, pt_: (bi, 0, col))
    paged = lambda rows: pl.BlockSpec((1, rows, LANES), lambda bi, s, pt_: (pt_[bi * n_pages + s], 0, 0))
    return pl.pallas_call(
        functools.partial(_nsa_decode_kernel, n_pages=n_pages),
        grid_spec=pltpu.PrefetchScalarGridSpec(
            num_scalar_prefetch=1,
            grid=(b, n_pages),
            in_specs=[per_b((N_HEADS, LANES)), per_b((N_HEADS, 3)), full((N_HEADS, 1))]
                     + [newrow(C_KV_COL + c) for c in range(6)]
                     + [paged(PAGE // C_CMP_BLOCK), paged(PAGE // C_CMP_BLOCK), paged(PAGE), paged(PAGE),
                        per_b(kw_buf.shape[1:]), per_b(vw_buf.shape[1:]), full((LANES, LANES)), full((LANES, LANES))],
            out_specs=per_b((N_HEADS, LANES)),
            scratch_shapes=[pltpu.VMEM((2 * C_HALF, LANES), F32), pltpu.VMEM((2 * C_HALF, LANES), F32),
                            pltpu.VMEM((past, LANES), F32), pltpu.VMEM((past, LANES), F32)]),
        out_shape=jax.ShapeDtypeStruct((b, N_HEADS, LANES), F32),
        compiler_params=_params("parallel", "arbitrary"),
    )(pt, qbd, gates, slc, ps3, ps3, ps3, ps3, ps3, ps3, ck_pool, cv_pool, ks_pool, vs_pool, kw_buf, vw_buf, wk0, wv0)


def _pad_cols(w, width):
    return jnp.pad(w, ((0, 0), (0, width - w.shape[1])))


def _block_diag_cmp(w):
    w3 = w.reshape(C_CMP_BLOCK, HEAD_DIM, HEAD_DIM)
    z = jnp.zeros_like(w3)
    return jnp.concatenate([jnp.concatenate([w3, z], 2), jnp.concatenate([z, w3], 2)], 1).astype(BF16)


def kernel(x_prompt, x_sample, cache_a_k, cache_a_v, cache_b_k, cache_b_v, cache_c_kc, cache_c_vc, cache_c_ks, cache_c_vs, state_c_kw, state_c_vw, cache_d_k, cache_d_v, cache_d_logf, page_table, w_in_0, w_out_0, ln_g_0, ln_b_0, w_in_1, lam_q1_1, lam_k1_1, lam_q2_1, lam_k2_1, subln_g_1, w_out_1, ln_g_1, ln_b_1, w_in_2, w_cmp_k_2, w_cmp_v_2, w_out_2, ln_g_2, ln_b_2, w_in_3, b_f_3, w_out_3, ln_g_3, ln_b_3):
    bp, sp, _ = x_prompt.shape
    bs = x_sample.shape[0]
    assert x_sample.shape[1] == 1 and sp % C_WINDOW == 0
    n_pool = cache_a_k.shape[0]
    mp = bp * sp
    tm = 512
    xp = x_prompt.reshape(mp, D_MODEL)
    xs = x_sample.reshape(bs, D_MODEL)
    heads = lambda a, b, l, h: a.reshape(b, l, h, -1)
    pool2 = lambda c: c.reshape(n_pool, PAGE, -1)
    split_slopes = lambda sl: jnp.stack(_split3(sl * LOG2E), axis=1)
    vcols = slice(2 * D_MODEL, 3 * D_MODEL)

    w = w_in_0.astype(BF16)
    slopes16 = _alibi_slopes(N_HEADS)
    sl3_16 = split_slopes(slopes16)
    pp = _matmul(xp, w, tm)
    ps = _matmul(xs, w, bs)
    a_k_p, a_v_p = heads(pp[:, 1024:2048], bp, sp, 16), heads(pp[:, 2048:3072], bp, sp, 16)
    a_k_s, a_v_s = heads(ps[:, 1024:2048], bs, 1, 16), heads(ps[:, 2048:3072], bs, 1, 16)
    qa, ka = _prep_moba(pp, bp, sl3_16, _block_mean(pp, bp, 1))
    o_p = _flash_t(qa, ka, _matmul_t(xp, w[:, vcols].T, tm), bp, fin="plain")
    o_s = _decode(ps.reshape(bs, 1, -1), page_table, pool2(cache_a_k), pool2(cache_a_v), "moba",
                  [slopes16.reshape(-1, 1)]).reshape(bs, D_MODEL)
    w_o = w_out_0.astype(BF16)
    xp = _out_ln(o_p, pp, 3, xp, w_o, ln_g_0, ln_b_0, tm)
    xs = _out_ln(o_s, ps, 3, xs, w_o, ln_g_0, ln_b_0, bs)

    lam_init = 0.8 - 0.6 * math.exp(-0.3 * 1)
    w = w_in_1.astype(BF16)
    slopes8 = jnp.repeat(_alibi_slopes(B_HEADS), 2)
    lamv = jnp.stack([lam_q1_1, lam_k1_1, lam_q2_1, lam_k2_1])
    pp = _matmul(xp, w, tm)
    ps = _matmul(xs, w, bs)
    b_k_p, b_v_p = heads(pp[:, 1024:2048], bp, sp, 16), heads(pp[:, 2048:3072], bp, sp, 8)
    b_k_s, b_v_s = heads(ps[:, 1024:2048], bs, 1, 16), heads(ps[:, 2048:3072], bs, 1, 8)
    qa, ka = _prep(pp, bp, "diff", 1, [split_slopes(slopes8)], tm)
    o_p = _flash_t(qa, ka, _matmul_t(xp, w[:, vcols].T, tm), bp, fin="diff",
                   extras=(lamv, subln_g_1.reshape(1, -1)), lam_init=lam_init)
    o_s = _decode(ps.reshape(bs, 1, -1), page_table, pool2(cache_b_k), pool2(cache_b_v), "diff",
                  [slopes8.reshape(-1, 1), lamv, jnp.tile(subln_g_1, B_HEADS).reshape(1, -1)],
                  lam_init).reshape(bs, D_MODEL)
    w_o = w_out_1.astype(BF16)
    xp = _out_ln(o_p, pp, 3, xp, w_o, ln_g_1, ln_b_1, tm)
    xs = _out_ln(o_s, ps, 3, xs, w_o, ln_g_1, ln_b_1, bs)

    kv0, z0, g0 = D_MODEL, D_MODEL + 6 * LANES, 2 * D_MODEL + 6 * LANES
    w = _pad_cols(jnp.concatenate([w_in_2[:, :kv0], w_in_2[:, z0:g0], w_in_2[:, kv0:z0], w_in_2[:, g0:]], 1),
                  C_WIDTH).astype(BF16)
    w2k, w2v = _block_diag_cmp(w_cmp_k_2), _block_diag_cmp(w_cmp_v_2)
    pp = _matmul(xp, w, tm)
    ps = _matmul(xs, w, bs)
    kvcol = lambda p, c: p[:, (C_KV_COL + c) * LANES:(C_KV_COL + c + 1) * LANES]
    c_p = [heads(kvcol(pp, c), bp, sp, 2) for c in range(6)]
    c_s = [heads(kvcol(ps, c), bs, 1, 2) for c in range(6)]
    keep = min(C_WINDOW, sp)
    ck, cv = _compress_prompt(pp, bp, w2k, w2v)
    o_p, sel = _nsa_cmp(pp, bp, slopes16, ck, cv, 256)
    qa, ka = _prep(pp, bp, "sel", C_KV_COL + 2, [sl3_16, sel], tm)
    o_p = _flash_t(qa, ka, _matmul_t(xp, kvcol(w, 3).T, tm), bp, fin="nsa", extras=(pp, o_p), gate_idx=1)
    qa, ka = _prep(pp, bp, "win", C_KV_COL + 4, [sl3_16], tm)
    o_p = _flash_t(qa, ka, _matmul_t(xp, kvcol(w, 5).T, tm), bp, fin="nsa", band=True, extras=(pp, o_p),
                   gate_idx=2)
    tokens = PAGE // C_CMP_BLOCK
    flat = lambda c: c.reshape(n_pool * tokens, C_CMP_BLOCK * LANES)
    ck_pool = _matmul(flat(cache_c_kc), w2k.reshape(-1, LANES), 512).reshape(n_pool, tokens, LANES)
    cv_pool = _matmul(flat(cache_c_vc), w2v.reshape(-1, LANES), 512).reshape(n_pool, tokens, LANES)
    q_s = ps[:, :D_MODEL].reshape(bs, N_HEADS, 1, HEAD_DIM)
    in_group = (jnp.arange(N_HEADS) // C_RATIO)[:, None] == jnp.arange(C_GROUPS)[None, :]
    qbd = jnp.where(in_group[None, :, :, None], q_s, 0.0).reshape(bs, N_HEADS, LANES)
    gates_s = ps[:, C_GATE_COL * LANES:C_GATE_COL * LANES + 3 * N_HEADS].reshape(bs, N_HEADS, 3)
    wb = state_c_kw.shape[1]
    o16 = _nsa_decode(ps.reshape(bs, 1, -1), qbd, gates_s, slopes16.reshape(-1, 1), page_table, ck_pool, cv_pool,
                      pool2(cache_c_ks), pool2(cache_c_vs), state_c_kw.reshape(bs, wb, LANES),
                      state_c_vw.reshape(bs, wb, LANES), w2k[0], w2v[0])
    o16 = o16.reshape(bs, N_HEADS, C_GROUPS, HEAD_DIM)
    o_s = jnp.where(in_group[None, :, :, None], o16, 0.0).sum(2).reshape(bs, D_MODEL)
    c_kw_s = jnp.concatenate([state_c_kw, c_s[4]], 1)[:, -min(C_WINDOW, wb + 1):]
    c_vw_s = jnp.concatenate([state_c_vw, c_s[5]], 1)[:, -min(C_WINDOW, wb + 1):]
    w_o = w_out_2.astype(BF16)
    xp = _out_ln(o_p, pp, C_Z_COL, xp, w_o, ln_g_2, ln_b_2, tm)
    xs = _out_ln(o_s, ps, C_Z_COL, xs, w_o, ln_g_2, ln_b_2, bs)

    f_col = 4 * D_MODEL // LANES
    w = _pad_cols(w_in_3, 4 * D_MODEL + LANES).astype(BF16)
    pp = _matmul(xp, w, tm)
    ps = _matmul(xs, w, bs)
    d_k_p, d_v_p = heads(pp[:, 1024:2048], bp, sp, 16), heads(pp[:, 2048:3072], bp, sp, 16)
    d_k_s, d_v_s = heads(ps[:, 1024:2048], bs, 1, 16), heads(ps[:, 2048:3072], bs, 1, 16)
    lf_p = _logf(pp, f_col, b_f_3, tm).reshape(bp, sp, N_HEADS)
    lf_s = _logf(ps, f_col, b_f_3, bs).reshape(bs, 1, N_HEADS)
    c_p3 = _cumsum_lanes(lf_p.transpose(0, 2, 1), tm).transpose(0, 2, 1).reshape(mp, N_HEADS)
    qa, ka = _prep(pp, bp, "fox", 1, [c_p3], tm)
    o_p = _flash_t(qa, ka, _matmul_t(xp, w[:, vcols].T, tm), bp, fin="plain")
    o_s = _decode(ps.reshape(bs, 1, -1), page_table, pool2(cache_d_k), pool2(cache_d_v), "fox",
                  [cache_d_logf.transpose(0, 2, 1), lf_s.reshape(bs, N_HEADS, 1)]).reshape(bs, D_MODEL)
    w_o = w_out_3.astype(BF16)
    xp = _out_ln(o_p, pp, 3, xp, w_o, ln_g_3, ln_b_3, tm)
    xs = _out_ln(o_s, ps, 3, xs, w_o, ln_g_3, ln_b_3, bs)

    return (xp.reshape(bp, sp, D_MODEL), xs.reshape(bs, 1, D_MODEL),
            a_k_p, a_v_p, a_k_s, a_v_s, b_k_p, b_v_p, b_k_s, b_v_s,
            c_p[0], c_p[1], c_p[2], c_p[3], c_p[4][:, -keep:], c_p[5][:, -keep:],
            c_s[0], c_s[1], c_s[2], c_s[3], c_kw_s, c_vw_s,
            d_k_p, d_v_p, lf_p, d_k_s, d_v_s, lf_s)
```

```python
import functools
import math

import numpy as np
import jax
import jax.numpy as jnp
from jax import lax
from jax.experimental import pallas as pl
from jax.experimental.pallas import tpu as pltpu

F32 = jnp.float32
BF16 = jnp.bfloat16
HIGHEST = lax.Precision.HIGHEST

D_MODEL = 1024
HEAD_DIM = 64
N_HEADS = 16
LANES = 128
SCALE = HEAD_DIM ** -0.5
PAGE = 128
DEPTH = 4
ALPHA = (2 * DEPTH) ** 0.25
LN_EPS = 1e-5
NEG = -1e30
TINY = 1e-30
A_BLOCK = 256
A_TOPK = 3
B_HEADS = 8
C_GROUPS = 2
C_RATIO = 8
C_CMP_BLOCK = 32
C_SEL_BLOCK = 64
C_TOPK = 4
C_WINDOW = 512
PAGES_PER_STEP = 4
VMEM_LIMIT = 56 * 1024 * 1024


def _dot_nt(a, b):
    return lax.dot_general(a, b, (((1,), (1,)), ((), ())), preferred_element_type=F32)


def _dot(a, b):
    return jnp.dot(a, b, preferred_element_type=F32)


def _iota(shape, dim):
    return lax.broadcasted_iota(jnp.int32, shape, dim)


def _params(*sem):
    return pltpu.CompilerParams(dimension_semantics=sem, vmem_limit_bytes=VMEM_LIMIT)


def _alibi_slopes(n):
    return jnp.exp2(-8.0 * jnp.arange(1, n + 1, dtype=F32) / n)


def _smem():
    return pl.BlockSpec(memory_space=pltpu.SMEM)


def _mm_kernel(x_ref, w_ref, o_ref):
    o_ref[...] = _dot(x_ref[...].astype(BF16), w_ref[...])


def _pick_tn(n):
    best = LANES
    for t in range(LANES, 1536 + 1, LANES):
        if n % t == 0:
            best = t
    return best


def _matmul(x, w, tm):
    m, k = x.shape
    n = w.shape[1]
    tm = min(tm, m)
    assert m % tm == 0
    tn = _pick_tn(n)
    return pl.pallas_call(
        _mm_kernel,
        grid=(m // tm, n // tn),
        in_specs=[pl.BlockSpec((tm, k), lambda i, j: (i, 0)),
                  pl.BlockSpec((k, tn), lambda i, j: (0, j))],
        out_specs=pl.BlockSpec((tm, tn), lambda i, j: (i, j)),
        out_shape=jax.ShapeDtypeStruct((m, n), F32),
        compiler_params=_params("parallel", "arbitrary"),
    )(x, w)


def _vt_kernel(x_ref, w_ref, o_ref):
    o_ref[...] = _dot_nt(w_ref[...], x_ref[...].astype(BF16)).astype(BF16)


def _matmul_t(x, wt, tm):
    m, k = x.shape
    n = wt.shape[0]
    return pl.pallas_call(
        _vt_kernel,
        grid=(m // tm,),
        in_specs=[pl.BlockSpec((tm, k), lambda i: (i, 0)), pl.BlockSpec((n, k), lambda i: (0, 0))],
        out_specs=pl.BlockSpec((n, tm), lambda i: (0, i)),
        out_shape=jax.ShapeDtypeStruct((n, m), BF16),
        compiler_params=_params("parallel"),
    )(x, wt)


def _out_ln_kernel(o_ref, z_ref, x_ref, w_ref, g_ref, b_ref, y_ref):
    z = z_ref[...]
    a = (o_ref[...] * (z * jax.nn.sigmoid(z))).astype(BF16)
    h = ALPHA * x_ref[...] + _dot(a, w_ref[...])
    hc = h - jnp.mean(h, axis=-1, keepdims=True)
    var = jnp.mean(hc * hc, axis=-1, keepdims=True)
    y_ref[...] = hc * lax.rsqrt(var + LN_EPS) * g_ref[...] + b_ref[...]


def _out_ln(o, p, z_col, x, w, g, b, tm):
    m = x.shape[0]
    row = lambda i: (i, 0)
    return pl.pallas_call(
        _out_ln_kernel,
        grid=(m // tm,),
        in_specs=[pl.BlockSpec((tm, D_MODEL), row),
                  pl.BlockSpec((tm, D_MODEL), lambda i: (i, z_col)),
                  pl.BlockSpec((tm, D_MODEL), row),
                  pl.BlockSpec((D_MODEL, D_MODEL), lambda i: (0, 0)),
                  pl.BlockSpec((1, D_MODEL), lambda i: (0, 0)),
                  pl.BlockSpec((1, D_MODEL), lambda i: (0, 0))],
        out_specs=pl.BlockSpec((tm, D_MODEL), row),
        out_shape=jax.ShapeDtypeStruct((m, D_MODEL), F32),
        compiler_params=_params("parallel"),
    )(o, p, x, w, g.reshape(1, -1), b.reshape(1, -1))


def _topk_mask(s, cand, lane_f, k):
    s = jnp.where(cand, s, NEG)
    sel = jnp.zeros(s.shape, F32)
    for _ in range(k):
        mx = jnp.max(s, axis=1, keepdims=True)
        idx = jnp.min(jnp.where(s == mx, lane_f, 1e9), axis=1, keepdims=True)
        pick = lane_f == idx
        valid = jnp.where(mx > 0.5 * NEG, 1.0, 0.0)
        sel = jnp.where(pick, valid, sel)
        s = jnp.where(pick, -3e38, s)
    return sel


def _lambda(lam_ref, lam_init):
    a = lam_ref[...]
    return (jnp.exp(jnp.sum(a[0:1] * a[1:2], axis=1, keepdims=True))
            - jnp.exp(jnp.sum(a[2:3] * a[3:4], axis=1, keepdims=True)) + lam_init)


def _block_mean_kernel(k_ref, o_ref):
    n = pl.program_id(1)

    @pl.when(n == 0)
    def _():
        o_ref[...] = jnp.zeros(o_ref.shape, F32)

    o_ref[0, pl.ds(n, 1), :] = jnp.sum(k_ref[...], axis=0, keepdims=True) / A_BLOCK


def _block_mean(p, batch, k_col):
    nb = p.shape[0] // batch // A_BLOCK
    return pl.pallas_call(
        _block_mean_kernel,
        grid=(batch, nb),
        in_specs=[pl.BlockSpec((A_BLOCK, D_MODEL), lambda b, n: (b * nb + n, k_col))],
        out_specs=pl.BlockSpec((1, LANES, D_MODEL), lambda b, n: (b, 0, 0)),
        out_shape=jax.ShapeDtypeStruct((batch, LANES, D_MODEL), F32),
        compiler_params=_params("parallel", "arbitrary"),
    )(p)


LOG2E = math.log2(math.e)
AUX0 = HEAD_DIM
SEL0 = 96
Q_SCALE = SCALE * LOG2E


def _split3(x):
    hi = x.astype(BF16).astype(F32)
    mid = (x - hi).astype(BF16).astype(F32)
    return hi, mid, x - hi - mid


def _lane_pick(lane, base, vals):
    out = jnp.zeros(lane.shape, F32)
    for idx, v in enumerate(vals):
        out = jnp.where(lane == base + idx, v, out)
    return out


def _slope_lanes(lane, sl_ref, h):
    parts = [sl_ref[h, c] for c in range(3)]
    return _lane_pick(lane, AUX0, parts + parts)


def _pos_lanes(lane, kpos):
    hi = kpos.astype(BF16).astype(F32)
    lo = kpos - hi
    return _lane_pick(lane, AUX0, [hi, hi, hi, lo, lo, lo])


def _head_cols(ref, h):
    x = ref[:, (h // 2) * LANES:(h // 2 + 1) * LANES]
    return x if h % 2 == 0 else pltpu.roll(x, HEAD_DIM, 1)


def _prep_moba_kernel(sl_ref, q_ref, k_ref, bm_ref, qa_ref, ka_ref, *, tm):
    i = pl.program_id(1)
    lane = _iota((tm, LANES), 1)
    lane_f = lane.astype(F32)
    low = lane < HEAD_DIM
    kpos = (i * tm + _iota((tm, LANES), 0)).astype(F32)
    k_aux = jnp.where(lane >= SEL0, jnp.where(lane == SEL0 + i, 1.0, 0.0), _pos_lanes(lane, kpos))
    bm_low = _iota((LANES, LANES), 1) < HEAD_DIM
    for h in range(N_HEADS):
        cs = slice(h * LANES, (h + 1) * LANES)
        q = _head_cols(q_ref, h)
        bm = bm_ref[0, :, (h // 2) * LANES:(h // 2 + 1) * LANES]
        bme = jnp.where(bm_low if h % 2 == 0 else ~bm_low, bm, 0.0).astype(BF16)
        qsel = q_ref[:, (h // 2) * LANES:(h // 2 + 1) * LANES].astype(BF16)
        sel = _topk_mask(_dot_nt(qsel, bme), lane < i, lane_f, A_TOPK)
        penalty = pltpu.roll(jnp.where((sel > 0.5) | (lane == i), 0.0, NEG), SEL0, 1)
        q_aux = jnp.where(lane >= SEL0, penalty, _slope_lanes(lane, sl_ref, h))
        qa_ref[:, cs] = jnp.where(low, q * Q_SCALE, q_aux).astype(BF16)
        ka_ref[:, cs] = jnp.where(low, _head_cols(k_ref, h), k_aux).astype(BF16)


def _prep_moba(p, batch, sl3, bm):
    tm = A_BLOCK
    nt = p.shape[0] // batch // tm
    assert nt <= LANES - SEL0
    row = lambda col: pl.BlockSpec((tm, D_MODEL), lambda b, i: (b * nt + i, col))
    aug = pl.BlockSpec((tm, N_HEADS * LANES), lambda b, i: (b * nt + i, 0))
    shape = jax.ShapeDtypeStruct((p.shape[0], N_HEADS * LANES), BF16)
    return pl.pallas_call(
        functools.partial(_prep_moba_kernel, tm=tm),
        grid=(batch, nt),
        in_specs=[_smem(), row(0), row(1), pl.BlockSpec((1, LANES, D_MODEL), lambda b, i: (b, 0, 0))],
        out_specs=[aug, aug],
        out_shape=[shape, shape],
        compiler_params=_params("parallel", "parallel"),
    )(sl3, p, p, bm)


def _prep_kernel(*refs, tm, mode):
    if mode == "diff":
        sl_ref, q_ref, k_ref, qa_ref, ka_ref = refs
    elif mode == "fox":
        q_ref, k_ref, c_ref, qa_ref, ka_ref = refs
    elif mode == "win":
        sl_ref, q_ref, k_ref, qa_ref, ka_ref = refs
    else:
        sl_ref, q_ref, k_ref, sel_ref, qa_ref, ka_ref = refs
    i = pl.program_id(1)
    lane = _iota((tm, LANES), 1)
    low = lane < HEAD_DIM
    kpos = (i * tm + _iota((tm, LANES), 0)).astype(F32)
    cw = 2 * LANES if mode == "sel" else LANES
    for h in range(N_HEADS):
        if mode == "fox":
            q_aux = _lane_pick(lane, AUX0, [-1.0, -1.0, -1.0])
        else:
            q_aux = _slope_lanes(lane, sl_ref, h)
        qa_ref[:, h * cw:h * cw + LANES] = jnp.where(low, _head_cols(q_ref, h) * Q_SCALE, q_aux).astype(BF16)
        if mode == "sel":
            g = h // C_RATIO
            picked = sel_ref[:, g * LANES:(g + 1) * LANES]
            qa_ref[:, h * cw + LANES:(h + 1) * cw] = jnp.where(picked > 0.5, 0.0, NEG).astype(BF16)
        if mode == "diff":
            ka_ref[:, h * cw:(h + 1) * cw] = jnp.where(low, _head_cols(k_ref, h), _pos_lanes(lane, kpos)).astype(BF16)
        if mode == "fox":
            c = jnp.broadcast_to(c_ref[:, h:h + 1], (tm, LANES)) * LOG2E
            ka_ref[:, h * cw:(h + 1) * cw] = jnp.where(low, _head_cols(k_ref, h), _lane_pick(lane, AUX0, _split3(c))).astype(BF16)
    if mode in ("win", "sel"):
        for g in range(C_GROUPS):
            ka_ref[:, g * cw:g * cw + LANES] = jnp.where(low, _head_cols(k_ref, g), _pos_lanes(lane, kpos)).astype(BF16)
            if mode == "sel":
                own = (i * tm + _iota((tm, LANES), 0)) // C_SEL_BLOCK
                ka_ref[:, g * cw + LANES:(g + 1) * cw] = jnp.where(lane == own, 1.0, 0.0).astype(BF16)


def _prep(p, batch, mode, k_col, extras, tm):
    nt = p.shape[0] // batch // tm
    cw = 2 * LANES if mode == "sel" else LANES
    nk = C_GROUPS if mode in ("win", "sel") else N_HEADS
    kw = LANES if mode in ("win", "sel") else D_MODEL
    row = lambda width, col: pl.BlockSpec((tm, width), lambda b, i: (b * nt + i, col))
    in_specs = ([] if mode == "fox" else [_smem()]) + [row(D_MODEL, 0), row(kw, k_col)]
    if mode == "fox":
        in_specs.append(row(N_HEADS, 0))
    if mode == "sel":
        assert extras[-1].shape[1] == C_GROUPS * LANES
        in_specs.append(row(C_GROUPS * LANES, 0))
    return pl.pallas_call(
        functools.partial(_prep_kernel, tm=tm, mode=mode),
        grid=(batch, nt),
        in_specs=in_specs,
        out_specs=[row(N_HEADS * cw, 0), row(nk * cw, 0)],
        out_shape=[jax.ShapeDtypeStruct((p.shape[0], N_HEADS * cw), BF16),
                   jax.ShapeDtypeStruct((p.shape[0], nk * cw), BF16)],
        compiler_params=_params("parallel", "parallel"),
    )(*(extras[:1] if mode != "fox" else []), p, p, *(extras if mode == "fox" else extras[1:]))


def _sweep_tables(nq, band):
    it, jt, ft, mt = [], [], [], []
    for i in range(nq):
        js = [j for j in (i - 1, i) if j >= 0] if band else list(range(i + 1))
        for n, j in enumerate(js):
            it.append(i)
            jt.append(j)
            ft.append(1 if n == 0 else 0)
            mt.append(1 if j == i else (2 if band else 0))
    return [np.asarray(a, np.int32) for a in (it, jt, ft, mt)]


def _flash_t_kernel(it_ref, jt_ref, ft_ref, mt_ref, *rest, t, cw, kdiv, vdiv, dv, fin, modes, lam_init, gate_idx,
                    has_live):
    if has_live:
        live_ref, rest = rest[0], rest[1:]
    q_ref, k_ref, v_ref, rest = rest[0], rest[1], rest[2], rest[3:]
    if fin == "diff":
        lam_ref, g_ref, o_ref, m_scr, l_scr, acc_scr = rest
    elif fin == "nsa":
        g_ref, prev_ref, o_ref, m_scr, l_scr, acc_scr = rest
    else:
        o_ref, m_scr, l_scr, acc_scr = rest
    st = pl.program_id(1)

    @pl.when(ft_ref[st] == 1)
    def _():
        m_scr[...] = jnp.full(m_scr.shape, NEG, F32)
        l_scr[...] = jnp.zeros(l_scr.shape, F32)
        acc_scr[...] = jnp.zeros(acc_scr.shape, F32)

    def tile(mode):
        if mode:
            diff = _iota((t, t), 0) - _iota((t, t), 1)
            allowed = diff <= 0 if mode == 1 else diff >= 0

        def scores(h):
            kh = k_ref[:, (h // kdiv) * cw:(h // kdiv + 1) * cw]
            return _dot_nt(kh, q_ref[:, h * cw:(h + 1) * cw])

        s_next = scores(0)
        for h in range(N_HEADS):
            s = s_next
            if h + 1 < N_HEADS:
                s_next = scores(h + 1)
            if mode:
                s = jnp.where(allowed, s, NEG)
            m_prev = m_scr[h:h + 1, :]
            m_new = jnp.maximum(m_prev, jnp.max(s, axis=0, keepdims=True))
            p = jnp.exp2(s - m_new)
            corr = jnp.exp2(m_prev - m_new)
            l_scr[h:h + 1, :] = corr * l_scr[h:h + 1, :] + jnp.sum(p, axis=0, keepdims=True)
            m_scr[h:h + 1, :] = m_new
            rows = slice(h * dv, (h + 1) * dv)
            vh = v_ref[(h // vdiv) * dv:(h // vdiv + 1) * dv, :]
            acc_scr[rows, :] = acc_scr[rows, :] * corr + _dot(vh, p.astype(BF16))

    for mode in modes:
        run = mt_ref[st] == mode
        if has_live and mode == 0:
            run = run & (live_ref[pl.program_id(0) * pl.num_programs(1) + st] != 0)
        pl.when(run)(functools.partial(tile, mode))

    @pl.when(it_ref[st] == jt_ref[st])
    def _():
        lo = _iota((t, LANES), 1) < HEAD_DIM
        for hp in range(N_HEADS // 2):
            cs = slice(hp * LANES, (hp + 1) * LANES)
            if fin == "diff":
                lam = _lambda(lam_ref, lam_init)
                o = (acc_scr[2 * hp * dv:(2 * hp + 1) * dv, :] / l_scr[2 * hp:2 * hp + 1, :]
                     - lam * (acc_scr[(2 * hp + 1) * dv:(2 * hp + 2) * dv, :] / l_scr[2 * hp + 1:2 * hp + 2, :]))
                o = o * lax.rsqrt(jnp.mean(o * o, axis=0, keepdims=True) + LN_EPS)
                o_ref[:, cs] = o.T * g_ref[...] * (1.0 - lam_init)
            else:
                o = jnp.concatenate(
                    [acc_scr[(2 * hp + e) * dv:(2 * hp + e + 1) * dv, :] / l_scr[2 * hp + e:2 * hp + e + 1, :]
                     for e in range(2)], axis=0).T
                if fin == "nsa":
                    g0 = jax.nn.sigmoid(g_ref[:, 6 * hp + gate_idx:6 * hp + gate_idx + 1])
                    g1 = jax.nn.sigmoid(g_ref[:, 6 * hp + 3 + gate_idx:6 * hp + 4 + gate_idx])
                    o = prev_ref[:, cs] + o * jnp.where(lo, g0, g1)
                o_ref[:, cs] = o


def _flash_t(qa, ka, vt, batch, *, fin, band=False, extras=(), lam_init=0.0, gate_idx=0, live=None, t=512):
    m = qa.shape[0]
    nq = m // batch // t
    cw = qa.shape[1] // N_HEADS
    kdiv = N_HEADS // (ka.shape[1] // cw)
    dv = LANES if fin == "diff" else HEAD_DIM
    vdiv = N_HEADS // (vt.shape[0] // dv)
    tabs = _sweep_tables(nq, band)
    if live is not None:
        tabs.append(live[:, tabs[0], tabs[1]].reshape(-1).astype(jnp.int32))
    modes = (1, 2) if band else (0, 1)
    assert not band or t == C_WINDOW
    imap = lambda f: (lambda b, s, it_, jt_, *_: f(b, s, it_, jt_))
    qrow = lambda width: pl.BlockSpec((t, width), imap(lambda b, s, it_, jt_: (b * nq + it_[s], 0)))
    in_specs = [qrow(qa.shape[1]),
                pl.BlockSpec((t, ka.shape[1]), imap(lambda b, s, it_, jt_: (b * nq + jt_[s], 0))),
                pl.BlockSpec((vt.shape[0], t), imap(lambda b, s, it_, jt_: (0, b * nq + jt_[s])))]
    if fin == "diff":
        in_specs += [pl.BlockSpec((4, HEAD_DIM), imap(lambda b, s, it_, jt_: (0, 0))),
                     pl.BlockSpec((1, LANES), imap(lambda b, s, it_, jt_: (0, 0)))]
    if fin == "nsa":
        in_specs += [pl.BlockSpec((t, LANES), imap(lambda b, s, it_, jt_: (b * nq + it_[s], C_GATE_COL))),
                     qrow(D_MODEL)]
    return pl.pallas_call(
        functools.partial(_flash_t_kernel, t=t, cw=cw, kdiv=kdiv, vdiv=vdiv, dv=dv, fin=fin, modes=modes,
                          lam_init=lam_init, gate_idx=gate_idx, has_live=live is not None),
        grid_spec=pltpu.PrefetchScalarGridSpec(
            num_scalar_prefetch=len(tabs),
            grid=(batch, len(tabs[0])),
            in_specs=in_specs,
            out_specs=qrow(D_MODEL),
            scratch_shapes=[pltpu.VMEM((N_HEADS, t), F32), pltpu.VMEM((N_HEADS, t), F32),
                            pltpu.VMEM((N_HEADS * dv, t), F32)]),
        out_shape=jax.ShapeDtypeStruct((m, D_MODEL), F32),
        compiler_params=_params("parallel", "arbitrary"),
    )(*tabs, qa, ka, vt, *extras)


def _logf_kernel(f_ref, b_ref, o_ref):
    x = f_ref[:, 0:N_HEADS] + b_ref[...]
    o_ref[...] = jnp.minimum(x, 0.0) - jnp.log1p(jnp.exp(-jnp.abs(x)))


def _logf(p, f_col, b_f, tm):
    m = p.shape[0]
    return pl.pallas_call(
        _logf_kernel,
        grid=(m // tm,),
        in_specs=[pl.BlockSpec((tm, LANES), lambda i: (i, f_col)),
                  pl.BlockSpec((1, N_HEADS), lambda i: (0, 0))],
        out_specs=pl.BlockSpec((tm, N_HEADS), lambda i: (i, 0)),
        out_shape=jax.ShapeDtypeStruct((m, N_HEADS), F32),
        compiler_params=_params("parallel"),
    )(p, b_f.reshape(1, -1))


def _cumsum_kernel(x_ref, o_ref, carry_scr, *, t):
    @pl.when(pl.program_id(1) == 0)
    def _():
        carry_scr[...] = jnp.zeros(carry_scr.shape, F32)

    x = x_ref[0]
    tri = (_iota((t, t), 0) <= _iota((t, t), 1)).astype(F32)
    c = jnp.dot(x, tri, precision=HIGHEST, preferred_element_type=F32) + carry_scr[...]
    o_ref[0] = c
    carry_scr[...] = c[:, t - 1:t]


def _cumsum_lanes(x, t):
    b, h, s = x.shape
    return pl.pallas_call(
        functools.partial(_cumsum_kernel, t=t),
        grid=(b, s // t),
        in_specs=[pl.BlockSpec((1, h, t), lambda bi, n: (bi, 0, n))],
        out_specs=pl.BlockSpec((1, h, t), lambda bi, n: (bi, 0, n)),
        out_shape=jax.ShapeDtypeStruct(x.shape, F32),
        scratch_shapes=[pltpu.VMEM((h, 1), F32)],
        compiler_params=_params("parallel", "arbitrary"),
    )(x)


C_Z_COL = 1
C_KV_COL = 16
C_GATE_COL = 22
C_WIDTH = 3072


def _compress_kernel(kc_ref, vc_ref, wk_ref, wv_ref, ck_ref, cv_ref, *, nch):
    stride = 2 * C_CMP_BLOCK
    for src, w_ref, dst in ((kc_ref, wk_ref, ck_ref), (vc_ref, wv_ref, cv_ref)):
        for parity in range(2):
            acc = jnp.zeros((nch, LANES), F32)
            for tt in range(C_CMP_BLOCK):
                rows = src[pl.ds(parity * C_CMP_BLOCK + tt, nch, stride=stride), :]
                acc = acc + _dot(rows.astype(BF16), w_ref[tt])
            dst[0, parity * nch:(parity + 1) * nch, :] = acc


def _compress_prompt(p, batch, w2k, w2v):
    s = p.shape[0] // batch
    nch = s // (2 * C_CMP_BLOCK)
    col = lambda c: pl.BlockSpec((s, LANES), lambda b: (b, c))
    wspec = pl.BlockSpec((C_CMP_BLOCK, LANES, LANES), lambda b: (0, 0, 0))
    ospec = pl.BlockSpec((1, 2 * nch, LANES), lambda b: (b, 0, 0))
    oshape = jax.ShapeDtypeStruct((batch, 2 * nch, LANES), F32)
    return pl.pallas_call(
        functools.partial(_compress_kernel, nch=nch),
        grid=(batch,),
        in_specs=[col(C_KV_COL), col(C_KV_COL + 1), wspec, wspec],
        out_specs=[ospec, ospec],
        out_shape=[oshape, oshape],
        compiler_params=_params("parallel"),
    )(p, p, w2k, w2v)


def _compress_pool_kernel(x_ref, w_ref, o_ref):
    acc = jnp.zeros(o_ref.shape, F32)
    for d in range(HEAD_DIM):
        acc = acc + _dot(x_ref[:, d, :].astype(BF16), w_ref[d])
    o_ref[...] = acc


def _compress_pool(cache, w_cmp):
    n_pool = cache.shape[0]
    tokens = PAGE // C_CMP_BLOCK
    x = cache.transpose(0, 2, 3, 1).reshape(n_pool * C_GROUPS, HEAD_DIM, PAGE)
    r = jnp.arange(PAGE)
    in_token = (r[:, None] // C_CMP_BLOCK == jnp.arange(tokens)[None, :]).astype(w_cmp.dtype)
    w_rows = w_cmp.reshape(C_CMP_BLOCK, HEAD_DIM, HEAD_DIM)[r % C_CMP_BLOCK]
    wx = (w_rows[:, :, None, :] * in_token[:, None, :, None]).transpose(1, 0, 2, 3)
    wx = wx.reshape(HEAD_DIM, PAGE, tokens * HEAD_DIM).astype(BF16)
    tm = min(256, x.shape[0])
    assert x.shape[0] % tm == 0
    out = pl.pallas_call(
        _compress_pool_kernel,
        grid=(x.shape[0] // tm,),
        in_specs=[pl.BlockSpec((tm, HEAD_DIM, PAGE), lambda i: (i, 0, 0)),
                  pl.BlockSpec(wx.shape, lambda i: (0, 0, 0))],
        out_specs=pl.BlockSpec((tm, tokens * HEAD_DIM), lambda i: (i, 0)),
        out_shape=jax.ShapeDtypeStruct((x.shape[0], tokens * HEAD_DIM), F32),
        compiler_params=_params("parallel"),
    )(x, wx)
    return out.reshape(n_pool, C_GROUPS, tokens, HEAD_DIM).transpose(0, 2, 1, 3).reshape(n_pool, tokens, LANES)


def _group_halves(x, g):
    lo = _iota(x.shape, 1) < HEAD_DIM
    base = jnp.where(lo if g == 0 else ~lo, x, 0.0)
    other = pltpu.roll(base, HEAD_DIM, 1)
    pair = (base, other) if g == 0 else (other, base)
    return pair[0].astype(BF16), pair[1].astype(BF16)


def _nsa_cmp_kernel(sl_ref, q_ref, g_ref, ck_ref, cv_ref, o_ref, sel_ref, *, tq, nch):
    q0 = pl.program_id(1) * tq
    nl = 2 * nch
    lane = _iota((tq, nl), 1)
    qpos = q0 + _iota((tq, nl), 0)
    tok = jnp.where(lane < nch, 2 * lane, 2 * (lane - nch) + 1)
    endp = (tok + 1) * C_CMP_BLOCK - 1
    okc = endp <= qpos
    relc = (endp[0:1, :] - q0).astype(F32)
    lane_s = _iota((tq, nch), 1)
    qblk = (q0 + _iota((tq, nch), 0)) // C_SEL_BLOCK
    lane_sf = lane_s.astype(F32)
    for g in range(C_GROUPS):
        k_lo, k_hi = _group_halves(ck_ref[0], g)
        v_lo, v_hi = _group_halves(cv_ref[0], g)
        imp = jnp.zeros((tq, nl), F32)
        for hp in range(g * C_RATIO // 2, (g + 1) * C_RATIO // 2):
            cs = slice(hp * LANES, (hp + 1) * LANES)
            qp = (q_ref[:, cs] * SCALE).astype(BF16)
            o_pair = None
            for e in range(2):
                h = 2 * hp + e
                s = _dot_nt(qp, k_lo if e == 0 else k_hi) + sl_ref[h] * relc
                s = jnp.where(okc, s, NEG)
                pc = jnp.where(okc, jnp.exp(s - jnp.max(s, axis=1, keepdims=True)), 0.0)
                pc = pc / jnp.maximum(jnp.sum(pc, axis=1, keepdims=True), TINY)
                imp = imp + pc
                gate = jax.nn.sigmoid(g_ref[:, 3 * h:3 * h + 1])
                o = _dot(pc.astype(BF16), v_lo if e == 0 else v_hi) * gate
                o_pair = o if o_pair is None else o_pair + o
            o_ref[:, cs] = o_pair
        imp_sel = imp[:, 0:nch] + imp[:, nch:nl]
        sel = _topk_mask(imp_sel, lane_s < qblk, lane_sf, C_TOPK)
        sel = jnp.where(lane_s == qblk, 1.0, sel)
        if nch < LANES:
            sel = jnp.concatenate([sel, jnp.zeros((tq, LANES - nch), F32)], axis=1)
        sel_ref[:, g * LANES:(g + 1) * LANES] = sel


def _nsa_cmp(p, batch, slopes, ck, cv, tq):
    nq = p.shape[0] // batch // tq
    nch = ck.shape[1] // 2
    assert nch <= LANES
    row = lambda width, col: pl.BlockSpec((tq, width), lambda b, i: (b * nq + i, col))
    cspec = pl.BlockSpec((1, 2 * nch, LANES), lambda b, i: (b, 0, 0))
    return pl.pallas_call(
        functools.partial(_nsa_cmp_kernel, tq=tq, nch=nch),
        grid=(batch, nq),
        in_specs=[_smem(), row(D_MODEL, 0), row(LANES, C_GATE_COL), cspec, cspec],
        out_specs=[row(D_MODEL, 0), row(C_GROUPS * LANES, 0)],
        out_shape=[jax.ShapeDtypeStruct((p.shape[0], D_MODEL), F32),
                   jax.ShapeDtypeStruct((p.shape[0], C_GROUPS * LANES), F32)],
        compiler_params=_params("parallel", "parallel"),
    )(slopes, p, p, ck, cv)


def _decode_kernel(pt_ref, qb_ref, q_ref, kn_ref, vn_ref, *rest, n_pages, mode, lam_init):
    del pt_ref
    gp = PAGES_PER_STEP
    if mode == "moba":
        slc_ref, rest = rest[0], rest[1:]
    elif mode == "diff":
        slc_ref, lam_ref, g_ref, rest = rest[0], rest[1], rest[2], rest[3:]
    else:
        lfn_ref, lft_refs, rest = rest[0], rest[1:1 + gp], rest[1 + gp:]
    kt_refs, v_refs, rest = rest[:gp], rest[gp:2 * gp], rest[2 * gp:]
    if mode == "moba":
        o_ref, st_scr, p_scr, pn_scr, acc_scr, sb_scr = rest
    elif mode == "diff":
        o_ref, st_scr, p_scr, pn_scr, acc_scr = rest
    else:
        o_ref, st_scr, p_scr, pn_scr, acc_scr, bias_scr, carry_scr = rest
    s = pl.program_id(1)
    n_steps = n_pages // gp
    past = n_pages * PAGE
    per_blk = A_BLOCK // PAGE
    lane_b = _iota((N_HEADS, LANES), 1)

    @pl.when(s == 0)
    def _():
        acc_scr[...] = jnp.zeros(acc_scr.shape, F32)
        if mode == "moba":
            sb_scr[...] = jnp.zeros(sb_scr.shape, F32)
        if mode == "fox":
            carry_scr[...] = jnp.zeros(carry_scr.shape, F32)

    @pl.when(s < n_steps)
    def _():
        qb = qb_ref[0]
        for g in range(gp):
            page = s * gp + g
            off = pl.multiple_of(page * PAGE, PAGE)
            st_scr[:, pl.ds(off, PAGE)] = jnp.sum(kt_refs[g][0] * qb, axis=1)
            if mode == "moba" and g % per_blk == per_blk - 1:
                blk_sum = kt_refs[g][0]
                for back in range(1, per_blk):
                    blk_sum = blk_sum + kt_refs[g - back][0]
                bm = jnp.sum(blk_sum, axis=2, keepdims=True) / A_BLOCK
                col = jnp.sum(bm * (qb[:, :, 0:1] / SCALE), axis=1)
                sb_scr[...] = jnp.where(lane_b == page // per_blk, col, sb_scr[...])
            if mode == "fox":
                tri = (_iota((PAGE, PAGE), 0) <= _iota((PAGE, PAGE), 1)).astype(F32)
                c = jnp.dot(lft_refs[g][0], tri, precision=HIGHEST, preferred_element_type=F32) + carry_scr[...]
                bias_scr[:, pl.ds(off, PAGE)] = -c
                carry_scr[...] = c[:, PAGE - 1:PAGE]

    @pl.when(s == n_steps - 1)
    def _():
        q = q_ref[0]
        qk_new = jnp.sum(q * kn_ref[0], axis=1, keepdims=True)
        s_new = qk_new * SCALE
        if mode == "fox":
            s_all = st_scr[...] + bias_scr[...]
            s_new = s_new - (carry_scr[...] + lfn_ref[0])
        else:
            kpos = _iota((1, past), 1)
            s_all = st_scr[...] + slc_ref[...] * (kpos - past).astype(F32)
        if mode == "moba":
            sb = jnp.where(lane_b == past // A_BLOCK, qk_new / A_BLOCK, sb_scr[...])
            sel = _topk_mask(sb, lane_b < past // A_BLOCK, lane_b.astype(F32), A_TOPK)
            st_scr[...] = s_all
            for n in range(past // A_BLOCK):
                cols = slice(n * A_BLOCK, (n + 1) * A_BLOCK)
                st_scr[:, cols] = st_scr[:, cols] + jnp.where(sel[:, n:n + 1] > 0.5, 0.0, NEG)
            s_all = st_scr[...]
        m = jnp.maximum(jnp.max(s_all, axis=1, keepdims=True), s_new)
        p = jnp.exp(s_all - m)
        pn = jnp.exp(s_new - m)
        l = jnp.sum(p, axis=1, keepdims=True) + pn
        p_scr[...] = p / l
        pn_scr[...] = pn / l

    @pl.when(s >= n_steps)
    def _():
        acc = acc_scr[...]
        for g in range(gp):
            off = pl.multiple_of(((s - n_steps) * gp + g) * PAGE, PAGE)
            pg = p_scr[:, pl.ds(off, PAGE)]
            if mode == "diff":
                row = _iota((N_HEADS, LANES), 0)
                pg = pg.astype(BF16)
                for h in range(B_HEADS):
                    d = _dot(pg, v_refs[g][0, :, h, :].astype(BF16))
                    acc = acc + jnp.where(row // 2 == h, d, 0.0)
            else:
                acc = acc + v_refs[g][0] * pg[:, None, :]
        acc_scr[...] = acc

    @pl.when(s == 2 * n_steps - 1)
    def _():
        acc = acc_scr[...]
        if mode == "diff":
            row = _iota((N_HEADS, LANES), 0)
            acc = acc + pn_scr[...] * vn_ref[0]
            signed = jnp.where(row % 2 == 0, acc, -_lambda(lam_ref, lam_init) * acc)
            o = signed + pltpu.roll(signed, N_HEADS - 1, 0)
            o = o * lax.rsqrt(jnp.mean(o * o, axis=1, keepdims=True) + LN_EPS)
            o_ref[0] = o * g_ref[...] * (1.0 - lam_init)
        else:
            acc = acc + vn_ref[0] * jnp.broadcast_to(pn_scr[...], (N_HEADS, LANES))[:, None, :]
            o_ref[0] = jnp.sum(acc, axis=2, keepdims=True)


def _decode(q3, kn3, vn3, page_table, kt_pool, v_pool, mode, extras, lam_init=0.0):
    b = q3.shape[0]
    n_pages = page_table.shape[1]
    gp = PAGES_PER_STEP
    assert n_pages % gp == 0 and gp % (A_BLOCK // PAGE) == 0
    n_steps = n_pages // gp
    past = n_pages * PAGE
    pt = page_table.reshape(-1)
    qb = jnp.broadcast_to((q3 * SCALE)[..., None], q3.shape + (LANES,))
    per_b = lambda shape: pl.BlockSpec((1,) + shape, lambda bi, s, pt_: (bi,) + tuple(0 for _ in shape))
    full = lambda shape: pl.BlockSpec(shape, lambda bi, s, pt_: tuple(0 for _ in shape))
    kpage = lambda g, nd: (lambda bi, s, pt_: (pt_[bi * n_pages + jnp.minimum(s, n_steps - 1) * gp + g],) + (0,) * nd)
    vpage = lambda g: (lambda bi, s, pt_: (pt_[bi * n_pages + jnp.maximum(s - n_steps, 0) * gp + g], 0, 0, 0))
    tile = (N_HEADS, HEAD_DIM, LANES)
    kspecs = [pl.BlockSpec((1,) + tile, kpage(g, 3)) for g in range(gp)]
    vspecs = [pl.BlockSpec((1,) + v_pool.shape[1:], vpage(g)) for g in range(gp)]
    scratch = [pltpu.VMEM((N_HEADS, past), F32), pltpu.VMEM((N_HEADS, past), F32), pltpu.VMEM((N_HEADS, 1), F32)]
    args = list(extras)
    if mode == "diff":
        vn = jnp.repeat(vn3, 2, axis=1)
        out_tile = (N_HEADS, LANES)
        in_specs = [full((N_HEADS, 1)), full((4, HEAD_DIM)), full((1, LANES))]
        scratch += [pltpu.VMEM(out_tile, F32)]
    else:
        vn = vn3[..., None] * (jnp.arange(LANES) == 0)
        out_tile = (N_HEADS, HEAD_DIM, 1)
        scratch += [pltpu.VMEM(tile, F32)]
        if mode == "moba":
            in_specs = [full((N_HEADS, 1))]
            scratch += [pltpu.VMEM((N_HEADS, LANES), F32)]
        else:
            lft, lfn = extras
            in_specs = [per_b((N_HEADS, 1))] + [pl.BlockSpec((1, N_HEADS, PAGE), kpage(g, 2)) for g in range(gp)]
            args = [lfn] + [lft] * gp
            scratch += [pltpu.VMEM((N_HEADS, past), F32), pltpu.VMEM((N_HEADS, 1), F32)]
    out = pl.pallas_call(
        functools.partial(_decode_kernel, n_pages=n_pages, mode=mode, lam_init=lam_init),
        grid_spec=pltpu.PrefetchScalarGridSpec(
            num_scalar_prefetch=1,
            grid=(b, 2 * n_steps),
            in_specs=[per_b(tile), per_b(q3.shape[1:]), per_b(kn3.shape[1:]), per_b(vn.shape[1:])]
                     + in_specs + kspecs + vspecs,
            out_specs=per_b(out_tile),
            scratch_shapes=scratch),
        out_shape=jax.ShapeDtypeStruct((b,) + out_tile, F32),
        compiler_params=_params("parallel", "arbitrary"),
    )(pt, qb, q3, kn3, vn, *args, *([kt_pool] * gp), *([v_pool] * gp))
    if mode == "diff":
        out = out[:, 0::2, :]
    return out.reshape(b, D_MODEL)


C_HALF = 64


def _softmax_new(s_all, s_new, v_all, v_new, v_transposed):
    m = jnp.maximum(jnp.max(s_all, axis=1, keepdims=True), s_new)
    p = jnp.exp(s_all - m)
    pn = jnp.exp(s_new - m)
    l = jnp.sum(p, axis=1, keepdims=True) + pn
    pv = _dot_nt(p.astype(BF16), v_all.astype(BF16)) if v_transposed else _dot(p.astype(BF16), v_all.astype(BF16))
    return (pv + pn * v_new) / l


def _nsa_decode_kernel(pt_ref, qbd_ref, g_ref, slc_ref, kcn_ref, vcn_ref, ksn_ref, vsn_ref, kwn_ref, vwn_ref,
                       kw_ref, vw_ref, wk0_ref, wv0_ref, *rest, n_pages):
    del pt_ref
    gp = PAGES_PER_STEP
    ckp_refs, cvp_refs, kst_refs, vst_refs = (rest[i * gp:(i + 1) * gp] for i in range(4))
    o_ref, ck_scr, cv_scr, kst_scr, vst_scr = rest[4 * gp:]
    s = pl.program_id(1)
    past = n_pages * PAGE
    per_page = PAGE // C_CMP_BLOCK // 2

    @pl.when(s == 0)
    def _():
        ck_scr[...] = jnp.zeros(ck_scr.shape, F32)
        cv_scr[...] = jnp.zeros(cv_scr.shape, F32)

    for g in range(gp):
        page = s * gp + g
        off = pl.multiple_of(page * PAGE, PAGE)
        kst_scr[:, pl.ds(off, PAGE)] = kst_refs[g][0]
        vst_scr[:, pl.ds(off, PAGE)] = vst_refs[g][0]
        for u in range(2 * per_page):
            dst = (u % 2) * C_HALF + per_page * page + u // 2
            ck_scr[pl.ds(dst, 1), :] = ckp_refs[g][0, u:u + 1, :]
            cv_scr[pl.ds(dst, 1), :] = cvp_refs[g][0, u:u + 1, :]

    @pl.when(s == n_pages // gp - 1)
    def _():
        qf = qbd_ref[0] * SCALE
        qb = qf.astype(BF16)
        slc = slc_ref[...]
        new_tok = per_page * n_pages
        ck_scr[new_tok:new_tok + 1, :] = _dot(jnp.broadcast_to(kcn_ref[0], (8, LANES)).astype(BF16), wk0_ref[...])[0:1]
        cv_scr[new_tok:new_tok + 1, :] = _dot(jnp.broadcast_to(vcn_ref[0], (8, LANES)).astype(BF16), wv0_ref[...])[0:1]
        lane = _iota((N_HEADS, LANES), 1)
        row = _iota((N_HEADS, LANES), 0)
        tok = jnp.where(lane < C_HALF, 2 * lane, 2 * (lane - C_HALF) + 1)
        endp = (tok + 1) * C_CMP_BLOCK - 1
        okc = endp <= past
        sc = _dot_nt(qb, ck_scr[...].astype(BF16)) + slc * (endp - past).astype(F32)
        sc = jnp.where(okc, sc, NEG)
        pc = jnp.where(okc, jnp.exp(sc - jnp.max(sc, axis=1, keepdims=True)), 0.0)
        pc = pc / jnp.maximum(jnp.sum(pc, axis=1, keepdims=True), TINY)
        o_cmp = _dot(pc.astype(BF16), cv_scr[...].astype(BF16))
        imp = jnp.where(row < C_RATIO, jnp.sum(pc[0:C_RATIO], axis=0, keepdims=True),
                        jnp.sum(pc[C_RATIO:N_HEADS], axis=0, keepdims=True))
        imp = imp + pltpu.roll(imp, C_HALF, 1)
        sel = _topk_mask(imp, lane < past // C_SEL_BLOCK, lane.astype(F32), C_TOPK)
        expand = (_iota((LANES, past), 0) == _iota((LANES, past), 1) // C_SEL_BLOCK).astype(BF16)
        picked = _dot(sel.astype(BF16), expand)
        kpos = _iota((1, past), 1)
        ss = _dot(qb, kst_scr[...].astype(BF16)) + slc * (kpos - past).astype(F32)
        ss = jnp.where(picked > 0.5, ss, NEG)
        ss_new = jnp.sum(qf * ksn_ref[0], axis=1, keepdims=True)
        o_sel = _softmax_new(ss, ss_new, vst_scr[...], vsn_ref[0], True)
        wb = kw_ref.shape[1]
        wpos = past - wb + _iota((1, wb), 1)
        okw = (past - wpos <= C_WINDOW) & (wpos >= 0)
        sw = _dot_nt(qb, kw_ref[0].astype(BF16)) + slc * (wpos - past).astype(F32)
        sw = jnp.where(okw, sw, NEG)
        sw_new = jnp.sum(qf * kwn_ref[0], axis=1, keepdims=True)
        o_win = _softmax_new(sw, sw_new, vw_ref[0], vwn_ref[0], False)
        gate = jax.nn.sigmoid(g_ref[0])
        o_ref[0] = gate[:, 0:1] * o_cmp + gate[:, 1:2] * o_sel + gate[:, 2:3] * o_win


def _nsa_decode(ps3, qbd, gates, slc, page_table, ck_pool, cv_pool, kst_pool, vst_pool, kw_buf, vw_buf, wk0, wv0):
    b = ps3.shape[0]
    n_pages = page_table.shape[1]
    gp = PAGES_PER_STEP
    past = n_pages * PAGE
    assert 2 * n_pages + 1 <= C_HALF and n_pages % gp == 0
    pt = page_table.reshape(-1)
    per_b = lambda shape: pl.BlockSpec((1,) + shape, lambda bi, s, pt_: (bi,) + tuple(0 for _ in shape))
    full = lambda shape: pl.BlockSpec(shape, lambda bi, s, pt_: tuple(0 for _ in shape))
    newrow = lambda col: pl.BlockSpec((1, 1, LANES), lambda bi, s, pt_: (bi, 0, col))
    paged = lambda rows: [pl.BlockSpec((1, rows, LANES), (lambda g: lambda bi, s, pt_: (pt_[bi * n_pages + s * gp + g], 0, 0))(g))
                          for g in range(gp)]
    tokens = PAGE // C_CMP_BLOCK
    return pl.pallas_call(
        functools.partial(_nsa_decode_kernel, n_pages=n_pages),
        grid_spec=pltpu.PrefetchScalarGridSpec(
            num_scalar_prefetch=1,
            grid=(b, n_pages // gp),
            in_specs=[per_b((N_HEADS, LANES)), per_b((N_HEADS, 3)), full((N_HEADS, 1))]
                     + [newrow(C_KV_COL + c) for c in range(6)]
                     + [per_b(kw_buf.shape[1:]), per_b(vw_buf.shape[1:]), full((LANES, LANES)), full((LANES, LANES))]
                     + paged(tokens) + paged(tokens) + paged(PAGE) + paged(PAGE),
            out_specs=per_b((N_HEADS, LANES)),
            scratch_shapes=[pltpu.VMEM((2 * C_HALF, LANES), F32), pltpu.VMEM((2 * C_HALF, LANES), F32),
                            pltpu.VMEM((LANES, past), F32), pltpu.VMEM((LANES, past), F32)]),
        out_shape=jax.ShapeDtypeStruct((b, N_HEADS, LANES), F32),
        compiler_params=_params("parallel", "arbitrary"),
    )(pt, qbd, gates, slc, ps3, ps3, ps3, ps3, ps3, ps3, kw_buf, vw_buf, wk0, wv0,
      *([ck_pool] * gp), *([cv_pool] * gp), *([kst_pool] * gp), *([vst_pool] * gp))


def _pad_cols(w, width):
    return jnp.pad(w, ((0, 0), (0, width - w.shape[1])))


def _block_diag_cmp(w):
    w3 = w.reshape(C_CMP_BLOCK, HEAD_DIM, HEAD_DIM)
    z = jnp.zeros_like(w3)
    return jnp.concatenate([jnp.concatenate([w3, z], 2), jnp.concatenate([z, w3], 2)], 1).astype(BF16)


def kernel(x_prompt, x_sample, cache_a_k, cache_a_v, cache_b_k, cache_b_v, cache_c_kc, cache_c_vc, cache_c_ks, cache_c_vs, state_c_kw, state_c_vw, cache_d_k, cache_d_v, cache_d_logf, page_table, w_in_0, w_out_0, ln_g_0, ln_b_0, w_in_1, lam_q1_1, lam_k1_1, lam_q2_1, lam_k2_1, subln_g_1, w_out_1, ln_g_1, ln_b_1, w_in_2, w_cmp_k_2, w_cmp_v_2, w_out_2, ln_g_2, ln_b_2, w_in_3, b_f_3, w_out_3, ln_g_3, ln_b_3):
    bp, sp, _ = x_prompt.shape
    bs = x_sample.shape[0]
    assert x_sample.shape[1] == 1 and sp % C_WINDOW == 0
    n_pool = cache_a_k.shape[0]
    mp = bp * sp
    tm = 512
    xp = x_prompt.reshape(mp, D_MODEL)
    xs = x_sample.reshape(bs, D_MODEL)
    heads = lambda a, b, l, h: a.reshape(b, l, h, -1)
    split_slopes = lambda sl: jnp.stack(_split3(sl * LOG2E), axis=1)
    vcols = slice(2 * D_MODEL, 3 * D_MODEL)
    hd3 = lambda p, c: p[:, c * D_MODEL:(c + 1) * D_MODEL].reshape(-1, N_HEADS, HEAD_DIM)
    rows_last = lambda c: c.transpose(0, 2, 3, 1)

    w = w_in_0.astype(BF16)
    slopes16 = _alibi_slopes(N_HEADS)
    sl3_16 = split_slopes(slopes16)
    pp = _matmul(xp, w, tm)
    ps = _matmul(xs, w, bs)
    a_k_p, a_v_p = heads(pp[:, 1024:2048], bp, sp, 16), heads(pp[:, 2048:3072], bp, sp, 16)
    a_k_s, a_v_s = heads(ps[:, 1024:2048], bs, 1, 16), heads(ps[:, 2048:3072], bs, 1, 16)
    qa, ka = _prep_moba(pp, bp, sl3_16, _block_mean(pp, bp, 1))
    o_p = _flash_t(qa, ka, _matmul_t(xp, w[:, vcols].T, tm), bp, fin="plain")
    o_s = _decode(hd3(ps, 0), hd3(ps, 1), hd3(ps, 2), page_table, rows_last(cache_a_k), rows_last(cache_a_v), "moba",
                  [slopes16.reshape(-1, 1)])
    w_o = w_out_0.astype(BF16)
    xp = _out_ln(o_p, pp, 3, xp, w_o, ln_g_0, ln_b_0, tm)
    xs = _out_ln(o_s, ps, 3, xs, w_o, ln_g_0, ln_b_0, bs)

    lam_init = 0.8 - 0.6 * math.exp(-0.3 * 1)
    w = w_in_1.astype(BF16)
    slopes8 = jnp.repeat(_alibi_slopes(B_HEADS), 2)
    lamv = jnp.stack([lam_q1_1, lam_k1_1, lam_q2_1, lam_k2_1])
    pp = _matmul(xp, w, tm)
    ps = _matmul(xs, w, bs)
    b_k_p, b_v_p = heads(pp[:, 1024:2048], bp, sp, 16), heads(pp[:, 2048:3072], bp, sp, 8)
    b_k_s, b_v_s = heads(ps[:, 1024:2048], bs, 1, 16), heads(ps[:, 2048:3072], bs, 1, 8)
    qa, ka = _prep(pp, bp, "diff", 1, [split_slopes(slopes8)], tm)
    o_p = _flash_t(qa, ka, _matmul_t(xp, w[:, vcols].T, tm), bp, fin="diff",
                   extras=(lamv, subln_g_1.reshape(1, -1)), lam_init=lam_init)
    o_s = _decode(hd3(ps, 0), hd3(ps, 1), ps[:, vcols].reshape(bs, B_HEADS, LANES), page_table, rows_last(cache_b_k),
                  cache_b_v, "diff", [slopes8.reshape(-1, 1), lamv, subln_g_1.reshape(1, -1)], lam_init)
    w_o = w_out_1.astype(BF16)
    xp = _out_ln(o_p, pp, 3, xp, w_o, ln_g_1, ln_b_1, tm)
    xs = _out_ln(o_s, ps, 3, xs, w_o, ln_g_1, ln_b_1, bs)

    kv0, z0, g0 = D_MODEL, D_MODEL + 6 * LANES, 2 * D_MODEL + 6 * LANES
    w = _pad_cols(jnp.concatenate([w_in_2[:, :kv0], w_in_2[:, z0:g0], w_in_2[:, kv0:z0], w_in_2[:, g0:]], 1),
                  C_WIDTH).astype(BF16)
    w2k, w2v = _block_diag_cmp(w_cmp_k_2), _block_diag_cmp(w_cmp_v_2)
    pp = _matmul(xp, w, tm)
    ps = _matmul(xs, w, bs)
    kvcol = lambda p, c: p[:, (C_KV_COL + c) * LANES:(C_KV_COL + c + 1) * LANES]
    c_p = [heads(kvcol(pp, c), bp, sp, 2) for c in range(6)]
    c_s = [heads(kvcol(ps, c), bs, 1, 2) for c in range(6)]
    keep = min(C_WINDOW, sp)
    ck, cv = _compress_prompt(pp, bp, w2k, w2v)
    o_p, sel = _nsa_cmp(pp, bp, slopes16, ck, cv, 256)
    qa, ka = _prep(pp, bp, "sel", C_KV_COL + 2, [sl3_16, sel], tm)
    blocks = tm // C_SEL_BLOCK
    live = (sel.reshape(bp, sp // tm, tm, C_GROUPS, LANES // blocks, blocks).max((2, 3, 5)) > 0.5)
    o_p = _flash_t(qa, ka, _matmul_t(xp, kvcol(w, 3).T, tm), bp, fin="nsa", extras=(pp, o_p), gate_idx=1, live=live)
    qa, ka = _prep(pp, bp, "win", C_KV_COL + 4, [sl3_16], tm)
    o_p = _flash_t(qa, ka, _matmul_t(xp, kvcol(w, 5).T, tm), bp, fin="nsa", band=True, extras=(pp, o_p),
                   gate_idx=2)
    ck_pool = _compress_pool(cache_c_kc, w_cmp_k_2)
    cv_pool = _compress_pool(cache_c_vc, w_cmp_v_2)
    q_s = ps[:, :D_MODEL].reshape(bs, N_HEADS, 1, HEAD_DIM)
    in_group = (jnp.arange(N_HEADS) // C_RATIO)[:, None] == jnp.arange(C_GROUPS)[None, :]
    qbd = jnp.where(in_group[None, :, :, None], q_s, 0.0).reshape(bs, N_HEADS, LANES)
    gates_s = ps[:, C_GATE_COL * LANES:C_GATE_COL * LANES + 3 * N_HEADS].reshape(bs, N_HEADS, 3)
    wb = state_c_kw.shape[1]
    o16 = _nsa_decode(ps.reshape(bs, 1, -1), qbd, gates_s, slopes16.reshape(-1, 1), page_table, ck_pool, cv_pool,
                      rows_last(cache_c_ks).reshape(n_pool, LANES, PAGE), rows_last(cache_c_vs).reshape(n_pool, LANES, PAGE),
                      state_c_kw.reshape(bs, wb, LANES),
                      state_c_vw.reshape(bs, wb, LANES), w2k[0], w2v[0])
    o16 = o16.reshape(bs, N_HEADS, C_GROUPS, HEAD_DIM)
    o_s = jnp.where(in_group[None, :, :, None], o16, 0.0).sum(2).reshape(bs, D_MODEL)
    c_kw_s = jnp.concatenate([state_c_kw, c_s[4]], 1)[:, -min(C_WINDOW, wb + 1):]
    c_vw_s = jnp.concatenate([state_c_vw, c_s[5]], 1)[:, -min(C_WINDOW, wb + 1):]
    w_o = w_out_2.astype(BF16)
    xp = _out_ln(o_p, pp, C_Z_COL, xp, w_o, ln_g_2, ln_b_2, tm)
    xs = _out_ln(o_s, ps, C_Z_COL, xs, w_o, ln_g_2, ln_b_2, bs)

    f_col = 4 * D_MODEL // LANES
    w = _pad_cols(w_in_3, 4 * D_MODEL + LANES).astype(BF16)
    pp = _matmul(xp, w, tm)
    ps = _matmul(xs, w, bs)
    d_k_p, d_v_p = heads(pp[:, 1024:2048], bp, sp, 16), heads(pp[:, 2048:3072], bp, sp, 16)
    d_k_s, d_v_s = heads(ps[:, 1024:2048], bs, 1, 16), heads(ps[:, 2048:3072], bs, 1, 16)
    lf_p = _logf(pp, f_col, b_f_3, tm).reshape(bp, sp, N_HEADS)
    lf_s = _logf(ps, f_col, b_f_3, bs).reshape(bs, 1, N_HEADS)
    c_p3 = _cumsum_lanes(lf_p.transpose(0, 2, 1), tm).transpose(0, 2, 1).reshape(mp, N_HEADS)
    qa, ka = _prep(pp, bp, "fox", 1, [c_p3], tm)
    o_p = _flash_t(qa, ka, _matmul_t(xp, w[:, vcols].T, tm), bp, fin="plain")
    o_s = _decode(hd3(ps, 0), hd3(ps, 1), hd3(ps, 2), page_table, rows_last(cache_d_k), rows_last(cache_d_v), "fox",
                  [cache_d_logf.transpose(0, 2, 1), lf_s.reshape(bs, N_HEADS, 1)])
    w_o = w_out_3.astype(BF16)
    xp = _out_ln(o_p, pp, 3, xp, w_o, ln_g_3, ln_b_3, tm)
    xs = _out_ln(o_s, ps, 3, xs, w_o, ln_g_3, ln_b_3, bs)

    return (xp.reshape(bp, sp, D_MODEL), xs.reshape(bs, 1, D_MODEL),
            a_k_p, a_v_p, a_k_s, a_v_s, b_k_p, b_v_p, b_k_s, b_v_s,
            c_p[0], c_p[1], c_p[2], c_p[3], c_p[4][:, -keep:], c_p[5][:, -keep:],
            c_s[0], c_s[1], c_s[2], c_s[3], c_kw_s, c_vw_s,
            d_k_p, d_v_p, lf_p, d_k_s, d_v_s, lf_s)
```

```python
import functools
import math

import numpy as np
import jax
import jax.numpy as jnp
from jax import lax
from jax.experimental import pallas as pl
from jax.experimental.pallas import tpu as pltpu

F32 = jnp.float32
BF16 = jnp.bfloat16
HIGHEST = lax.Precision.HIGHEST

D_MODEL = 1024
HEAD_DIM = 64
N_HEADS = 16
LANES = 128
SCALE = HEAD_DIM ** -0.5
PAGE = 128
DEPTH = 4
ALPHA = (2 * DEPTH) ** 0.25
LN_EPS = 1e-5
NEG = -1e30
TINY = 1e-30
A_BLOCK = 256
A_TOPK = 3
B_HEADS = 8
C_GROUPS = 2
C_RATIO = 8
C_CMP_BLOCK = 32
C_SEL_BLOCK = 64
C_TOPK = 4
C_WINDOW = 512
PAGES_PER_STEP = 8
NSA_PAGES_PER_STEP = 4
VMEM_LIMIT = 56 * 1024 * 1024


def _dot_nt(a, b):
    return lax.dot_general(a, b, (((1,), (1,)), ((), ())), preferred_element_type=F32)


def _dot(a, b):
    return jnp.dot(a, b, preferred_element_type=F32)


def _iota(shape, dim):
    return lax.broadcasted_iota(jnp.int32, shape, dim)


def _params(*sem):
    return pltpu.CompilerParams(dimension_semantics=sem, vmem_limit_bytes=VMEM_LIMIT)


def _alibi_slopes(n):
    return jnp.exp2(-8.0 * jnp.arange(1, n + 1, dtype=F32) / n)


def _smem():
    return pl.BlockSpec(memory_space=pltpu.SMEM)


def _mm_kernel(x_ref, w_ref, o_ref):
    o_ref[...] = _dot(x_ref[...].astype(BF16), w_ref[...])


def _pick_tn(n):
    best = LANES
    for t in range(LANES, 1536 + 1, LANES):
        if n % t == 0:
            best = t
    return best


def _matmul(x, w, tm):
    m, k = x.shape
    n = w.shape[1]
    tm = min(tm, m)
    assert m % tm == 0
    tn = _pick_tn(n)
    return pl.pallas_call(
        _mm_kernel,
        grid=(m // tm, n // tn),
        in_specs=[pl.BlockSpec((tm, k), lambda i, j: (i, 0)),
                  pl.BlockSpec((k, tn), lambda i, j: (0, j))],
        out_specs=pl.BlockSpec((tm, tn), lambda i, j: (i, j)),
        out_shape=jax.ShapeDtypeStruct((m, n), F32),
        compiler_params=_params("parallel", "arbitrary"),
    )(x, w)


def _vt_kernel(x_ref, w_ref, o_ref):
    o_ref[...] = _dot_nt(w_ref[...], x_ref[...].astype(BF16)).astype(BF16)


def _matmul_t(x, wt, tm):
    m, k = x.shape
    n = wt.shape[0]
    return pl.pallas_call(
        _vt_kernel,
        grid=(m // tm,),
        in_specs=[pl.BlockSpec((tm, k), lambda i: (i, 0)), pl.BlockSpec((n, k), lambda i: (0, 0))],
        out_specs=pl.BlockSpec((n, tm), lambda i: (0, i)),
        out_shape=jax.ShapeDtypeStruct((n, m), BF16),
        compiler_params=_params("parallel"),
    )(x, wt)


def _out_ln_kernel(o_ref, z_ref, x_ref, w_ref, g_ref, b_ref, y_ref):
    z = z_ref[...]
    a = (o_ref[...] * (z * jax.nn.sigmoid(z))).astype(BF16)
    h = ALPHA * x_ref[...] + _dot(a, w_ref[...])
    hc = h - jnp.mean(h, axis=-1, keepdims=True)
    var = jnp.mean(hc * hc, axis=-1, keepdims=True)
    y_ref[...] = hc * lax.rsqrt(var + LN_EPS) * g_ref[...] + b_ref[...]


def _out_ln(o, p, z_col, x, w, g, b, tm):
    m = x.shape[0]
    row = lambda i: (i, 0)
    return pl.pallas_call(
        _out_ln_kernel,
        grid=(m // tm,),
        in_specs=[pl.BlockSpec((tm, D_MODEL), row),
                  pl.BlockSpec((tm, D_MODEL), lambda i: (i, z_col)),
                  pl.BlockSpec((tm, D_MODEL), row),
                  pl.BlockSpec((D_MODEL, D_MODEL), lambda i: (0, 0)),
                  pl.BlockSpec((1, D_MODEL), lambda i: (0, 0)),
                  pl.BlockSpec((1, D_MODEL), lambda i: (0, 0))],
        out_specs=pl.BlockSpec((tm, D_MODEL), row),
        out_shape=jax.ShapeDtypeStruct((m, D_MODEL), F32),
        compiler_params=_params("parallel"),
    )(o, p, x, w, g.reshape(1, -1), b.reshape(1, -1))


def _topk_mask(s, cand, lane_f, k):
    s = jnp.where(cand, s, NEG)
    sel = jnp.zeros(s.shape, F32)
    for _ in range(k):
        mx = jnp.max(s, axis=1, keepdims=True)
        idx = jnp.min(jnp.where(s == mx, lane_f, 1e9), axis=1, keepdims=True)
        pick = lane_f == idx
        valid = jnp.where(mx > 0.5 * NEG, 1.0, 0.0)
        sel = jnp.where(pick, valid, sel)
        s = jnp.where(pick, -3e38, s)
    return sel


def _lambda(lam_ref, lam_init):
    a = lam_ref[...]
    return (jnp.exp(jnp.sum(a[0:1] * a[1:2], axis=1, keepdims=True))
            - jnp.exp(jnp.sum(a[2:3] * a[3:4], axis=1, keepdims=True)) + lam_init)


def _block_mean_kernel(k_ref, o_ref):
    n = pl.program_id(1)

    @pl.when(n == 0)
    def _():
        o_ref[...] = jnp.zeros(o_ref.shape, F32)

    o_ref[0, pl.ds(n, 1), :] = jnp.sum(k_ref[...], axis=0, keepdims=True) / A_BLOCK


def _block_mean(p, batch, k_col):
    nb = p.shape[0] // batch // A_BLOCK
    return pl.pallas_call(
        _block_mean_kernel,
        grid=(batch, nb),
        in_specs=[pl.BlockSpec((A_BLOCK, D_MODEL), lambda b, n: (b * nb + n, k_col))],
        out_specs=pl.BlockSpec((1, LANES, D_MODEL), lambda b, n: (b, 0, 0)),
        out_shape=jax.ShapeDtypeStruct((batch, LANES, D_MODEL), F32),
        compiler_params=_params("parallel", "arbitrary"),
    )(p)


LOG2E = math.log2(math.e)
AUX0 = HEAD_DIM
SEL0 = 96
Q_SCALE = SCALE * LOG2E


def _split3(x):
    hi = x.astype(BF16).astype(F32)
    mid = (x - hi).astype(BF16).astype(F32)
    return hi, mid, x - hi - mid


def _lane_pick(lane, base, vals):
    out = jnp.zeros(lane.shape, F32)
    for idx, v in enumerate(vals):
        out = jnp.where(lane == base + idx, v, out)
    return out


def _slope_lanes(lane, sl_ref, h):
    parts = [sl_ref[h, c] for c in range(3)]
    return _lane_pick(lane, AUX0, parts + parts)


def _pos_lanes(lane, kpos):
    hi = kpos.astype(BF16).astype(F32)
    lo = kpos - hi
    return _lane_pick(lane, AUX0, [hi, hi, hi, lo, lo, lo])


def _head_cols(ref, h):
    x = ref[:, (h // 2) * LANES:(h // 2 + 1) * LANES]
    return x if h % 2 == 0 else pltpu.roll(x, HEAD_DIM, 1)


def _prep_moba_kernel(sl_ref, q_ref, k_ref, bm_ref, qa_ref, ka_ref, *, tm):
    i = pl.program_id(1)
    lane = _iota((tm, LANES), 1)
    lane_f = lane.astype(F32)
    low = lane < HEAD_DIM
    kpos = (i * tm + _iota((tm, LANES), 0)).astype(F32)
    k_aux = jnp.where(lane >= SEL0, jnp.where(lane == SEL0 + i, 1.0, 0.0), _pos_lanes(lane, kpos))
    bm_low = _iota((LANES, LANES), 1) < HEAD_DIM
    for h in range(N_HEADS):
        cs = slice(h * LANES, (h + 1) * LANES)
        q = _head_cols(q_ref, h)
        bm = bm_ref[0, :, (h // 2) * LANES:(h // 2 + 1) * LANES]
        bme = jnp.where(bm_low if h % 2 == 0 else ~bm_low, bm, 0.0).astype(BF16)
        qsel = q_ref[:, (h // 2) * LANES:(h // 2 + 1) * LANES].astype(BF16)
        sel = _topk_mask(_dot_nt(qsel, bme), lane < i, lane_f, A_TOPK)
        penalty = pltpu.roll(jnp.where((sel > 0.5) | (lane == i), 0.0, NEG), SEL0, 1)
        q_aux = jnp.where(lane >= SEL0, penalty, _slope_lanes(lane, sl_ref, h))
        qa_ref[:, cs] = jnp.where(low, q * Q_SCALE, q_aux).astype(BF16)
        ka_ref[:, cs] = jnp.where(low, _head_cols(k_ref, h), k_aux).astype(BF16)


def _prep_moba(p, batch, sl3, bm):
    tm = A_BLOCK
    nt = p.shape[0] // batch // tm
    assert nt <= LANES - SEL0
    row = lambda col: pl.BlockSpec((tm, D_MODEL), lambda b, i: (b * nt + i, col))
    aug = pl.BlockSpec((tm, N_HEADS * LANES), lambda b, i: (b * nt + i, 0))
    shape = jax.ShapeDtypeStruct((p.shape[0], N_HEADS * LANES), BF16)
    return pl.pallas_call(
        functools.partial(_prep_moba_kernel, tm=tm),
        grid=(batch, nt),
        in_specs=[_smem(), row(0), row(1), pl.BlockSpec((1, LANES, D_MODEL), lambda b, i: (b, 0, 0))],
        out_specs=[aug, aug],
        out_shape=[shape, shape],
        compiler_params=_params("parallel", "parallel"),
    )(sl3, p, p, bm)


def _prep_kernel(*refs, tm, mode):
    if mode == "diff":
        sl_ref, q_ref, k_ref, qa_ref, ka_ref = refs
    elif mode == "fox":
        q_ref, k_ref, c_ref, qa_ref, ka_ref = refs
    elif mode == "win":
        sl_ref, q_ref, k_ref, qa_ref, ka_ref = refs
    else:
        sl_ref, q_ref, k_ref, sel_ref, qa_ref, ka_ref = refs
    i = pl.program_id(1)
    lane = _iota((tm, LANES), 1)
    low = lane < HEAD_DIM
    kpos = (i * tm + _iota((tm, LANES), 0)).astype(F32)
    cw = 2 * LANES if mode == "sel" else LANES
    for h in range(N_HEADS):
        if mode == "fox":
            q_aux = _lane_pick(lane, AUX0, [-1.0, -1.0, -1.0])
        else:
            q_aux = _slope_lanes(lane, sl_ref, h)
        qa_ref[:, h * cw:h * cw + LANES] = jnp.where(low, _head_cols(q_ref, h) * Q_SCALE, q_aux).astype(BF16)
        if mode == "sel":
            g = h // C_RATIO
            picked = sel_ref[:, g * LANES:(g + 1) * LANES]
            qa_ref[:, h * cw + LANES:(h + 1) * cw] = jnp.where(picked > 0.5, 0.0, NEG).astype(BF16)
        if mode == "diff":
            ka_ref[:, h * cw:(h + 1) * cw] = jnp.where(low, _head_cols(k_ref, h), _pos_lanes(lane, kpos)).astype(BF16)
        if mode == "fox":
            c = jnp.broadcast_to(c_ref[:, h:h + 1], (tm, LANES)) * LOG2E
            ka_ref[:, h * cw:(h + 1) * cw] = jnp.where(low, _head_cols(k_ref, h), _lane_pick(lane, AUX0, _split3(c))).astype(BF16)
    if mode in ("win", "sel"):
        for g in range(C_GROUPS):
            ka_ref[:, g * cw:g * cw + LANES] = jnp.where(low, _head_cols(k_ref, g), _pos_lanes(lane, kpos)).astype(BF16)
            if mode == "sel":
                own = (i * tm + _iota((tm, LANES), 0)) // C_SEL_BLOCK
                ka_ref[:, g * cw + LANES:(g + 1) * cw] = jnp.where(lane == own, 1.0, 0.0).astype(BF16)


def _prep(p, batch, mode, k_col, extras, tm):
    nt = p.shape[0] // batch // tm
    cw = 2 * LANES if mode == "sel" else LANES
    nk = C_GROUPS if mode in ("win", "sel") else N_HEADS
    kw = LANES if mode in ("win", "sel") else D_MODEL
    row = lambda width, col: pl.BlockSpec((tm, width), lambda b, i: (b * nt + i, col))
    in_specs = ([] if mode == "fox" else [_smem()]) + [row(D_MODEL, 0), row(kw, k_col)]
    if mode == "fox":
        in_specs.append(row(N_HEADS, 0))
    if mode == "sel":
        assert extras[-1].shape[1] == C_GROUPS * LANES
        in_specs.append(row(C_GROUPS * LANES, 0))
    return pl.pallas_call(
        functools.partial(_prep_kernel, tm=tm, mode=mode),
        grid=(batch, nt),
        in_specs=in_specs,
        out_specs=[row(N_HEADS * cw, 0), row(nk * cw, 0)],
        out_shape=[jax.ShapeDtypeStruct((p.shape[0], N_HEADS * cw), BF16),
                   jax.ShapeDtypeStruct((p.shape[0], nk * cw), BF16)],
        compiler_params=_params("parallel", "parallel"),
    )(*(extras[:1] if mode != "fox" else []), p, p, *(extras if mode == "fox" else extras[1:]))


def _sweep_tables(nq, band):
    it, jt, ft, mt = [], [], [], []
    for i in range(nq):
        js = [j for j in (i - 1, i) if j >= 0] if band else list(range(i + 1))
        for n, j in enumerate(js):
            it.append(i)
            jt.append(j)
            ft.append(1 if n == 0 else 0)
            mt.append(1 if j == i else (2 if band else 0))
    return [np.asarray(a, np.int32) for a in (it, jt, ft, mt)]


def _flash_t_kernel(it_ref, jt_ref, ft_ref, mt_ref, *rest, t, cw, kdiv, vdiv, dv, fin, modes, lam_init, gate_idx,
                    has_live):
    if has_live:
        live_ref, rest = rest[0], rest[1:]
    q_ref, k_ref, v_ref, rest = rest[0], rest[1], rest[2], rest[3:]
    if fin == "diff":
        lam_ref, g_ref, o_ref, m_scr, l_scr, acc_scr = rest
    elif fin == "nsa":
        g_ref, prev_ref, o_ref, m_scr, l_scr, acc_scr = rest
    else:
        o_ref, m_scr, l_scr, acc_scr = rest
    st = pl.program_id(1)

    @pl.when(ft_ref[st] == 1)
    def _():
        m_scr[...] = jnp.full(m_scr.shape, NEG, F32)
        l_scr[...] = jnp.zeros(l_scr.shape, F32)
        acc_scr[...] = jnp.zeros(acc_scr.shape, F32)

    def tile(mode):
        if mode:
            diff = _iota((t, t), 0) - _iota((t, t), 1)
            allowed = diff <= 0 if mode == 1 else diff >= 0

        def scores(h):
            kh = k_ref[:, (h // kdiv) * cw:(h // kdiv + 1) * cw]
            return _dot_nt(kh, q_ref[:, h * cw:(h + 1) * cw])

        s_next = scores(0)
        for h in range(N_HEADS):
            s = s_next
            if h + 1 < N_HEADS:
                s_next = scores(h + 1)
            if mode:
                s = jnp.where(allowed, s, NEG)
            m_prev = m_scr[h:h + 1, :]
            m_new = jnp.maximum(m_prev, jnp.max(s, axis=0, keepdims=True))
            p = jnp.exp2(s - m_new)
            corr = jnp.exp2(m_prev - m_new)
            l_scr[h:h + 1, :] = corr * l_scr[h:h + 1, :] + jnp.sum(p, axis=0, keepdims=True)
            m_scr[h:h + 1, :] = m_new
            rows = slice(h * dv, (h + 1) * dv)
            vh = v_ref[(h // vdiv) * dv:(h // vdiv + 1) * dv, :]
            acc_scr[rows, :] = acc_scr[rows, :] * corr + _dot(vh, p.astype(BF16))

    for mode in modes:
        run = mt_ref[st] == mode
        if has_live and mode == 0:
            run = run & (live_ref[pl.program_id(0) * pl.num_programs(1) + st] != 0)
        pl.when(run)(functools.partial(tile, mode))

    @pl.when(it_ref[st] == jt_ref[st])
    def _():
        lo = _iota((t, LANES), 1) < HEAD_DIM
        for hp in range(N_HEADS // 2):
            cs = slice(hp * LANES, (hp + 1) * LANES)
            if fin == "diff":
                lam = _lambda(lam_ref, lam_init)
                o = (acc_scr[2 * hp * dv:(2 * hp + 1) * dv, :] / l_scr[2 * hp:2 * hp + 1, :]
                     - lam * (acc_scr[(2 * hp + 1) * dv:(2 * hp + 2) * dv, :] / l_scr[2 * hp + 1:2 * hp + 2, :]))
                o = o * lax.rsqrt(jnp.mean(o * o, axis=0, keepdims=True) + LN_EPS)
                o_ref[:, cs] = o.T * g_ref[...] * (1.0 - lam_init)
            else:
                o = jnp.concatenate(
                    [acc_scr[(2 * hp + e) * dv:(2 * hp + e + 1) * dv, :] / l_scr[2 * hp + e:2 * hp + e + 1, :]
                     for e in range(2)], axis=0).T
                if fin == "nsa":
                    g0 = jax.nn.sigmoid(g_ref[:, 6 * hp + gate_idx:6 * hp + gate_idx + 1])
                    g1 = jax.nn.sigmoid(g_ref[:, 6 * hp + 3 + gate_idx:6 * hp + 4 + gate_idx])
                    o = prev_ref[:, cs] + o * jnp.where(lo, g0, g1)
                o_ref[:, cs] = o


def _flash_t(qa, ka, vt, batch, *, fin, band=False, extras=(), lam_init=0.0, gate_idx=0, live=None, t=512):
    m = qa.shape[0]
    nq = m // batch // t
    cw = qa.shape[1] // N_HEADS
    kdiv = N_HEADS // (ka.shape[1] // cw)
    dv = LANES if fin == "diff" else HEAD_DIM
    vdiv = N_HEADS // (vt.shape[0] // dv)
    tabs = _sweep_tables(nq, band)
    if live is not None:
        tabs.append(live[:, tabs[0], tabs[1]].reshape(-1).astype(jnp.int32))
    modes = (1, 2) if band else (0, 1)
    assert not band or t == C_WINDOW
    imap = lambda f: (lambda b, s, it_, jt_, *_: f(b, s, it_, jt_))
    qrow = lambda width: pl.BlockSpec((t, width), imap(lambda b, s, it_, jt_: (b * nq + it_[s], 0)))
    in_specs = [qrow(qa.shape[1]),
                pl.BlockSpec((t, ka.shape[1]), imap(lambda b, s, it_, jt_: (b * nq + jt_[s], 0))),
                pl.BlockSpec((vt.shape[0], t), imap(lambda b, s, it_, jt_: (0, b * nq + jt_[s])))]
    if fin == "diff":
        in_specs += [pl.BlockSpec((4, HEAD_DIM), imap(lambda b, s, it_, jt_: (0, 0))),
                     pl.BlockSpec((1, LANES), imap(lambda b, s, it_, jt_: (0, 0)))]
    if fin == "nsa":
        in_specs += [pl.BlockSpec((t, LANES), imap(lambda b, s, it_, jt_: (b * nq + it_[s], C_GATE_COL))),
                     qrow(D_MODEL)]
    return pl.pallas_call(
        functools.partial(_flash_t_kernel, t=t, cw=cw, kdiv=kdiv, vdiv=vdiv, dv=dv, fin=fin, modes=modes,
                          lam_init=lam_init, gate_idx=gate_idx, has_live=live is not None),
        grid_spec=pltpu.PrefetchScalarGridSpec(
            num_scalar_prefetch=len(tabs),
            grid=(batch, len(tabs[0])),
            in_specs=in_specs,
            out_specs=qrow(D_MODEL),
            scratch_shapes=[pltpu.VMEM((N_HEADS, t), F32), pltpu.VMEM((N_HEADS, t), F32),
                            pltpu.VMEM((N_HEADS * dv, t), F32)]),
        out_shape=jax.ShapeDtypeStruct((m, D_MODEL), F32),
        compiler_params=_params("parallel", "arbitrary"),
    )(*tabs, qa, ka, vt, *extras)


def _logf_kernel(f_ref, b_ref, o_ref):
    x = f_ref[:, 0:N_HEADS] + b_ref[...]
    o_ref[...] = jnp.minimum(x, 0.0) - jnp.log1p(jnp.exp(-jnp.abs(x)))


def _logf(p, f_col, b_f, tm):
    m = p.shape[0]
    return pl.pallas_call(
        _logf_kernel,
        grid=(m // tm,),
        in_specs=[pl.BlockSpec((tm, LANES), lambda i: (i, f_col)),
                  pl.BlockSpec((1, N_HEADS), lambda i: (0, 0))],
        out_specs=pl.BlockSpec((tm, N_HEADS), lambda i: (i, 0)),
        out_shape=jax.ShapeDtypeStruct((m, N_HEADS), F32),
        compiler_params=_params("parallel"),
    )(p, b_f.reshape(1, -1))


def _cumsum_kernel(x_ref, o_ref, carry_scr, *, t):
    @pl.when(pl.program_id(1) == 0)
    def _():
        carry_scr[...] = jnp.zeros(carry_scr.shape, F32)

    x = x_ref[0]
    tri = (_iota((t, t), 0) <= _iota((t, t), 1)).astype(F32)
    c = jnp.dot(x, tri, precision=HIGHEST, preferred_element_type=F32) + carry_scr[...]
    o_ref[0] = c
    carry_scr[...] = c[:, t - 1:t]


def _cumsum_lanes(x, t):
    b, h, s = x.shape
    return pl.pallas_call(
        functools.partial(_cumsum_kernel, t=t),
        grid=(b, s // t),
        in_specs=[pl.BlockSpec((1, h, t), lambda bi, n: (bi, 0, n))],
        out_specs=pl.BlockSpec((1, h, t), lambda bi, n: (bi, 0, n)),
        out_shape=jax.ShapeDtypeStruct(x.shape, F32),
        scratch_shapes=[pltpu.VMEM((h, 1), F32)],
        compiler_params=_params("parallel", "arbitrary"),
    )(x)


C_Z_COL = 1
C_KV_COL = 16
C_GATE_COL = 22
C_WIDTH = 3072


def _compress_kernel(kc_ref, vc_ref, wk_ref, wv_ref, ck_ref, cv_ref, *, nch):
    stride = 2 * C_CMP_BLOCK
    for src, w_ref, dst in ((kc_ref, wk_ref, ck_ref), (vc_ref, wv_ref, cv_ref)):
        for parity in range(2):
            acc = jnp.zeros((nch, LANES), F32)
            for tt in range(C_CMP_BLOCK):
                rows = src[pl.ds(parity * C_CMP_BLOCK + tt, nch, stride=stride), :]
                acc = acc + _dot(rows.astype(BF16), w_ref[tt])
            dst[0, parity * nch:(parity + 1) * nch, :] = acc


def _compress_prompt(p, batch, w2k, w2v):
    s = p.shape[0] // batch
    nch = s // (2 * C_CMP_BLOCK)
    col = lambda c: pl.BlockSpec((s, LANES), lambda b: (b, c))
    wspec = pl.BlockSpec((C_CMP_BLOCK, LANES, LANES), lambda b: (0, 0, 0))
    ospec = pl.BlockSpec((1, 2 * nch, LANES), lambda b: (b, 0, 0))
    oshape = jax.ShapeDtypeStruct((batch, 2 * nch, LANES), F32)
    return pl.pallas_call(
        functools.partial(_compress_kernel, nch=nch),
        grid=(batch,),
        in_specs=[col(C_KV_COL), col(C_KV_COL + 1), wspec, wspec],
        out_specs=[ospec, ospec],
        out_shape=[oshape, oshape],
        compiler_params=_params("parallel"),
    )(p, p, w2k, w2v)


def _compress_pool_kernel(x_ref, w_ref, o_ref):
    acc = jnp.zeros(o_ref.shape, F32)
    for d in range(HEAD_DIM):
        acc = acc + _dot(x_ref[:, d, :].astype(BF16), w_ref[d])
    o_ref[...] = acc


def _compress_pool(cache, w_cmp):
    n_pool = cache.shape[0]
    tokens = PAGE // C_CMP_BLOCK
    x = cache.transpose(0, 2, 3, 1).reshape(n_pool * C_GROUPS, HEAD_DIM, PAGE)
    r = jnp.arange(PAGE)
    in_token = (r[:, None] // C_CMP_BLOCK == jnp.arange(tokens)[None, :]).astype(w_cmp.dtype)
    w_rows = w_cmp.reshape(C_CMP_BLOCK, HEAD_DIM, HEAD_DIM)[r % C_CMP_BLOCK]
    wx = (w_rows[:, :, None, :] * in_token[:, None, :, None]).transpose(1, 0, 2, 3)
    wx = wx.reshape(HEAD_DIM, PAGE, tokens * HEAD_DIM).astype(BF16)
    tm = min(256, x.shape[0])
    assert x.shape[0] % tm == 0
    out = pl.pallas_call(
        _compress_pool_kernel,
        grid=(x.shape[0] // tm,),
        in_specs=[pl.BlockSpec((tm, HEAD_DIM, PAGE), lambda i: (i, 0, 0)),
                  pl.BlockSpec(wx.shape, lambda i: (0, 0, 0))],
        out_specs=pl.BlockSpec((tm, tokens * HEAD_DIM), lambda i: (i, 0)),
        out_shape=jax.ShapeDtypeStruct((x.shape[0], tokens * HEAD_DIM), F32),
        compiler_params=_params("parallel"),
    )(x, wx)
    return out.reshape(n_pool, C_GROUPS, tokens, HEAD_DIM).transpose(0, 2, 1, 3).reshape(n_pool, tokens, LANES)


def _group_halves(x, g):
    lo = _iota(x.shape, 1) < HEAD_DIM
    base = jnp.where(lo if g == 0 else ~lo, x, 0.0)
    other = pltpu.roll(base, HEAD_DIM, 1)
    pair = (base, other) if g == 0 else (other, base)
    return pair[0].astype(BF16), pair[1].astype(BF16)


def _nsa_cmp_kernel(sl_ref, q_ref, g_ref, ck_ref, cv_ref, o_ref, sel_ref, *, tq, nch):
    q0 = pl.program_id(1) * tq
    nl = 2 * nch
    lane = _iota((tq, nl), 1)
    qpos = q0 + _iota((tq, nl), 0)
    tok = jnp.where(lane < nch, 2 * lane, 2 * (lane - nch) + 1)
    endp = (tok + 1) * C_CMP_BLOCK - 1
    okc = endp <= qpos
    relc = (endp[0:1, :] - q0).astype(F32)
    lane_s = _iota((tq, nch), 1)
    qblk = (q0 + _iota((tq, nch), 0)) // C_SEL_BLOCK
    lane_sf = lane_s.astype(F32)
    for g in range(C_GROUPS):
        k_lo, k_hi = _group_halves(ck_ref[0], g)
        v_lo, v_hi = _group_halves(cv_ref[0], g)
        imp = jnp.zeros((tq, nl), F32)
        for hp in range(g * C_RATIO // 2, (g + 1) * C_RATIO // 2):
            cs = slice(hp * LANES, (hp + 1) * LANES)
            qp = (q_ref[:, cs] * SCALE).astype(BF16)
            o_pair = None
            for e in range(2):
                h = 2 * hp + e
                s = _dot_nt(qp, k_lo if e == 0 else k_hi) + sl_ref[h] * relc
                s = jnp.where(okc, s, NEG)
                pc = jnp.where(okc, jnp.exp(s - jnp.max(s, axis=1, keepdims=True)), 0.0)
                pc = pc / jnp.maximum(jnp.sum(pc, axis=1, keepdims=True), TINY)
                imp = imp + pc
                gate = jax.nn.sigmoid(g_ref[:, 3 * h:3 * h + 1])
                o = _dot(pc.astype(BF16), v_lo if e == 0 else v_hi) * gate
                o_pair = o if o_pair is None else o_pair + o
            o_ref[:, cs] = o_pair
        imp_sel = imp[:, 0:nch] + imp[:, nch:nl]
        sel = _topk_mask(imp_sel, lane_s < qblk, lane_sf, C_TOPK)
        sel = jnp.where(lane_s == qblk, 1.0, sel)
        if nch < LANES:
            sel = jnp.concatenate([sel, jnp.zeros((tq, LANES - nch), F32)], axis=1)
        sel_ref[:, g * LANES:(g + 1) * LANES] = sel


def _nsa_cmp(p, batch, slopes, ck, cv, tq):
    nq = p.shape[0] // batch // tq
    nch = ck.shape[1] // 2
    assert nch <= LANES
    row = lambda width, col: pl.BlockSpec((tq, width), lambda b, i: (b * nq + i, col))
    cspec = pl.BlockSpec((1, 2 * nch, LANES), lambda b, i: (b, 0, 0))
    return pl.pallas_call(
        functools.partial(_nsa_cmp_kernel, tq=tq, nch=nch),
        grid=(batch, nq),
        in_specs=[_smem(), row(D_MODEL, 0), row(LANES, C_GATE_COL), cspec, cspec],
        out_specs=[row(D_MODEL, 0), row(C_GROUPS * LANES, 0)],
        out_shape=[jax.ShapeDtypeStruct((p.shape[0], D_MODEL), F32),
                   jax.ShapeDtypeStruct((p.shape[0], C_GROUPS * LANES), F32)],
        compiler_params=_params("parallel", "parallel"),
    )(slopes, p, p, ck, cv)


def _decode_kernel(pt_ref, qb_ref, q_ref, kn_ref, vn_ref, *rest, n_pages, mode, lam_init):
    del pt_ref
    gp = PAGES_PER_STEP
    if mode == "moba":
        slc_ref, rest = rest[0], rest[1:]
    elif mode == "diff":
        slc_ref, lam_ref, g_ref, rest = rest[0], rest[1], rest[2], rest[3:]
    else:
        lfn_ref, lft_refs, rest = rest[0], rest[1:1 + gp], rest[1 + gp:]
    kt_refs, v_refs, rest = rest[:gp], rest[gp:2 * gp], rest[2 * gp:]
    if mode == "moba":
        o_ref, st_scr, p_scr, pn_scr, acc_scr, sb_scr = rest
    elif mode == "diff":
        o_ref, st_scr, p_scr, pn_scr, acc_scr = rest
    else:
        o_ref, st_scr, p_scr, pn_scr, acc_scr, bias_scr, carry_scr = rest
    s = pl.program_id(1)
    n_steps = n_pages // gp
    past = n_pages * PAGE
    per_blk = A_BLOCK // PAGE
    lane_b = _iota((N_HEADS, LANES), 1)

    @pl.when(s == 0)
    def _():
        acc_scr[...] = jnp.zeros(acc_scr.shape, F32)
        if mode == "moba":
            sb_scr[...] = jnp.zeros(sb_scr.shape, F32)
        if mode == "fox":
            carry_scr[...] = jnp.zeros(carry_scr.shape, F32)

    @pl.when(s < n_steps)
    def _():
        qb = qb_ref[0]
        for g in range(gp):
            page = s * gp + g
            off = pl.multiple_of(page * PAGE, PAGE)
            st_scr[:, pl.ds(off, PAGE)] = jnp.sum(kt_refs[g][0] * qb, axis=1)
            if mode == "moba" and g % per_blk == per_blk - 1:
                blk_sum = kt_refs[g][0]
                for back in range(1, per_blk):
                    blk_sum = blk_sum + kt_refs[g - back][0]
                bm = jnp.sum(blk_sum, axis=2, keepdims=True) / A_BLOCK
                col = jnp.sum(bm * (qb[:, :, 0:1] / SCALE), axis=1)
                sb_scr[...] = jnp.where(lane_b == page // per_blk, col, sb_scr[...])
            if mode == "fox":
                tri = (_iota((PAGE, PAGE), 0) <= _iota((PAGE, PAGE), 1)).astype(F32)
                c = jnp.dot(lft_refs[g][0], tri, precision=HIGHEST, preferred_element_type=F32) + carry_scr[...]
                bias_scr[:, pl.ds(off, PAGE)] = -c
                carry_scr[...] = c[:, PAGE - 1:PAGE]

    @pl.when(s == n_steps - 1)
    def _():
        q = q_ref[0]
        qk_new = jnp.sum(q * kn_ref[0], axis=1, keepdims=True)
        s_new = qk_new * SCALE
        if mode == "fox":
            s_all = st_scr[...] + bias_scr[...]
            s_new = s_new - (carry_scr[...] + lfn_ref[0])
        else:
            kpos = _iota((1, past), 1)
            s_all = st_scr[...] + slc_ref[...] * (kpos - past).astype(F32)
        if mode == "moba":
            sb = jnp.where(lane_b == past // A_BLOCK, qk_new / A_BLOCK, sb_scr[...])
            sel = _topk_mask(sb, lane_b < past // A_BLOCK, lane_b.astype(F32), A_TOPK)
            st_scr[...] = s_all
            for n in range(past // A_BLOCK):
                cols = slice(n * A_BLOCK, (n + 1) * A_BLOCK)
                st_scr[:, cols] = st_scr[:, cols] + jnp.where(sel[:, n:n + 1] > 0.5, 0.0, NEG)
            s_all = st_scr[...]
        m = jnp.maximum(jnp.max(s_all, axis=1, keepdims=True), s_new)
        p = jnp.exp(s_all - m)
        pn = jnp.exp(s_new - m)
        l = jnp.sum(p, axis=1, keepdims=True) + pn
        p_scr[...] = p / l
        pn_scr[...] = pn / l

    @pl.when(s >= n_steps)
    def _():
        acc = acc_scr[...]
        for g in range(gp):
            off = pl.multiple_of(((s - n_steps) * gp + g) * PAGE, PAGE)
            pg = p_scr[:, pl.ds(off, PAGE)]
            if mode == "diff":
                row = _iota((N_HEADS, LANES), 0)
                pg = pg.astype(BF16)
                for h in range(B_HEADS):
                    vh = v_refs[g][0, pl.ds(h, PAGE, stride=B_HEADS), :]
                    acc = acc + jnp.where(row // 2 == h, _dot(pg, vh.astype(BF16)), 0.0)
            else:
                acc = acc + v_refs[g][0] * pg[:, None, :]
        acc_scr[...] = acc

    @pl.when(s == 2 * n_steps - 1)
    def _():
        acc = acc_scr[...]
        if mode == "diff":
            row = _iota((N_HEADS, LANES), 0)
            acc = acc + pn_scr[...] * vn_ref[0]
            signed = jnp.where(row % 2 == 0, acc, -_lambda(lam_ref, lam_init) * acc)
            o = signed + pltpu.roll(signed, N_HEADS - 1, 0)
            o = o * lax.rsqrt(jnp.mean(o * o, axis=1, keepdims=True) + LN_EPS)
            o_ref[0] = o * g_ref[...] * (1.0 - lam_init)
        else:
            acc = acc + vn_ref[0] * jnp.broadcast_to(pn_scr[...], (N_HEADS, LANES))[:, None, :]
            o_ref[0] = jnp.sum(acc, axis=2, keepdims=True)


def _decode(q3, kn3, vn3, page_table, kt_pool, v_pool, mode, extras, lam_init=0.0):
    b = q3.shape[0]
    n_pages = page_table.shape[1]
    gp = PAGES_PER_STEP
    assert n_pages % gp == 0 and gp % (A_BLOCK // PAGE) == 0
    n_steps = n_pages // gp
    past = n_pages * PAGE
    pt = page_table.reshape(-1)
    qb = jnp.broadcast_to((q3 * SCALE)[..., None], q3.shape + (LANES,))
    per_b = lambda shape: pl.BlockSpec((1,) + shape, lambda bi, s, pt_: (bi,) + tuple(0 for _ in shape))
    full = lambda shape: pl.BlockSpec(shape, lambda bi, s, pt_: tuple(0 for _ in shape))
    kpage = lambda g, nd: (lambda bi, s, pt_: (pt_[bi * n_pages + jnp.minimum(s, n_steps - 1) * gp + g],) + (0,) * nd)
    vpage = lambda g: (lambda bi, s, pt_: (pt_[bi * n_pages + jnp.maximum(s - n_steps, 0) * gp + g],)
                       + (0,) * (v_pool.ndim - 1))
    tile = (N_HEADS, HEAD_DIM, LANES)
    kspecs = [pl.BlockSpec((1,) + tile, kpage(g, 3)) for g in range(gp)]
    vspecs = [pl.BlockSpec((1,) + v_pool.shape[1:], vpage(g)) for g in range(gp)]
    scratch = [pltpu.VMEM((N_HEADS, past), F32), pltpu.VMEM((N_HEADS, past), F32), pltpu.VMEM((N_HEADS, 1), F32)]
    args = list(extras)
    if mode == "diff":
        vn = jnp.repeat(vn3, 2, axis=1)
        out_tile = (N_HEADS, LANES)
        in_specs = [full((N_HEADS, 1)), full((4, HEAD_DIM)), full((1, LANES))]
        scratch += [pltpu.VMEM(out_tile, F32)]
    else:
        vn = vn3[..., None] * (jnp.arange(LANES) == 0)
        out_tile = (N_HEADS, HEAD_DIM, 1)
        scratch += [pltpu.VMEM(tile, F32)]
        if mode == "moba":
            in_specs = [full((N_HEADS, 1))]
            scratch += [pltpu.VMEM((N_HEADS, LANES), F32)]
        else:
            lft, lfn = extras
            in_specs = [per_b((N_HEADS, 1))] + [pl.BlockSpec((1, N_HEADS, PAGE), kpage(g, 2)) for g in range(gp)]
            args = [lfn] + [lft] * gp
            scratch += [pltpu.VMEM((N_HEADS, past), F32), pltpu.VMEM((N_HEADS, 1), F32)]
    out = pl.pallas_call(
        functools.partial(_decode_kernel, n_pages=n_pages, mode=mode, lam_init=lam_init),
        grid_spec=pltpu.PrefetchScalarGridSpec(
            num_scalar_prefetch=1,
            grid=(b, 2 * n_steps),
            in_specs=[per_b(tile), per_b(q3.shape[1:]), per_b(kn3.shape[1:]), per_b(vn.shape[1:])]
                     + in_specs + kspecs + vspecs,
            out_specs=per_b(out_tile),
            scratch_shapes=scratch),
        out_shape=jax.ShapeDtypeStruct((b,) + out_tile, F32),
        compiler_params=_params("parallel", "arbitrary"),
    )(pt, qb, q3, kn3, vn, *args, *([kt_pool] * gp), *([v_pool] * gp))
    if mode == "diff":
        out = out[:, 0::2, :]
    return out.reshape(b, D_MODEL)


C_HALF = 64


def _softmax_new(s_all, s_new, v_all, v_new, v_transposed):
    m = jnp.maximum(jnp.max(s_all, axis=1, keepdims=True), s_new)
    p = jnp.exp(s_all - m)
    pn = jnp.exp(s_new - m)
    l = jnp.sum(p, axis=1, keepdims=True) + pn
    pv = _dot_nt(p.astype(BF16), v_all.astype(BF16)) if v_transposed else _dot(p.astype(BF16), v_all.astype(BF16))
    return (pv + pn * v_new) / l


def _nsa_decode_kernel(pt_ref, qbd_ref, g_ref, slc_ref, kcn_ref, vcn_ref, ksn_ref, vsn_ref, kwn_ref, vwn_ref,
                       kw_ref, vw_ref, wk0_ref, wv0_ref, *rest, n_pages):
    del pt_ref
    gp = NSA_PAGES_PER_STEP
    ckp_refs, cvp_refs, kst_refs, vst_refs = (rest[i * gp:(i + 1) * gp] for i in range(4))
    o_ref, ck_scr, cv_scr, kst_scr, vst_scr = rest[4 * gp:]
    s = pl.program_id(1)
    past = n_pages * PAGE
    per_page = PAGE // C_CMP_BLOCK // 2

    @pl.when(s == 0)
    def _():
        ck_scr[...] = jnp.zeros(ck_scr.shape, F32)
        cv_scr[...] = jnp.zeros(cv_scr.shape, F32)

    for g in range(gp):
        page = s * gp + g
        off = pl.multiple_of(page * PAGE, PAGE)
        kst_scr[:, pl.ds(off, PAGE)] = kst_refs[g][0]
        vst_scr[:, pl.ds(off, PAGE)] = vst_refs[g][0]
        for u in range(2 * per_page):
            dst = (u % 2) * C_HALF + per_page * page + u // 2
            ck_scr[pl.ds(dst, 1), :] = ckp_refs[g][0, u:u + 1, :]
            cv_scr[pl.ds(dst, 1), :] = cvp_refs[g][0, u:u + 1, :]

    @pl.when(s == n_pages // gp - 1)
    def _():
        qf = qbd_ref[0] * SCALE
        qb = qf.astype(BF16)
        slc = slc_ref[...]
        new_tok = per_page * n_pages
        ck_scr[new_tok:new_tok + 1, :] = _dot(jnp.broadcast_to(kcn_ref[0], (8, LANES)).astype(BF16), wk0_ref[...])[0:1]
        cv_scr[new_tok:new_tok + 1, :] = _dot(jnp.broadcast_to(vcn_ref[0], (8, LANES)).astype(BF16), wv0_ref[...])[0:1]
        lane = _iota((N_HEADS, LANES), 1)
        row = _iota((N_HEADS, LANES), 0)
        tok = jnp.where(lane < C_HALF, 2 * lane, 2 * (lane - C_HALF) + 1)
        endp = (tok + 1) * C_CMP_BLOCK - 1
        okc = endp <= past
        sc = _dot_nt(qb, ck_scr[...].astype(BF16)) + slc * (endp - past).astype(F32)
        sc = jnp.where(okc, sc, NEG)
        pc = jnp.where(okc, jnp.exp(sc - jnp.max(sc, axis=1, keepdims=True)), 0.0)
        pc = pc / jnp.maximum(jnp.sum(pc, axis=1, keepdims=True), TINY)
        o_cmp = _dot(pc.astype(BF16), cv_scr[...].astype(BF16))
        imp = jnp.where(row < C_RATIO, jnp.sum(pc[0:C_RATIO], axis=0, keepdims=True),
                        jnp.sum(pc[C_RATIO:N_HEADS], axis=0, keepdims=True))
        imp = imp + pltpu.roll(imp, C_HALF, 1)
        sel = _topk_mask(imp, lane < past // C_SEL_BLOCK, lane.astype(F32), C_TOPK)
        expand = (_iota((LANES, past), 0) == _iota((LANES, past), 1) // C_SEL_BLOCK).astype(BF16)
        picked = _dot(sel.astype(BF16), expand)
        kpos = _iota((1, past), 1)
        ss = _dot(qb, kst_scr[...].astype(BF16)) + slc * (kpos - past).astype(F32)
        ss = jnp.where(picked > 0.5, ss, NEG)
        ss_new = jnp.sum(qf * ksn_ref[0], axis=1, keepdims=True)
        o_sel = _softmax_new(ss, ss_new, vst_scr[...], vsn_ref[0], True)
        wb = kw_ref.shape[1]
        wpos = past - wb + _iota((1, wb), 1)
        okw = (past - wpos <= C_WINDOW) & (wpos >= 0)
        sw = _dot_nt(qb, kw_ref[0].astype(BF16)) + slc * (wpos - past).astype(F32)
        sw = jnp.where(okw, sw, NEG)
        sw_new = jnp.sum(qf * kwn_ref[0], axis=1, keepdims=True)
        o_win = _softmax_new(sw, sw_new, vw_ref[0], vwn_ref[0], False)
        gate = jax.nn.sigmoid(g_ref[0])
        o_ref[0] = gate[:, 0:1] * o_cmp + gate[:, 1:2] * o_sel + gate[:, 2:3] * o_win


def _nsa_decode(ps3, qbd, gates, slc, page_table, ck_pool, cv_pool, kst_pool, vst_pool, kw_buf, vw_buf, wk0, wv0):
    b = ps3.shape[0]
    n_pages = page_table.shape[1]
    gp = NSA_PAGES_PER_STEP
    past = n_pages * PAGE
    assert 2 * n_pages + 1 <= C_HALF and n_pages % gp == 0
    pt = page_table.reshape(-1)
    per_b = lambda shape: pl.BlockSpec((1,) + shape, lambda bi, s, pt_: (bi,) + tuple(0 for _ in shape))
    full = lambda shape: pl.BlockSpec(shape, lambda bi, s, pt_: tuple(0 for _ in shape))
    newrow = lambda col: pl.BlockSpec((1, 1, LANES), lambda bi, s, pt_: (bi, 0, col))
    paged = lambda rows: [pl.BlockSpec((1, rows, LANES), (lambda g: lambda bi, s, pt_: (pt_[bi * n_pages + s * gp + g], 0, 0))(g))
                          for g in range(gp)]
    tokens = PAGE // C_CMP_BLOCK
    return pl.pallas_call(
        functools.partial(_nsa_decode_kernel, n_pages=n_pages),
        grid_spec=pltpu.PrefetchScalarGridSpec(
            num_scalar_prefetch=1,
            grid=(b, n_pages // gp),
            in_specs=[per_b((N_HEADS, LANES)), per_b((N_HEADS, 3)), full((N_HEADS, 1))]
                     + [newrow(C_KV_COL + c) for c in range(6)]
                     + [per_b(kw_buf.shape[1:]), per_b(vw_buf.shape[1:]), full((LANES, LANES)), full((LANES, LANES))]
                     + paged(tokens) + paged(tokens) + paged(PAGE) + paged(PAGE),
            out_specs=per_b((N_HEADS, LANES)),
            scratch_shapes=[pltpu.VMEM((2 * C_HALF, LANES), F32), pltpu.VMEM((2 * C_HALF, LANES), F32),
                            pltpu.VMEM((LANES, past), F32), pltpu.VMEM((LANES, past), F32)]),
        out_shape=jax.ShapeDtypeStruct((b, N_HEADS, LANES), F32),
        compiler_params=_params("parallel", "arbitrary"),
    )(pt, qbd, gates, slc, ps3, ps3, ps3, ps3, ps3, ps3, kw_buf, vw_buf, wk0, wv0,
      *([ck_pool] * gp), *([cv_pool] * gp), *([kst_pool] * gp), *([vst_pool] * gp))


def _pad_cols(w, width):
    return jnp.pad(w, ((0, 0), (0, width - w.shape[1])))


def _block_diag_cmp(w):
    w3 = w.reshape(C_CMP_BLOCK, HEAD_DIM, HEAD_DIM)
    z = jnp.zeros_like(w3)
    return jnp.concatenate([jnp.concatenate([w3, z], 2), jnp.concatenate([z, w3], 2)], 1).astype(BF16)


def kernel(x_prompt, x_sample, cache_a_k, cache_a_v, cache_b_k, cache_b_v, cache_c_kc, cache_c_vc, cache_c_ks, cache_c_vs, state_c_kw, state_c_vw, cache_d_k, cache_d_v, cache_d_logf, page_table, w_in_0, w_out_0, ln_g_0, ln_b_0, w_in_1, lam_q1_1, lam_k1_1, lam_q2_1, lam_k2_1, subln_g_1, w_out_1, ln_g_1, ln_b_1, w_in_2, w_cmp_k_2, w_cmp_v_2, w_out_2, ln_g_2, ln_b_2, w_in_3, b_f_3, w_out_3, ln_g_3, ln_b_3):
    bp, sp, _ = x_prompt.shape
    bs = x_sample.shape[0]
    assert x_sample.shape[1] == 1 and sp % C_WINDOW == 0
    n_pool = cache_a_k.shape[0]
    mp = bp * sp
    tm = 512
    xp = x_prompt.reshape(mp, D_MODEL)
    xs = x_sample.reshape(bs, D_MODEL)
    heads = lambda a, b, l, h: a.reshape(b, l, h, -1)
    split_slopes = lambda sl: jnp.stack(_split3(sl * LOG2E), axis=1)
    vcols = slice(2 * D_MODEL, 3 * D_MODEL)
    hd3 = lambda p, c: p[:, c * D_MODEL:(c + 1) * D_MODEL].reshape(-1, N_HEADS, HEAD_DIM)
    rows_last = lambda c: c.transpose(0, 2, 3, 1)

    w = w_in_0.astype(BF16)
    slopes16 = _alibi_slopes(N_HEADS)
    sl3_16 = split_slopes(slopes16)
    pp = _matmul(xp, w, tm)
    ps = _matmul(xs, w, bs)
    a_k_p, a_v_p = heads(pp[:, 1024:2048], bp, sp, 16), heads(pp[:, 2048:3072], bp, sp, 16)
    a_k_s, a_v_s = heads(ps[:, 1024:2048], bs, 1, 16), heads(ps[:, 2048:3072], bs, 1, 16)
    qa, ka = _prep_moba(pp, bp, sl3_16, _block_mean(pp, bp, 1))
    o_p = _flash_t(qa, ka, _matmul_t(xp, w[:, vcols].T, tm), bp, fin="plain")
    o_s = _decode(hd3(ps, 0), hd3(ps, 1), hd3(ps, 2), page_table, rows_last(cache_a_k), rows_last(cache_a_v), "moba",
                  [slopes16.reshape(-1, 1)])
    w_o = w_out_0.astype(BF16)
    xp = _out_ln(o_p, pp, 3, xp, w_o, ln_g_0, ln_b_0, tm)
    xs = _out_ln(o_s, ps, 3, xs, w_o, ln_g_0, ln_b_0, bs)

    lam_init = 0.8 - 0.6 * math.exp(-0.3 * 1)
    w = w_in_1.astype(BF16)
    slopes8 = jnp.repeat(_alibi_slopes(B_HEADS), 2)
    lamv = jnp.stack([lam_q1_1, lam_k1_1, lam_q2_1, lam_k2_1])
    pp = _matmul(xp, w, tm)
    ps = _matmul(xs, w, bs)
    b_k_p, b_v_p = heads(pp[:, 1024:2048], bp, sp, 16), heads(pp[:, 2048:3072], bp, sp, 8)
    b_k_s, b_v_s = heads(ps[:, 1024:2048], bs, 1, 16), heads(ps[:, 2048:3072], bs, 1, 8)
    qa, ka = _prep(pp, bp, "diff", 1, [split_slopes(slopes8)], tm)
    o_p = _flash_t(qa, ka, _matmul_t(xp, w[:, vcols].T, tm), bp, fin="diff",
                   extras=(lamv, subln_g_1.reshape(1, -1)), lam_init=lam_init)
    o_s = _decode(hd3(ps, 0), hd3(ps, 1), ps[:, vcols].reshape(bs, B_HEADS, LANES), page_table, rows_last(cache_b_k),
                  cache_b_v.reshape(n_pool, PAGE * B_HEADS, LANES), "diff",
                  [slopes8.reshape(-1, 1), lamv, subln_g_1.reshape(1, -1)], lam_init)
    w_o = w_out_1.astype(BF16)
    xp = _out_ln(o_p, pp, 3, xp, w_o, ln_g_1, ln_b_1, tm)
    xs = _out_ln(o_s, ps, 3, xs, w_o, ln_g_1, ln_b_1, bs)

    kv0, z0, g0 = D_MODEL, D_MODEL + 6 * LANES, 2 * D_MODEL + 6 * LANES
    w = _pad_cols(jnp.concatenate([w_in_2[:, :kv0], w_in_2[:, z0:g0], w_in_2[:, kv0:z0], w_in_2[:, g0:]], 1),
                  C_WIDTH).astype(BF16)
    w2k, w2v = _block_diag_cmp(w_cmp_k_2), _block_diag_cmp(w_cmp_v_2)
    pp = _matmul(xp, w, tm)
    ps = _matmul(xs, w, bs)
    kvcol = lambda p, c: p[:, (C_KV_COL + c) * LANES:(C_KV_COL + c + 1) * LANES]
    c_p = [heads(kvcol(pp, c), bp, sp, 2) for c in range(6)]
    c_s = [heads(kvcol(ps, c), bs, 1, 2) for c in range(6)]
    keep = min(C_WINDOW, sp)
    ck, cv = _compress_prompt(pp, bp, w2k, w2v)
    o_p, sel = _nsa_cmp(pp, bp, slopes16, ck, cv, 256)
    qa, ka = _prep(pp, bp, "sel", C_KV_COL + 2, [sl3_16, sel], tm)
    blocks = tm // C_SEL_BLOCK
    live = (sel.reshape(bp, sp // tm, tm, C_GROUPS, LANES // blocks, blocks).max((2, 3, 5)) > 0.5)
    o_p = _flash_t(qa, ka, _matmul_t(xp, kvcol(w, 3).T, tm), bp, fin="nsa", extras=(pp, o_p), gate_idx=1, live=live)
    qa, ka = _prep(pp, bp, "win", C_KV_COL + 4, [sl3_16], tm)
    o_p = _flash_t(qa, ka, _matmul_t(xp, kvcol(w, 5).T, tm), bp, fin="nsa", band=True, extras=(pp, o_p),
                   gate_idx=2)
    ck_pool = _compress_pool(cache_c_kc, w_cmp_k_2)
    cv_pool = _compress_pool(cache_c_vc, w_cmp_v_2)
    q_s = ps[:, :D_MODEL].reshape(bs, N_HEADS, 1, HEAD_DIM)
    in_group = (jnp.arange(N_HEADS) // C_RATIO)[:, None] == jnp.arange(C_GROUPS)[None, :]
    qbd = jnp.where(in_group[None, :, :, None], q_s, 0.0).reshape(bs, N_HEADS, LANES)
    gates_s = ps[:, C_GATE_COL * LANES:C_GATE_COL * LANES + 3 * N_HEADS].reshape(bs, N_HEADS, 3)
    wb = state_c_kw.shape[1]
    o16 = _nsa_decode(ps.reshape(bs, 1, -1), qbd, gates_s, slopes16.reshape(-1, 1), page_table, ck_pool, cv_pool,
                      rows_last(cache_c_ks).reshape(n_pool, LANES, PAGE), rows_last(cache_c_vs).reshape(n_pool, LANES, PAGE),
                      state_c_kw.reshape(bs, wb, LANES),
                      state_c_vw.reshape(bs, wb, LANES), w2k[0], w2v[0])
    o16 = o16.reshape(bs, N_HEADS, C_GROUPS, HEAD_DIM)
    o_s = jnp.where(in_group[None, :, :, None], o16, 0.0).sum(2).reshape(bs, D_MODEL)
    c_kw_s = jnp.concatenate([state_c_kw, c_s[4]], 1)[:, -min(C_WINDOW, wb + 1):]
    c_vw_s = jnp.concatenate([state_c_vw, c_s[5]], 1)[:, -min(C_WINDOW, wb + 1):]
    w_o = w_out_2.astype(BF16)
    xp = _out_ln(o_p, pp, C_Z_COL, xp, w_o, ln_g_2, ln_b_2, tm)
    xs = _out_ln(o_s, ps, C_Z_COL, xs, w_o, ln_g_2, ln_b_2, bs)

    f_col = 4 * D_MODEL // LANES
    w = _pad_cols(w_in_3, 4 * D_MODEL + LANES).astype(BF16)
    pp = _matmul(xp, w, tm)
    ps = _matmul(xs, w, bs)
    d_k_p, d_v_p = heads(pp[:, 1024:2048], bp, sp, 16), heads(pp[:, 2048:3072], bp, sp, 16)
    d_k_s, d_v_s = heads(ps[:, 1024:2048], bs, 1, 16), heads(ps[:, 2048:3072], bs, 1, 16)
    lf_p = _logf(pp, f_col, b_f_3, tm).reshape(bp, sp, N_HEADS)
    lf_s = _logf(ps, f_col, b_f_3, bs).reshape(bs, 1, N_HEADS)
    c_p3 = _cumsum_lanes(lf_p.transpose(0, 2, 1), tm).transpose(0, 2, 1).reshape(mp, N_HEADS)
    qa, ka = _prep(pp, bp, "fox", 1, [c_p3], tm)
    o_p = _flash_t(qa, ka, _matmul_t(xp, w[:, vcols].T, tm), bp, fin="plain")
    o_s = _decode(hd3(ps, 0), hd3(ps, 1), hd3(ps, 2), page_table, rows_last(cache_d_k), rows_last(cache_d_v), "fox",
                  [cache_d_logf.transpose(0, 2, 1), lf_s.reshape(bs, N_HEADS, 1)])
    w_o = w_out_3.astype(BF16)
    xp = _out_ln(o_p, pp, 3, xp, w_o, ln_g_3, ln_b_3, tm)
    xs = _out_ln(o_s, ps, 3, xs, w_o, ln_g_3, ln_b_3, bs)

    return (xp.reshape(bp, sp, D_MODEL), xs.reshape(bs, 1, D_MODEL),
            a_k_p, a_v_p, a_k_s, a_v_s, b_k_p, b_v_p, b_k_s, b_v_s,
            c_p[0], c_p[1], c_p[2], c_p[3], c_p[4][:, -keep:], c_p[5][:, -keep:],
            c_s[0], c_s[1], c_s[2], c_s[3], c_kw_s, c_vw_s,
            d_k_p, d_v_p, lf_p, d_k_s, d_v_s, lf_s)
```

```python
import functools
import math

import numpy as np
import jax
import jax.numpy as jnp
from jax import lax
from jax.experimental import pallas as pl
from jax.experimental.pallas import tpu as pltpu

F32 = jnp.float32
BF16 = jnp.bfloat16
HIGHEST = lax.Precision.HIGHEST

D_MODEL = 1024
HEAD_DIM = 64
N_HEADS = 16
LANES = 128
SCALE = HEAD_DIM ** -0.5
PAGE = 128
DEPTH = 4
ALPHA = (2 * DEPTH) ** 0.25
LN_EPS = 1e-5
NEG = -1e30
TINY = 1e-30
A_BLOCK = 256
A_TOPK = 3
B_HEADS = 8
C_GROUPS = 2
C_RATIO = 8
C_CMP_BLOCK = 32
C_SEL_BLOCK = 64
C_TOPK = 4
C_WINDOW = 512
PAGES_PER_STEP = 8
NSA_PAGES_PER_STEP = 4
VMEM_LIMIT = 56 * 1024 * 1024


def _dot_nt(a, b):
    return lax.dot_general(a, b, (((1,), (1,)), ((), ())), preferred_element_type=F32)


def _dot(a, b):
    return jnp.dot(a, b, preferred_element_type=F32)


def _iota(shape, dim):
    return lax.broadcasted_iota(jnp.int32, shape, dim)


def _params(*sem):
    return pltpu.CompilerParams(dimension_semantics=sem, vmem_limit_bytes=VMEM_LIMIT)


def _alibi_slopes(n):
    return jnp.exp2(-8.0 * jnp.arange(1, n + 1, dtype=F32) / n)


def _smem():
    return pl.BlockSpec(memory_space=pltpu.SMEM)


def _mm_kernel(x_ref, w_ref, o_ref):
    o_ref[...] = _dot(x_ref[...].astype(BF16), w_ref[...])


def _pick_tn(n):
    best = LANES
    for t in range(LANES, 1536 + 1, LANES):
        if n % t == 0:
            best = t
    return best


def _matmul(x, w, tm):
    m, k = x.shape
    n = w.shape[1]
    tm = min(tm, m)
    assert m % tm == 0
    tn = _pick_tn(n)
    return pl.pallas_call(
        _mm_kernel,
        grid=(m // tm, n // tn),
        in_specs=[pl.BlockSpec((tm, k), lambda i, j: (i, 0)),
                  pl.BlockSpec((k, tn), lambda i, j: (0, j))],
        out_specs=pl.BlockSpec((tm, tn), lambda i, j: (i, j)),
        out_shape=jax.ShapeDtypeStruct((m, n), F32),
        compiler_params=_params("parallel", "arbitrary"),
    )(x, w)


def _vt_kernel(x_ref, w_ref, o_ref):
    o_ref[...] = _dot_nt(w_ref[...], x_ref[...].astype(BF16)).astype(BF16)


def _matmul_t(x, wt, tm):
    m, k = x.shape
    n = wt.shape[0]
    return pl.pallas_call(
        _vt_kernel,
        grid=(m // tm,),
        in_specs=[pl.BlockSpec((tm, k), lambda i: (i, 0)), pl.BlockSpec((n, k), lambda i: (0, 0))],
        out_specs=pl.BlockSpec((n, tm), lambda i: (0, i)),
        out_shape=jax.ShapeDtypeStruct((n, m), BF16),
        compiler_params=_params("parallel"),
    )(x, wt)


def _out_ln_kernel(o_ref, z_ref, x_ref, w_ref, g_ref, b_ref, y_ref):
    z = z_ref[...]
    a = (o_ref[...] * (z * jax.nn.sigmoid(z))).astype(BF16)
    h = ALPHA * x_ref[...] + _dot(a, w_ref[...])
    hc = h - jnp.mean(h, axis=-1, keepdims=True)
    var = jnp.mean(hc * hc, axis=-1, keepdims=True)
    y_ref[...] = hc * lax.rsqrt(var + LN_EPS) * g_ref[...] + b_ref[...]


def _out_ln(o, p, z_col, x, w, g, b, tm):
    m = x.shape[0]
    row = lambda i: (i, 0)
    return pl.pallas_call(
        _out_ln_kernel,
        grid=(m // tm,),
        in_specs=[pl.BlockSpec((tm, D_MODEL), row),
                  pl.BlockSpec((tm, D_MODEL), lambda i: (i, z_col)),
                  pl.BlockSpec((tm, D_MODEL), row),
                  pl.BlockSpec((D_MODEL, D_MODEL), lambda i: (0, 0)),
                  pl.BlockSpec((1, D_MODEL), lambda i: (0, 0)),
                  pl.BlockSpec((1, D_MODEL), lambda i: (0, 0))],
        out_specs=pl.BlockSpec((tm, D_MODEL), row),
        out_shape=jax.ShapeDtypeStruct((m, D_MODEL), F32),
        compiler_params=_params("parallel"),
    )(o, p, x, w, g.reshape(1, -1), b.reshape(1, -1))


def _topk_mask(s, cand, lane_f, k):
    s = jnp.where(cand, s, NEG)
    sel = jnp.zeros(s.shape, F32)
    for _ in range(k):
        mx = jnp.max(s, axis=1, keepdims=True)
        idx = jnp.min(jnp.where(s == mx, lane_f, 1e9), axis=1, keepdims=True)
        pick = lane_f == idx
        valid = jnp.where(mx > 0.5 * NEG, 1.0, 0.0)
        sel = jnp.where(pick, valid, sel)
        s = jnp.where(pick, -3e38, s)
    return sel


def _lambda(lam_ref, lam_init):
    a = lam_ref[...]
    return (jnp.exp(jnp.sum(a[0:1] * a[1:2], axis=1, keepdims=True))
            - jnp.exp(jnp.sum(a[2:3] * a[3:4], axis=1, keepdims=True)) + lam_init)


def _block_mean_kernel(k_ref, o_ref):
    n = pl.program_id(1)

    @pl.when(n == 0)
    def _():
        o_ref[...] = jnp.zeros(o_ref.shape, F32)

    o_ref[0, pl.ds(n, 1), :] = jnp.sum(k_ref[...], axis=0, keepdims=True) / A_BLOCK


def _block_mean(p, batch, k_col):
    nb = p.shape[0] // batch // A_BLOCK
    return pl.pallas_call(
        _block_mean_kernel,
        grid=(batch, nb),
        in_specs=[pl.BlockSpec((A_BLOCK, D_MODEL), lambda b, n: (b * nb + n, k_col))],
        out_specs=pl.BlockSpec((1, LANES, D_MODEL), lambda b, n: (b, 0, 0)),
        out_shape=jax.ShapeDtypeStruct((batch, LANES, D_MODEL), F32),
        compiler_params=_params("parallel", "arbitrary"),
    )(p)


LOG2E = math.log2(math.e)
AUX0 = HEAD_DIM
SEL0 = 96
Q_SCALE = SCALE * LOG2E


def _split3(x):
    hi = x.astype(BF16).astype(F32)
    mid = (x - hi).astype(BF16).astype(F32)
    return hi, mid, x - hi - mid


def _lane_pick(lane, base, vals):
    out = jnp.zeros(lane.shape, F32)
    for idx, v in enumerate(vals):
        out = jnp.where(lane == base + idx, v, out)
    return out


def _slope_lanes(lane, sl_ref, h):
    parts = [sl_ref[h, c] for c in range(3)]
    return _lane_pick(lane, AUX0, parts + parts)


def _pos_lanes(lane, kpos):
    hi = kpos.astype(BF16).astype(F32)
    lo = kpos - hi
    return _lane_pick(lane, AUX0, [hi, hi, hi, lo, lo, lo])


def _head_cols(ref, h):
    x = ref[:, (h // 2) * LANES:(h // 2 + 1) * LANES]
    return x if h % 2 == 0 else pltpu.roll(x, HEAD_DIM, 1)


def _prep_moba_kernel(sl_ref, q_ref, k_ref, bm_ref, qa_ref, ka_ref, *, tm):
    i = pl.program_id(1)
    lane = _iota((tm, LANES), 1)
    lane_f = lane.astype(F32)
    low = lane < HEAD_DIM
    kpos = (i * tm + _iota((tm, LANES), 0)).astype(F32)
    k_aux = jnp.where(lane >= SEL0, jnp.where(lane == SEL0 + i, 1.0, 0.0), _pos_lanes(lane, kpos))
    bm_low = _iota((LANES, LANES), 1) < HEAD_DIM
    for h in range(N_HEADS):
        cs = slice(h * LANES, (h + 1) * LANES)
        q = _head_cols(q_ref, h)
        bm = bm_ref[0, :, (h // 2) * LANES:(h // 2 + 1) * LANES]
        bme = jnp.where(bm_low if h % 2 == 0 else ~bm_low, bm, 0.0).astype(BF16)
        qsel = q_ref[:, (h // 2) * LANES:(h // 2 + 1) * LANES].astype(BF16)
        sel = _topk_mask(_dot_nt(qsel, bme), lane < i, lane_f, A_TOPK)
        penalty = pltpu.roll(jnp.where((sel > 0.5) | (lane == i), 0.0, NEG), SEL0, 1)
        q_aux = jnp.where(lane >= SEL0, penalty, _slope_lanes(lane, sl_ref, h))
        qa_ref[:, cs] = jnp.where(low, q * Q_SCALE, q_aux).astype(BF16)
        ka_ref[:, cs] = jnp.where(low, _head_cols(k_ref, h), k_aux).astype(BF16)


def _prep_moba(p, batch, sl3, bm):
    tm = A_BLOCK
    nt = p.shape[0] // batch // tm
    assert nt <= LANES - SEL0
    row = lambda col: pl.BlockSpec((tm, D_MODEL), lambda b, i: (b * nt + i, col))
    aug = pl.BlockSpec((tm, N_HEADS * LANES), lambda b, i: (b * nt + i, 0))
    shape = jax.ShapeDtypeStruct((p.shape[0], N_HEADS * LANES), BF16)
    return pl.pallas_call(
        functools.partial(_prep_moba_kernel, tm=tm),
        grid=(batch, nt),
        in_specs=[_smem(), row(0), row(1), pl.BlockSpec((1, LANES, D_MODEL), lambda b, i: (b, 0, 0))],
        out_specs=[aug, aug],
        out_shape=[shape, shape],
        compiler_params=_params("parallel", "parallel"),
    )(sl3, p, p, bm)


def _prep_kernel(*refs, tm, mode):
    if mode == "diff":
        sl_ref, q_ref, k_ref, qa_ref, ka_ref = refs
    elif mode == "fox":
        q_ref, k_ref, c_ref, qa_ref, ka_ref = refs
    elif mode == "win":
        sl_ref, q_ref, k_ref, qa_ref, ka_ref = refs
    else:
        sl_ref, q_ref, k_ref, sel_ref, qa_ref, ka_ref = refs
    i = pl.program_id(1)
    lane = _iota((tm, LANES), 1)
    low = lane < HEAD_DIM
    kpos = (i * tm + _iota((tm, LANES), 0)).astype(F32)
    cw = 2 * LANES if mode == "sel" else LANES
    for h in range(N_HEADS):
        if mode == "fox":
            q_aux = _lane_pick(lane, AUX0, [-1.0, -1.0, -1.0])
        else:
            q_aux = _slope_lanes(lane, sl_ref, h)
        qa_ref[:, h * cw:h * cw + LANES] = jnp.where(low, _head_cols(q_ref, h) * Q_SCALE, q_aux).astype(BF16)
        if mode == "sel":
            g = h // C_RATIO
            picked = sel_ref[:, g * LANES:(g + 1) * LANES]
            qa_ref[:, h * cw + LANES:(h + 1) * cw] = jnp.where(picked > 0.5, 0.0, NEG).astype(BF16)
        if mode == "diff":
            ka_ref[:, h * cw:(h + 1) * cw] = jnp.where(low, _head_cols(k_ref, h), _pos_lanes(lane, kpos)).astype(BF16)
        if mode == "fox":
            c = jnp.broadcast_to(c_ref[:, h:h + 1], (tm, LANES)) * LOG2E
            ka_ref[:, h * cw:(h + 1) * cw] = jnp.where(low, _head_cols(k_ref, h), _lane_pick(lane, AUX0, _split3(c))).astype(BF16)
    if mode in ("win", "sel"):
        for g in range(C_GROUPS):
            ka_ref[:, g * cw:g * cw + LANES] = jnp.where(low, _head_cols(k_ref, g), _pos_lanes(lane, kpos)).astype(BF16)
            if mode == "sel":
                own = (i * tm + _iota((tm, LANES), 0)) // C_SEL_BLOCK
                ka_ref[:, g * cw + LANES:(g + 1) * cw] = jnp.where(lane == own, 1.0, 0.0).astype(BF16)


def _prep(p, batch, mode, k_col, extras, tm):
    nt = p.shape[0] // batch // tm
    cw = 2 * LANES if mode == "sel" else LANES
    nk = C_GROUPS if mode in ("win", "sel") else N_HEADS
    kw = LANES if mode in ("win", "sel") else D_MODEL
    row = lambda width, col: pl.BlockSpec((tm, width), lambda b, i: (b * nt + i, col))
    in_specs = ([] if mode == "fox" else [_smem()]) + [row(D_MODEL, 0), row(kw, k_col)]
    if mode == "fox":
        in_specs.append(row(N_HEADS, 0))
    if mode == "sel":
        assert extras[-1].shape[1] == C_GROUPS * LANES
        in_specs.append(row(C_GROUPS * LANES, 0))
    return pl.pallas_call(
        functools.partial(_prep_kernel, tm=tm, mode=mode),
        grid=(batch, nt),
        in_specs=in_specs,
        out_specs=[row(N_HEADS * cw, 0), row(nk * cw, 0)],
        out_shape=[jax.ShapeDtypeStruct((p.shape[0], N_HEADS * cw), BF16),
                   jax.ShapeDtypeStruct((p.shape[0], nk * cw), BF16)],
        compiler_params=_params("parallel", "parallel"),
    )(*(extras[:1] if mode != "fox" else []), p, p, *(extras if mode == "fox" else extras[1:]))


def _sweep_tables(nq, band):
    it, jt, ft, mt = [], [], [], []
    for i in range(nq):
        js = [j for j in (i - 1, i) if j >= 0] if band else list(range(i + 1))
        for n, j in enumerate(js):
            it.append(i)
            jt.append(j)
            ft.append(1 if n == 0 else 0)
            mt.append(1 if j == i else (2 if band else 0))
    return [np.asarray(a, np.int32) for a in (it, jt, ft, mt)]


def _flash_t_kernel(it_ref, jt_ref, ft_ref, mt_ref, *rest, t, cw, kdiv, vdiv, dv, fin, modes, lam_init, gate_idx,
                    has_live):
    if has_live:
        live_ref, rest = rest[0], rest[1:]
    q_ref, k_ref, v_ref, rest = rest[0], rest[1], rest[2], rest[3:]
    if fin == "diff":
        lam_ref, g_ref, o_ref, m_scr, l_scr, acc_scr = rest
    elif fin == "nsa":
        g_ref, prev_ref, o_ref, m_scr, l_scr, acc_scr = rest
    else:
        o_ref, m_scr, l_scr, acc_scr = rest
    st = pl.program_id(1)

    @pl.when(ft_ref[st] == 1)
    def _():
        m_scr[...] = jnp.full(m_scr.shape, NEG, F32)
        l_scr[...] = jnp.zeros(l_scr.shape, F32)
        acc_scr[...] = jnp.zeros(acc_scr.shape, F32)

    def tile(mode):
        if mode:
            diff = _iota((t, t), 0) - _iota((t, t), 1)
            allowed = diff <= 0 if mode == 1 else diff >= 0

        def scores(h):
            kh = k_ref[:, (h // kdiv) * cw:(h // kdiv + 1) * cw]
            return _dot_nt(kh, q_ref[:, h * cw:(h + 1) * cw])

        s_next = scores(0)
        for h in range(N_HEADS):
            s = s_next
            if h + 1 < N_HEADS:
                s_next = scores(h + 1)
            if mode:
                s = jnp.where(allowed, s, NEG)
            m_prev = m_scr[h:h + 1, :]
            m_new = jnp.maximum(m_prev, jnp.max(s, axis=0, keepdims=True))
            p = jnp.exp2(s - m_new)
            corr = jnp.exp2(m_prev - m_new)
            l_scr[h:h + 1, :] = corr * l_scr[h:h + 1, :] + jnp.sum(p, axis=0, keepdims=True)
            m_scr[h:h + 1, :] = m_new
            rows = slice(h * dv, (h + 1) * dv)
            vh = v_ref[(h // vdiv) * dv:(h // vdiv + 1) * dv, :]
            acc_scr[rows, :] = acc_scr[rows, :] * corr + _dot(vh, p.astype(BF16))

    for mode in modes:
        run = mt_ref[st] == mode
        if has_live and mode == 0:
            run = run & (live_ref[pl.program_id(0) * pl.num_programs(1) + st] != 0)
        pl.when(run)(functools.partial(tile, mode))

    @pl.when(it_ref[st] == jt_ref[st])
    def _():
        lo = _iota((t, LANES), 1) < HEAD_DIM
        for hp in range(N_HEADS // 2):
            cs = slice(hp * LANES, (hp + 1) * LANES)
            if fin == "diff":
                lam = _lambda(lam_ref, lam_init)
                o = (acc_scr[2 * hp * dv:(2 * hp + 1) * dv, :] / l_scr[2 * hp:2 * hp + 1, :]
                     - lam * (acc_scr[(2 * hp + 1) * dv:(2 * hp + 2) * dv, :] / l_scr[2 * hp + 1:2 * hp + 2, :]))
                o = o * lax.rsqrt(jnp.mean(o * o, axis=0, keepdims=True) + LN_EPS)
                o_ref[:, cs] = o.T * g_ref[...] * (1.0 - lam_init)
            else:
                o = jnp.concatenate(
                    [acc_scr[(2 * hp + e) * dv:(2 * hp + e + 1) * dv, :] / l_scr[2 * hp + e:2 * hp + e + 1, :]
                     for e in range(2)], axis=0).T
                if fin == "nsa":
                    g0 = jax.nn.sigmoid(g_ref[:, 6 * hp + gate_idx:6 * hp + gate_idx + 1])
                    g1 = jax.nn.sigmoid(g_ref[:, 6 * hp + 3 + gate_idx:6 * hp + 4 + gate_idx])
                    o = prev_ref[:, cs] + o * jnp.where(lo, g0, g1)
                o_ref[:, cs] = o


def _flash_t(qa, ka, vt, batch, *, fin, band=False, extras=(), lam_init=0.0, gate_idx=0, live=None, t=512):
    m = qa.shape[0]
    nq = m // batch // t
    cw = qa.shape[1] // N_HEADS
    kdiv = N_HEADS // (ka.shape[1] // cw)
    dv = LANES if fin == "diff" else HEAD_DIM
    vdiv = N_HEADS // (vt.shape[0] // dv)
    tabs = _sweep_tables(nq, band)
    if live is not None:
        tabs.append(live[:, tabs[0], tabs[1]].reshape(-1).astype(jnp.int32))
    modes = (1, 2) if band else (0, 1)
    assert not band or t == C_WINDOW
    imap = lambda f: (lambda b, s, it_, jt_, *_: f(b, s, it_, jt_))
    qrow = lambda width: pl.BlockSpec((t, width), imap(lambda b, s, it_, jt_: (b * nq + it_[s], 0)))
    in_specs = [qrow(qa.shape[1]),
                pl.BlockSpec((t, ka.shape[1]), imap(lambda b, s, it_, jt_: (b * nq + jt_[s], 0))),
                pl.BlockSpec((vt.shape[0], t), imap(lambda b, s, it_, jt_: (0, b * nq + jt_[s])))]
    if fin == "diff":
        in_specs += [pl.BlockSpec((4, HEAD_DIM), imap(lambda b, s, it_, jt_: (0, 0))),
                     pl.BlockSpec((1, LANES), imap(lambda b, s, it_, jt_: (0, 0)))]
    if fin == "nsa":
        in_specs += [pl.BlockSpec((t, LANES), imap(lambda b, s, it_, jt_: (b * nq + it_[s], C_GATE_COL))),
                     qrow(D_MODEL)]
    return pl.pallas_call(
        functools.partial(_flash_t_kernel, t=t, cw=cw, kdiv=kdiv, vdiv=vdiv, dv=dv, fin=fin, modes=modes,
                          lam_init=lam_init, gate_idx=gate_idx, has_live=live is not None),
        grid_spec=pltpu.PrefetchScalarGridSpec(
            num_scalar_prefetch=len(tabs),
            grid=(batch, len(tabs[0])),
            in_specs=in_specs,
            out_specs=qrow(D_MODEL),
            scratch_shapes=[pltpu.VMEM((N_HEADS, t), F32), pltpu.VMEM((N_HEADS, t), F32),
                            pltpu.VMEM((N_HEADS * dv, t), F32)]),
        out_shape=jax.ShapeDtypeStruct((m, D_MODEL), F32),
        compiler_params=_params("parallel", "arbitrary"),
    )(*tabs, qa, ka, vt, *extras)


def _logf_kernel(f_ref, b_ref, o_ref):
    x = f_ref[:, 0:N_HEADS] + b_ref[...]
    o_ref[...] = jnp.minimum(x, 0.0) - jnp.log1p(jnp.exp(-jnp.abs(x)))


def _logf(p, f_col, b_f, tm):
    m = p.shape[0]
    return pl.pallas_call(
        _logf_kernel,
        grid=(m // tm,),
        in_specs=[pl.BlockSpec((tm, LANES), lambda i: (i, f_col)),
                  pl.BlockSpec((1, N_HEADS), lambda i: (0, 0))],
        out_specs=pl.BlockSpec((tm, N_HEADS), lambda i: (i, 0)),
        out_shape=jax.ShapeDtypeStruct((m, N_HEADS), F32),
        compiler_params=_params("parallel"),
    )(p, b_f.reshape(1, -1))


def _cumsum_kernel(x_ref, o_ref, carry_scr, *, t):
    @pl.when(pl.program_id(1) == 0)
    def _():
        carry_scr[...] = jnp.zeros(carry_scr.shape, F32)

    x = x_ref[0]
    tri = (_iota((t, t), 0) <= _iota((t, t), 1)).astype(F32)
    c = jnp.dot(x, tri, precision=HIGHEST, preferred_element_type=F32) + carry_scr[...]
    o_ref[0] = c
    carry_scr[...] = c[:, t - 1:t]


def _cumsum_lanes(x, t):
    b, h, s = x.shape
    return pl.pallas_call(
        functools.partial(_cumsum_kernel, t=t),
        grid=(b, s // t),
        in_specs=[pl.BlockSpec((1, h, t), lambda bi, n: (bi, 0, n))],
        out_specs=pl.BlockSpec((1, h, t), lambda bi, n: (bi, 0, n)),
        out_shape=jax.ShapeDtypeStruct(x.shape, F32),
        scratch_shapes=[pltpu.VMEM((h, 1), F32)],
        compiler_params=_params("parallel", "arbitrary"),
    )(x)


C_Z_COL = 1
C_KV_COL = 16
C_GATE_COL = 22
C_WIDTH = 3072


def _compress_kernel(kc_ref, vc_ref, wk_ref, wv_ref, ck_ref, cv_ref, *, nch):
    stride = 2 * C_CMP_BLOCK
    for src, w_ref, dst in ((kc_ref, wk_ref, ck_ref), (vc_ref, wv_ref, cv_ref)):
        for parity in range(2):
            acc = jnp.zeros((nch, LANES), F32)
            for tt in range(C_CMP_BLOCK):
                rows = src[pl.ds(parity * C_CMP_BLOCK + tt, nch, stride=stride), :]
                acc = acc + _dot(rows.astype(BF16), w_ref[tt])
            dst[0, parity * nch:(parity + 1) * nch, :] = acc


def _compress_prompt(p, batch, w2k, w2v):
    s = p.shape[0] // batch
    nch = s // (2 * C_CMP_BLOCK)
    col = lambda c: pl.BlockSpec((s, LANES), lambda b: (b, c))
    wspec = pl.BlockSpec((C_CMP_BLOCK, LANES, LANES), lambda b: (0, 0, 0))
    ospec = pl.BlockSpec((1, 2 * nch, LANES), lambda b: (b, 0, 0))
    oshape = jax.ShapeDtypeStruct((batch, 2 * nch, LANES), F32)
    return pl.pallas_call(
        functools.partial(_compress_kernel, nch=nch),
        grid=(batch,),
        in_specs=[col(C_KV_COL), col(C_KV_COL + 1), wspec, wspec],
        out_specs=[ospec, ospec],
        out_shape=[oshape, oshape],
        compiler_params=_params("parallel"),
    )(p, p, w2k, w2v)


def _compress_pool_kernel(x_ref, w_ref, o_ref):
    acc = jnp.zeros(o_ref.shape, F32)
    for d in range(HEAD_DIM):
        acc = acc + _dot(x_ref[:, d, :].astype(BF16), w_ref[d])
    o_ref[...] = acc


def _compress_pool(cache, w_cmp):
    n_pool = cache.shape[0]
    tokens = PAGE // C_CMP_BLOCK
    x = cache.transpose(0, 2, 3, 1).reshape(n_pool * C_GROUPS, HEAD_DIM, PAGE)
    r = jnp.arange(PAGE)
    in_token = (r[:, None] // C_CMP_BLOCK == jnp.arange(tokens)[None, :]).astype(w_cmp.dtype)
    w_rows = w_cmp.reshape(C_CMP_BLOCK, HEAD_DIM, HEAD_DIM)[r % C_CMP_BLOCK]
    wx = (w_rows[:, :, None, :] * in_token[:, None, :, None]).transpose(1, 0, 2, 3)
    wx = wx.reshape(HEAD_DIM, PAGE, tokens * HEAD_DIM).astype(BF16)
    tm = min(256, x.shape[0])
    assert x.shape[0] % tm == 0
    out = pl.pallas_call(
        _compress_pool_kernel,
        grid=(x.shape[0] // tm,),
        in_specs=[pl.BlockSpec((tm, HEAD_DIM, PAGE), lambda i: (i, 0, 0)),
                  pl.BlockSpec(wx.shape, lambda i: (0, 0, 0))],
        out_specs=pl.BlockSpec((tm, tokens * HEAD_DIM), lambda i: (i, 0)),
        out_shape=jax.ShapeDtypeStruct((x.shape[0], tokens * HEAD_DIM), F32),
        compiler_params=_params("parallel"),
    )(x, wx)
    return out.reshape(n_pool, C_GROUPS, tokens, HEAD_DIM).transpose(0, 2, 1, 3).reshape(n_pool, tokens, LANES)


def _group_halves(x, g):
    lo = _iota(x.shape, 1) < HEAD_DIM
    base = jnp.where(lo if g == 0 else ~lo, x, 0.0)
    other = pltpu.roll(base, HEAD_DIM, 1)
    pair = (base, other) if g == 0 else (other, base)
    return pair[0].astype(BF16), pair[1].astype(BF16)


def _nsa_cmp_kernel(sl_ref, q_ref, g_ref, ck_ref, cv_ref, o_ref, sel_ref, *, tq, nch):
    q0 = pl.program_id(1) * tq
    nl = 2 * nch
    lane = _iota((tq, nl), 1)
    qpos = q0 + _iota((tq, nl), 0)
    tok = jnp.where(lane < nch, 2 * lane, 2 * (lane - nch) + 1)
    endp = (tok + 1) * C_CMP_BLOCK - 1
    okc = endp <= qpos
    relc = (endp[0:1, :] - q0).astype(F32)
    lane_s = _iota((tq, nch), 1)
    qblk = (q0 + _iota((tq, nch), 0)) // C_SEL_BLOCK
    lane_sf = lane_s.astype(F32)
    for g in range(C_GROUPS):
        k_lo, k_hi = _group_halves(ck_ref[0], g)
        v_lo, v_hi = _group_halves(cv_ref[0], g)
        imp = jnp.zeros((tq, nl), F32)
        for hp in range(g * C_RATIO // 2, (g + 1) * C_RATIO // 2):
            cs = slice(hp * LANES, (hp + 1) * LANES)
            qp = (q_ref[:, cs] * SCALE).astype(BF16)
            o_pair = None
            for e in range(2):
                h = 2 * hp + e
                s = _dot_nt(qp, k_lo if e == 0 else k_hi) + sl_ref[h] * relc
                s = jnp.where(okc, s, NEG)
                pc = jnp.where(okc, jnp.exp(s - jnp.max(s, axis=1, keepdims=True)), 0.0)
                pc = pc / jnp.maximum(jnp.sum(pc, axis=1, keepdims=True), TINY)
                imp = imp + pc
                gate = jax.nn.sigmoid(g_ref[:, 3 * h:3 * h + 1])
                o = _dot(pc.astype(BF16), v_lo if e == 0 else v_hi) * gate
                o_pair = o if o_pair is None else o_pair + o
            o_ref[:, cs] = o_pair
        imp_sel = imp[:, 0:nch] + imp[:, nch:nl]
        sel = _topk_mask(imp_sel, lane_s < qblk, lane_sf, C_TOPK)
        sel = jnp.where(lane_s == qblk, 1.0, sel)
        if nch < LANES:
            sel = jnp.concatenate([sel, jnp.zeros((tq, LANES - nch), F32)], axis=1)
        sel_ref[:, g * LANES:(g + 1) * LANES] = sel


def _nsa_cmp(p, batch, slopes, ck, cv, tq):
    nq = p.shape[0] // batch // tq
    nch = ck.shape[1] // 2
    assert nch <= LANES
    row = lambda width, col: pl.BlockSpec((tq, width), lambda b, i: (b * nq + i, col))
    cspec = pl.BlockSpec((1, 2 * nch, LANES), lambda b, i: (b, 0, 0))
    return pl.pallas_call(
        functools.partial(_nsa_cmp_kernel, tq=tq, nch=nch),
        grid=(batch, nq),
        in_specs=[_smem(), row(D_MODEL, 0), row(LANES, C_GATE_COL), cspec, cspec],
        out_specs=[row(D_MODEL, 0), row(C_GROUPS * LANES, 0)],
        out_shape=[jax.ShapeDtypeStruct((p.shape[0], D_MODEL), F32),
                   jax.ShapeDtypeStruct((p.shape[0], C_GROUPS * LANES), F32)],
        compiler_params=_params("parallel", "parallel"),
    )(slopes, p, p, ck, cv)


def _decode_kernel(pt_ref, q_ref, kn_ref, vn_ref, *rest, n_pages, mode, lam_init):
    del pt_ref
    gp = PAGES_PER_STEP
    if mode == "moba":
        slc_ref, rest = rest[0], rest[1:]
    elif mode == "diff":
        slc_ref, lam_ref, g_ref, rest = rest[0], rest[1], rest[2], rest[3:]
    else:
        lfn_ref, lft_refs, rest = rest[0], rest[1:1 + gp], rest[1 + gp:]
    kt_refs, v_refs, rest = rest[:gp], rest[gp:2 * gp], rest[2 * gp:]
    if mode == "fox":
        o_ref, qbd_scr, st_scr, p_scr, pn_scr, acc_scr, lf_scr = rest
    else:
        o_ref, qbd_scr, st_scr, p_scr, pn_scr, acc_scr = rest
    s = pl.program_id(1)
    n_steps = n_pages // gp
    past = n_pages * PAGE
    row = _iota((N_HEADS, D_MODEL), 0)
    lane = _iota((N_HEADS, D_MODEL), 1)
    lane_b = _iota((N_HEADS, LANES), 1)

    @pl.when(s == 0)
    def _():
        qbd_scr[...] = jnp.where(lane // HEAD_DIM == row, jnp.broadcast_to(q_ref[0], (N_HEADS, D_MODEL)), 0.0)
        acc_scr[...] = jnp.zeros(acc_scr.shape, F32)

    @pl.when(s < n_steps)
    def _():
        qb = (qbd_scr[...] * SCALE).astype(BF16)
        for g in range(gp):
            off = pl.multiple_of((s * gp + g) * PAGE, PAGE)
            st_scr[:, pl.ds(off, PAGE)] = _dot(qb, kt_refs[g][0].astype(BF16))
            if mode == "fox":
                lf_scr[:, pl.ds(off, PAGE)] = lft_refs[g][0]

    @pl.when(s == n_steps - 1)
    def _():
        qf = qbd_scr[...]
        qk_new = jnp.sum(qf * jnp.broadcast_to(kn_ref[0], qf.shape), axis=1, keepdims=True)
        s_new = qk_new * SCALE
        if mode == "fox":
            c = lf_scr[...]
            kpos = _iota(c.shape, 1)
            shift = 1
            while shift < past:
                c = c + jnp.where(kpos >= shift, pltpu.roll(c, shift, 1), 0.0)
                shift *= 2
            s_all = st_scr[...] - c
            s_new = s_new - (c[:, past - 1:past] + lfn_ref[0])
        else:
            kpos = _iota((1, past), 1)
            s_all = st_scr[...] + slc_ref[...] * (kpos - past).astype(F32)
        if mode == "moba":
            sb = jnp.where(lane_b == past // A_BLOCK, qk_new / A_BLOCK, 0.0)
            for n in range(past // A_BLOCK):
                blk = jnp.sum(st_scr[:, n * A_BLOCK:(n + 1) * A_BLOCK], axis=1, keepdims=True) / (SCALE * A_BLOCK)
                sb = jnp.where(lane_b == n, blk, sb)
            sel = _topk_mask(sb, lane_b < past // A_BLOCK, lane_b.astype(F32), A_TOPK)
            st_scr[...] = s_all
            for n in range(past // A_BLOCK):
                cols = slice(n * A_BLOCK, (n + 1) * A_BLOCK)
                st_scr[:, cols] = st_scr[:, cols] + jnp.where(sel[:, n:n + 1] > 0.5, 0.0, NEG)
            s_all = st_scr[...]
        m = jnp.maximum(jnp.max(s_all, axis=1, keepdims=True), s_new)
        p = jnp.exp(s_all - m)
        pn = jnp.exp(s_new - m)
        l = jnp.sum(p, axis=1, keepdims=True) + pn
        p_scr[...] = p / l
        pn_scr[...] = pn / l

    @pl.when(s >= n_steps)
    def _():
        acc = acc_scr[...]
        for g in range(gp):
            off = pl.multiple_of(((s - n_steps) * gp + g) * PAGE, PAGE)
            pg = p_scr[:, pl.ds(off, PAGE)].astype(BF16)
            if mode == "diff":
                for h in range(B_HEADS):
                    vh = v_refs[g][0, pl.ds(h, PAGE, stride=B_HEADS), :]
                    acc = acc + jnp.where(_iota((N_HEADS, LANES), 0) // 2 == h, _dot(pg, vh.astype(BF16)), 0.0)
            else:
                acc = acc + _dot_nt(pg, v_refs[g][0].astype(BF16))
        acc_scr[...] = acc

    @pl.when(s == 2 * n_steps - 1)
    def _():
        acc = acc_scr[...]
        if mode == "diff":
            row_b = _iota((N_HEADS, LANES), 0)
            acc = acc + pn_scr[...] * vn_ref[0]
            signed = jnp.where(row_b % 2 == 0, acc, -_lambda(lam_ref, lam_init) * acc)
            o = signed + pltpu.roll(signed, N_HEADS - 1, 0)
            o = o * lax.rsqrt(jnp.mean(o * o, axis=1, keepdims=True) + LN_EPS)
            o_ref[0] = o * g_ref[...] * (1.0 - lam_init)
        else:
            acc = acc + pn_scr[...] * jnp.broadcast_to(vn_ref[0], (N_HEADS, D_MODEL))
            o_ref[0] = jnp.sum(jnp.where(lane // HEAD_DIM == row, acc, 0.0), axis=0, keepdims=True)


def _decode(ps3, vn, page_table, kt_pool, v_pool, mode, extras, lam_init=0.0):
    b = ps3.shape[0]
    n_pages = page_table.shape[1]
    gp = PAGES_PER_STEP
    assert n_pages % gp == 0 and (n_pages * PAGE) % A_BLOCK == 0
    n_steps = n_pages // gp
    past = n_pages * PAGE
    pt = page_table.reshape(-1)
    rowspec = lambda col: pl.BlockSpec((1, 1, D_MODEL), lambda bi, s, pt_: (bi, 0, col))
    per_b = lambda shape: pl.BlockSpec((1,) + shape, lambda bi, s, pt_: (bi,) + tuple(0 for _ in shape))
    full = lambda shape: pl.BlockSpec(shape, lambda bi, s, pt_: tuple(0 for _ in shape))
    kpage = lambda g: (lambda bi, s, pt_: (pt_[bi * n_pages + jnp.minimum(s, n_steps - 1) * gp + g], 0, 0))
    vpage = lambda g: (lambda bi, s, pt_: (pt_[bi * n_pages + jnp.maximum(s - n_steps, 0) * gp + g], 0, 0))
    kspecs = [pl.BlockSpec((1,) + kt_pool.shape[1:], kpage(g)) for g in range(gp)]
    vspecs = [pl.BlockSpec((1,) + v_pool.shape[1:], vpage(g)) for g in range(gp)]
    out_tile = (N_HEADS, LANES) if mode == "diff" else (1, D_MODEL)
    scratch = [pltpu.VMEM((N_HEADS, D_MODEL), F32), pltpu.VMEM((N_HEADS, past), F32),
               pltpu.VMEM((N_HEADS, past), F32), pltpu.VMEM((N_HEADS, 1), F32),
               pltpu.VMEM((N_HEADS, LANES if mode == "diff" else D_MODEL), F32)]
    args = list(extras)
    if mode == "moba":
        in_specs = [full((N_HEADS, 1))]
    elif mode == "diff":
        in_specs = [full((N_HEADS, 1)), full((4, HEAD_DIM)), full((1, LANES))]
    else:
        lft, lfn = extras
        in_specs = [per_b((N_HEADS, 1))] + [pl.BlockSpec((1, N_HEADS, PAGE), kpage(g)) for g in range(gp)]
        args = [lfn] + [lft] * gp
        scratch += [pltpu.VMEM((N_HEADS, past), F32)]
    vn_spec, vn_arg = (per_b(vn.shape[1:]), vn) if mode == "diff" else (rowspec(2), ps3)
    out = pl.pallas_call(
        functools.partial(_decode_kernel, n_pages=n_pages, mode=mode, lam_init=lam_init),
        grid_spec=pltpu.PrefetchScalarGridSpec(
            num_scalar_prefetch=1,
            grid=(b, 2 * n_steps),
            in_specs=[rowspec(0), rowspec(1), vn_spec] + in_specs + kspecs + vspecs,
            out_specs=per_b(out_tile),
            scratch_shapes=scratch),
        out_shape=jax.ShapeDtypeStruct((b,) + out_tile, F32),
        compiler_params=_params("parallel", "arbitrary"),
    )(pt, ps3, ps3, vn_arg, *args, *([kt_pool] * gp), *([v_pool] * gp))
    if mode == "diff":
        out = out[:, 0::2, :]
    return out.reshape(b, D_MODEL)


C_HALF = 64


def _softmax_new(s_all, s_new, v_all, v_new, v_transposed):
    m = jnp.maximum(jnp.max(s_all, axis=1, keepdims=True), s_new)
    p = jnp.exp(s_all - m)
    pn = jnp.exp(s_new - m)
    l = jnp.sum(p, axis=1, keepdims=True) + pn
    pv = _dot_nt(p.astype(BF16), v_all.astype(BF16)) if v_transposed else _dot(p.astype(BF16), v_all.astype(BF16))
    return (pv + pn * v_new) / l


def _nsa_decode_kernel(pt_ref, qbd_ref, g_ref, slc_ref, kcn_ref, vcn_ref, ksn_ref, vsn_ref, kwn_ref, vwn_ref,
                       kw_ref, vw_ref, wk0_ref, wv0_ref, *rest, n_pages):
    del pt_ref
    gp = NSA_PAGES_PER_STEP
    ckp_refs, cvp_refs, kst_refs, vst_refs = (rest[i * gp:(i + 1) * gp] for i in range(4))
    o_ref, ck_scr, cv_scr, kst_scr, vst_scr = rest[4 * gp:]
    s = pl.program_id(1)
    past = n_pages * PAGE
    per_page = PAGE // C_CMP_BLOCK // 2

    @pl.when(s == 0)
    def _():
        ck_scr[...] = jnp.zeros(ck_scr.shape, F32)
        cv_scr[...] = jnp.zeros(cv_scr.shape, F32)

    for g in range(gp):
        page = s * gp + g
        off = pl.multiple_of(page * PAGE, PAGE)
        kst_scr[:, pl.ds(off, PAGE)] = kst_refs[g][0]
        vst_scr[:, pl.ds(off, PAGE)] = vst_refs[g][0]
        for u in range(2 * per_page):
            dst = (u % 2) * C_HALF + per_page * page + u // 2
            ck_scr[pl.ds(dst, 1), :] = ckp_refs[g][0, u:u + 1, :]
            cv_scr[pl.ds(dst, 1), :] = cvp_refs[g][0, u:u + 1, :]

    @pl.when(s == n_pages // gp - 1)
    def _():
        qf = qbd_ref[0] * SCALE
        qb = qf.astype(BF16)
        slc = slc_ref[...]
        new_tok = per_page * n_pages
        ck_scr[new_tok:new_tok + 1, :] = _dot(jnp.broadcast_to(kcn_ref[0], (8, LANES)).astype(BF16), wk0_ref[...])[0:1]
        cv_scr[new_tok:new_tok + 1, :] = _dot(jnp.broadcast_to(vcn_ref[0], (8, LANES)).astype(BF16), wv0_ref[...])[0:1]
        lane = _iota((N_HEADS, LANES), 1)
        row = _iota((N_HEADS, LANES), 0)
        tok = jnp.where(lane < C_HALF, 2 * lane, 2 * (lane - C_HALF) + 1)
        endp = (tok + 1) * C_CMP_BLOCK - 1
        okc = endp <= past
        sc = _dot_nt(qb, ck_scr[...].astype(BF16)) + slc * (endp - past).astype(F32)
        sc = jnp.where(okc, sc, NEG)
        pc = jnp.where(okc, jnp.exp(sc - jnp.max(sc, axis=1, keepdims=True)), 0.0)
        pc = pc / jnp.maximum(jnp.sum(pc, axis=1, keepdims=True), TINY)
        o_cmp = _dot(pc.astype(BF16), cv_scr[...].astype(BF16))
        imp = jnp.where(row < C_RATIO, jnp.sum(pc[0:C_RATIO], axis=0, keepdims=True),
                        jnp.sum(pc[C_RATIO:N_HEADS], axis=0, keepdims=True))
        imp = imp + pltpu.roll(imp, C_HALF, 1)
        sel = _topk_mask(imp, lane < past // C_SEL_BLOCK, lane.astype(F32), C_TOPK)
        expand = (_iota((LANES, past), 0) == _iota((LANES, past), 1) // C_SEL_BLOCK).astype(BF16)
        picked = _dot(sel.astype(BF16), expand)
        kpos = _iota((1, past), 1)
        ss = _dot(qb, kst_scr[...].astype(BF16)) + slc * (kpos - past).astype(F32)
        ss = jnp.where(picked > 0.5, ss, NEG)
        ss_new = jnp.sum(qf * ksn_ref[0], axis=1, keepdims=True)
        o_sel = _softmax_new(ss, ss_new, vst_scr[...], vsn_ref[0], True)
        wb = kw_ref.shape[1]
        wpos = past - wb + _iota((1, wb), 1)
        okw = (past - wpos <= C_WINDOW) & (wpos >= 0)
        sw = _dot_nt(qb, kw_ref[0].astype(BF16)) + slc * (wpos - past).astype(F32)
        sw = jnp.where(okw, sw, NEG)
        sw_new = jnp.sum(qf * kwn_ref[0], axis=1, keepdims=True)
        o_win = _softmax_new(sw, sw_new, vw_ref[0], vwn_ref[0], False)
        gate = jax.nn.sigmoid(g_ref[0])
        o_ref[0] = gate[:, 0:1] * o_cmp + gate[:, 1:2] * o_sel + gate[:, 2:3] * o_win


def _nsa_decode(ps3, qbd, gates, slc, page_table, ck_pool, cv_pool, kst_pool, vst_pool, kw_buf, vw_buf, wk0, wv0):
    b = ps3.shape[0]
    n_pages = page_table.shape[1]
    gp = NSA_PAGES_PER_STEP
    past = n_pages * PAGE
    assert 2 * n_pages + 1 <= C_HALF and n_pages % gp == 0
    pt = page_table.reshape(-1)
    per_b = lambda shape: pl.BlockSpec((1,) + shape, lambda bi, s, pt_: (bi,) + tuple(0 for _ in shape))
    full = lambda shape: pl.BlockSpec(shape, lambda bi, s, pt_: tuple(0 for _ in shape))
    newrow = lambda col: pl.BlockSpec((1, 1, LANES), lambda bi, s, pt_: (bi, 0, col))
    paged = lambda rows: [pl.BlockSpec((1, rows, LANES), (lambda g: lambda bi, s, pt_: (pt_[bi * n_pages + s * gp + g], 0, 0))(g))
                          for g in range(gp)]
    tokens = PAGE // C_CMP_BLOCK
    return pl.pallas_call(
        functools.partial(_nsa_decode_kernel, n_pages=n_pages),
        grid_spec=pltpu.PrefetchScalarGridSpec(
            num_scalar_prefetch=1,
            grid=(b, n_pages // gp),
            in_specs=[per_b((N_HEADS, LANES)), per_b((N_HEADS, 3)), full((N_HEADS, 1))]
                     + [newrow(C_KV_COL + c) for c in range(6)]
                     + [per_b(kw_buf.shape[1:]), per_b(vw_buf.shape[1:]), full((LANES, LANES)), full((LANES, LANES))]
                     + paged(tokens) + paged(tokens) + paged(PAGE) + paged(PAGE),
            out_specs=per_b((N_HEADS, LANES)),
            scratch_shapes=[pltpu.VMEM((2 * C_HALF, LANES), F32), pltpu.VMEM((2 * C_HALF, LANES), F32),
                            pltpu.VMEM((LANES, past), F32), pltpu.VMEM((LANES, past), F32)]),
        out_shape=jax.ShapeDtypeStruct((b, N_HEADS, LANES), F32),
        compiler_params=_params("parallel", "arbitrary"),
    )(pt, qbd, gates, slc, ps3, ps3, ps3, ps3, ps3, ps3, kw_buf, vw_buf, wk0, wv0,
      *([ck_pool] * gp), *([cv_pool] * gp), *([kst_pool] * gp), *([vst_pool] * gp))


def _pad_cols(w, width):
    return jnp.pad(w, ((0, 0), (0, width - w.shape[1])))


def _block_diag_cmp(w):
    w3 = w.reshape(C_CMP_BLOCK, HEAD_DIM, HEAD_DIM)
    z = jnp.zeros_like(w3)
    return jnp.concatenate([jnp.concatenate([w3, z], 2), jnp.concatenate([z, w3], 2)], 1).astype(BF16)


def kernel(x_prompt, x_sample, cache_a_k, cache_a_v, cache_b_k, cache_b_v, cache_c_kc, cache_c_vc, cache_c_ks, cache_c_vs, state_c_kw, state_c_vw, cache_d_k, cache_d_v, cache_d_logf, page_table, w_in_0, w_out_0, ln_g_0, ln_b_0, w_in_1, lam_q1_1, lam_k1_1, lam_q2_1, lam_k2_1, subln_g_1, w_out_1, ln_g_1, ln_b_1, w_in_2, w_cmp_k_2, w_cmp_v_2, w_out_2, ln_g_2, ln_b_2, w_in_3, b_f_3, w_out_3, ln_g_3, ln_b_3):
    bp, sp, _ = x_prompt.shape
    bs = x_sample.shape[0]
    assert x_sample.shape[1] == 1 and sp % C_WINDOW == 0
    n_pool = cache_a_k.shape[0]
    mp = bp * sp
    tm = 512
    xp = x_prompt.reshape(mp, D_MODEL)
    xs = x_sample.reshape(bs, D_MODEL)
    heads = lambda a, b, l, h: a.reshape(b, l, h, -1)
    split_slopes = lambda sl: jnp.stack(_split3(sl * LOG2E), axis=1)
    vcols = slice(2 * D_MODEL, 3 * D_MODEL)
    rows_last = lambda c: c.transpose(0, 2, 3, 1)
    kt_view = lambda c: rows_last(c).reshape(n_pool, D_MODEL, PAGE)

    w = w_in_0.astype(BF16)
    slopes16 = _alibi_slopes(N_HEADS)
    sl3_16 = split_slopes(slopes16)
    pp = _matmul(xp, w, tm)
    ps = _matmul(xs, w, bs)
    a_k_p, a_v_p = heads(pp[:, 1024:2048], bp, sp, 16), heads(pp[:, 2048:3072], bp, sp, 16)
    a_k_s, a_v_s = heads(ps[:, 1024:2048], bs, 1, 16), heads(ps[:, 2048:3072], bs, 1, 16)
    qa, ka = _prep_moba(pp, bp, sl3_16, _block_mean(pp, bp, 1))
    o_p = _flash_t(qa, ka, _matmul_t(xp, w[:, vcols].T, tm), bp, fin="plain")
    o_s = _decode(ps.reshape(bs, 1, -1), None, page_table, kt_view(cache_a_k), kt_view(cache_a_v), "moba",
                  [slopes16.reshape(-1, 1)])
    w_o = w_out_0.astype(BF16)
    xp = _out_ln(o_p, pp, 3, xp, w_o, ln_g_0, ln_b_0, tm)
    xs = _out_ln(o_s, ps, 3, xs, w_o, ln_g_0, ln_b_0, bs)

    lam_init = 0.8 - 0.6 * math.exp(-0.3 * 1)
    w = w_in_1.astype(BF16)
    slopes8 = jnp.repeat(_alibi_slopes(B_HEADS), 2)
    lamv = jnp.stack([lam_q1_1, lam_k1_1, lam_q2_1, lam_k2_1])
    pp = _matmul(xp, w, tm)
    ps = _matmul(xs, w, bs)
    b_k_p, b_v_p = heads(pp[:, 1024:2048], bp, sp, 16), heads(pp[:, 2048:3072], bp, sp, 8)
    b_k_s, b_v_s = heads(ps[:, 1024:2048], bs, 1, 16), heads(ps[:, 2048:3072], bs, 1, 8)
    qa, ka = _prep(pp, bp, "diff", 1, [split_slopes(slopes8)], tm)
    o_p = _flash_t(qa, ka, _matmul_t(xp, w[:, vcols].T, tm), bp, fin="diff",
                   extras=(lamv, subln_g_1.reshape(1, -1)), lam_init=lam_init)
    o_s = _decode(ps.reshape(bs, 1, -1), jnp.repeat(ps[:, vcols].reshape(bs, B_HEADS, LANES), 2, axis=1), page_table,
                  kt_view(cache_b_k), cache_b_v.reshape(n_pool, PAGE * B_HEADS, LANES), "diff",
                  [slopes8.reshape(-1, 1), lamv, subln_g_1.reshape(1, -1)], lam_init)
    w_o = w_out_1.astype(BF16)
    xp = _out_ln(o_p, pp, 3, xp, w_o, ln_g_1, ln_b_1, tm)
    xs = _out_ln(o_s, ps, 3, xs, w_o, ln_g_1, ln_b_1, bs)

    kv0, z0, g0 = D_MODEL, D_MODEL + 6 * LANES, 2 * D_MODEL + 6 * LANES
    w = _pad_cols(jnp.concatenate([w_in_2[:, :kv0], w_in_2[:, z0:g0], w_in_2[:, kv0:z0], w_in_2[:, g0:]], 1),
                  C_WIDTH).astype(BF16)
    w2k, w2v = _block_diag_cmp(w_cmp_k_2), _block_diag_cmp(w_cmp_v_2)
    pp = _matmul(xp, w, tm)
    ps = _matmul(xs, w, bs)
    kvcol = lambda p, c: p[:, (C_KV_COL + c) * LANES:(C_KV_COL + c + 1) * LANES]
    c_p = [heads(kvcol(pp, c), bp, sp, 2) for c in range(6)]
    c_s = [heads(kvcol(ps, c), bs, 1, 2) for c in range(6)]
    keep = min(C_WINDOW, sp)
    ck, cv = _compress_prompt(pp, bp, w2k, w2v)
    o_p, sel = _nsa_cmp(pp, bp, slopes16, ck, cv, 256)
    qa, ka = _prep(pp, bp, "sel", C_KV_COL + 2, [sl3_16, sel], tm)
    blocks = tm // C_SEL_BLOCK
    live = (sel.reshape(bp, sp // tm, tm, C_GROUPS, LANES // blocks, blocks).max((2, 3, 5)) > 0.5)
    o_p = _flash_t(qa, ka, _matmul_t(xp, kvcol(w, 3).T, tm), bp, fin="nsa", extras=(pp, o_p), gate_idx=1, live=live)
    qa, ka = _prep(pp, bp, "win", C_KV_COL + 4, [sl3_16], tm)
    o_p = _flash_t(qa, ka, _matmul_t(xp, kvcol(w, 5).T, tm), bp, fin="nsa", band=True, extras=(pp, o_p),
                   gate_idx=2)
    ck_pool = _compress_pool(cache_c_kc, w_cmp_k_2)
    cv_pool = _compress_pool(cache_c_vc, w_cmp_v_2)
    q_s = ps[:, :D_MODEL].reshape(bs, N_HEADS, 1, HEAD_DIM)
    in_group = (jnp.arange(N_HEADS) // C_RATIO)[:, None] == jnp.arange(C_GROUPS)[None, :]
    qbd = jnp.where(in_group[None, :, :, None], q_s, 0.0).reshape(bs, N_HEADS, LANES)
    gates_s = ps[:, C_GATE_COL * LANES:C_GATE_COL * LANES + 3 * N_HEADS].reshape(bs, N_HEADS, 3)
    wb = state_c_kw.shape[1]
    o16 = _nsa_decode(ps.reshape(bs, 1, -1), qbd, gates_s, slopes16.reshape(-1, 1), page_table, ck_pool, cv_pool,
                      rows_last(cache_c_ks).reshape(n_pool, LANES, PAGE), rows_last(cache_c_vs).reshape(n_pool, LANES, PAGE),
                      state_c_kw.reshape(bs, wb, LANES),
                      state_c_vw.reshape(bs, wb, LANES), w2k[0], w2v[0])
    o16 = o16.reshape(bs, N_HEADS, C_GROUPS, HEAD_DIM)
    o_s = jnp.where(in_group[None, :, :, None], o16, 0.0).sum(2).reshape(bs, D_MODEL)
    c_kw_s = jnp.concatenate([state_c_kw, c_s[4]], 1)[:, -min(C_WINDOW, wb + 1):]
    c_vw_s = jnp.concatenate([state_c_vw, c_s[5]], 1)[:, -min(C_WINDOW, wb + 1):]
    w_o = w_out_2.astype(BF16)
    xp = _out_ln(o_p, pp, C_Z_COL, xp, w_o, ln_g_2, ln_b_2, tm)
    xs = _out_ln(o_s, ps, C_Z_COL, xs, w_o, ln_g_2, ln_b_2, bs)

    f_col = 4 * D_MODEL // LANES
    w = _pad_cols(w_in_3, 4 * D_MODEL + LANES).astype(BF16)
    pp = _matmul(xp, w, tm)
    ps = _matmul(xs, w, bs)
    d_k_p, d_v_p = heads(pp[:, 1024:2048], bp, sp, 16), heads(pp[:, 2048:3072], bp, sp, 16)
    d_k_s, d_v_s = heads(ps[:, 1024:2048], bs, 1, 16), heads(ps[:, 2048:3072], bs, 1, 16)
    lf_p = _logf(pp, f_col, b_f_3, tm).reshape(bp, sp, N_HEADS)
    lf_s = _logf(ps, f_col, b_f_3, bs).reshape(bs, 1, N_HEADS)
    c_p3 = _cumsum_lanes(lf_p.transpose(0, 2, 1), tm).transpose(0, 2, 1).reshape(mp, N_HEADS)
    qa, ka = _prep(pp, bp, "fox", 1, [c_p3], tm)
    o_p = _flash_t(qa, ka, _matmul_t(xp, w[:, vcols].T, tm), bp, fin="plain")
    o_s = _decode(ps.reshape(bs, 1, -1), None, page_table, kt_view(cache_d_k), kt_view(cache_d_v), "fox",
                  [cache_d_logf.transpose(0, 2, 1), lf_s.reshape(bs, N_HEADS, 1)])
    w_o = w_out_3.astype(BF16)
    xp = _out_ln(o_p, pp, 3, xp, w_o, ln_g_3, ln_b_3, tm)
    xs = _out_ln(o_s, ps, 3, xs, w_o, ln_g_3, ln_b_3, bs)

    return (xp.reshape(bp, sp, D_MODEL), xs.reshape(bs, 1, D_MODEL),
            a_k_p, a_v_p, a_k_s, a_v_s, b_k_p, b_v_p, b_k_s, b_v_s,
            c_p[0], c_p[1], c_p[2], c_p[3], c_p[4][:, -keep:], c_p[5][:, -keep:],
            c_s[0], c_s[1], c_s[2], c_s[3], c_kw_s, c_vw_s,
            d_k_p, d_v_p, lf_p, d_k_s, d_v_s, lf_s)
```

```python
import functools
import math

import numpy as np
import jax
import jax.numpy as jnp
from jax import lax
from jax.experimental import pallas as pl
from jax.experimental.pallas import tpu as pltpu

F32 = jnp.float32
BF16 = jnp.bfloat16
HIGHEST = lax.Precision.HIGHEST

D_MODEL = 1024
HEAD_DIM = 64
N_HEADS = 16
LANES = 128
SCALE = HEAD_DIM ** -0.5
PAGE = 128
DEPTH = 4
ALPHA = (2 * DEPTH) ** 0.25
LN_EPS = 1e-5
NEG = -1e30
TINY = 1e-30
A_BLOCK = 256
A_TOPK = 3
B_HEADS = 8
C_GROUPS = 2
C_RATIO = 8
C_CMP_BLOCK = 32
C_SEL_BLOCK = 64
C_TOPK = 4
C_WINDOW = 512
PAGES_PER_STEP = 8
NSA_PAGES_PER_STEP = 4
VMEM_LIMIT = 56 * 1024 * 1024


def _dot_nt(a, b):
    return lax.dot_general(a, b, (((1,), (1,)), ((), ())), preferred_element_type=F32)


def _dot(a, b):
    return jnp.dot(a, b, preferred_element_type=F32)


def _iota(shape, dim):
    return lax.broadcasted_iota(jnp.int32, shape, dim)


def _params(*sem):
    return pltpu.CompilerParams(dimension_semantics=sem, vmem_limit_bytes=VMEM_LIMIT)


def _alibi_slopes(n):
    return jnp.exp2(-8.0 * jnp.arange(1, n + 1, dtype=F32) / n)


def _smem():
    return pl.BlockSpec(memory_space=pltpu.SMEM)


def _mm_kernel(x_ref, w_ref, o_ref):
    o_ref[...] = _dot(x_ref[...].astype(BF16), w_ref[...])


def _pick_tn(n):
    best = LANES
    for t in range(LANES, 1536 + 1, LANES):
        if n % t == 0:
            best = t
    return best


def _matmul(x, w, tm):
    m, k = x.shape
    n = w.shape[1]
    tm = min(tm, m)
    assert m % tm == 0
    tn = _pick_tn(n)
    return pl.pallas_call(
        _mm_kernel,
        grid=(m // tm, n // tn),
        in_specs=[pl.BlockSpec((tm, k), lambda i, j: (i, 0)),
                  pl.BlockSpec((k, tn), lambda i, j: (0, j))],
        out_specs=pl.BlockSpec((tm, tn), lambda i, j: (i, j)),
        out_shape=jax.ShapeDtypeStruct((m, n), F32),
        compiler_params=_params("parallel", "arbitrary"),
    )(x, w)


def _vt_kernel(x_ref, w_ref, *o_refs, want_bf16, want_f32):
    r = _dot_nt(w_ref[...], x_ref[...].astype(BF16))
    if want_bf16:
        o_refs[0][...] = r.astype(BF16)
    if want_f32:
        o_refs[-1][0] = r


def _matmul_t(x, wt, tm, batch, want_bf16=True, want_f32=False):
    m, k = x.shape
    n = wt.shape[0]
    nt = m // batch // tm
    out_specs, out_shape = [], []
    if want_bf16:
        out_specs.append(pl.BlockSpec((n, tm), lambda i: (0, i)))
        out_shape.append(jax.ShapeDtypeStruct((n, m), BF16))
    if want_f32:
        out_specs.append(pl.BlockSpec((1, n, tm), lambda i: (i // nt, 0, i % nt)))
        out_shape.append(jax.ShapeDtypeStruct((batch, n, m // batch), F32))
    return pl.pallas_call(
        functools.partial(_vt_kernel, want_bf16=want_bf16, want_f32=want_f32),
        grid=(m // tm,),
        in_specs=[pl.BlockSpec((tm, k), lambda i: (i, 0)), pl.BlockSpec((n, k), lambda i: (0, 0))],
        out_specs=out_specs,
        out_shape=out_shape,
        compiler_params=_params("parallel"),
    )(x, wt)


def _out_ln_kernel(o_ref, z_ref, x_ref, w_ref, g_ref, b_ref, y_ref):
    z = z_ref[...]
    a = (o_ref[...] * (z * jax.nn.sigmoid(z))).astype(BF16)
    h = ALPHA * x_ref[...] + _dot(a, w_ref[...])
    hc = h - jnp.mean(h, axis=-1, keepdims=True)
    var = jnp.mean(hc * hc, axis=-1, keepdims=True)
    y_ref[...] = hc * lax.rsqrt(var + LN_EPS) * g_ref[...] + b_ref[...]


def _out_ln(o, p, z_col, x, w, g, b, tm):
    m = x.shape[0]
    row = lambda i: (i, 0)
    return pl.pallas_call(
        _out_ln_kernel,
        grid=(m // tm,),
        in_specs=[pl.BlockSpec((tm, D_MODEL), row),
                  pl.BlockSpec((tm, D_MODEL), lambda i: (i, z_col)),
                  pl.BlockSpec((tm, D_MODEL), row),
                  pl.BlockSpec((D_MODEL, D_MODEL), lambda i: (0, 0)),
                  pl.BlockSpec((1, D_MODEL), lambda i: (0, 0)),
                  pl.BlockSpec((1, D_MODEL), lambda i: (0, 0))],
        out_specs=pl.BlockSpec((tm, D_MODEL), row),
        out_shape=jax.ShapeDtypeStruct((m, D_MODEL), F32),
        compiler_params=_params("parallel"),
    )(o, p, x, w, g.reshape(1, -1), b.reshape(1, -1))


def _topk_mask(s, cand, lane_f, k):
    s = jnp.where(cand, s, NEG)
    sel = jnp.zeros(s.shape, F32)
    for _ in range(k):
        mx = jnp.max(s, axis=1, keepdims=True)
        idx = jnp.min(jnp.where(s == mx, lane_f, 1e9), axis=1, keepdims=True)
        pick = lane_f == idx
        valid = jnp.where(mx > 0.5 * NEG, 1.0, 0.0)
        sel = jnp.where(pick, valid, sel)
        s = jnp.where(pick, -3e38, s)
    return sel


def _lambda(lam_ref, lam_init):
    a = lam_ref[...]
    return (jnp.exp(jnp.sum(a[0:1] * a[1:2], axis=1, keepdims=True))
            - jnp.exp(jnp.sum(a[2:3] * a[3:4], axis=1, keepdims=True)) + lam_init)


def _block_mean_kernel(k_ref, o_ref):
    n = pl.program_id(1)

    @pl.when(n == 0)
    def _():
        o_ref[...] = jnp.zeros(o_ref.shape, F32)

    o_ref[0, pl.ds(n, 1), :] = jnp.sum(k_ref[...], axis=0, keepdims=True) / A_BLOCK


def _block_mean(p, batch, k_col):
    nb = p.shape[0] // batch // A_BLOCK
    return pl.pallas_call(
        _block_mean_kernel,
        grid=(batch, nb),
        in_specs=[pl.BlockSpec((A_BLOCK, D_MODEL), lambda b, n: (b * nb + n, k_col))],
        out_specs=pl.BlockSpec((1, LANES, D_MODEL), lambda b, n: (b, 0, 0)),
        out_shape=jax.ShapeDtypeStruct((batch, LANES, D_MODEL), F32),
        compiler_params=_params("parallel", "arbitrary"),
    )(p)


LOG2E = math.log2(math.e)
AUX0 = HEAD_DIM
SEL0 = 96
Q_SCALE = SCALE * LOG2E


def _split3(x):
    hi = x.astype(BF16).astype(F32)
    mid = (x - hi).astype(BF16).astype(F32)
    return hi, mid, x - hi - mid


def _lane_pick(lane, base, vals):
    out = jnp.zeros(lane.shape, F32)
    for idx, v in enumerate(vals):
        out = jnp.where(lane == base + idx, v, out)
    return out


def _slope_lanes(lane, sl_ref, h):
    parts = [sl_ref[h, c] for c in range(3)]
    return _lane_pick(lane, AUX0, parts + parts)


def _pos_lanes(lane, kpos):
    hi = kpos.astype(BF16).astype(F32)
    lo = kpos - hi
    return _lane_pick(lane, AUX0, [hi, hi, hi, lo, lo, lo])


def _head_cols(ref, h):
    x = ref[:, (h // 2) * LANES:(h // 2 + 1) * LANES]
    return x if h % 2 == 0 else pltpu.roll(x, HEAD_DIM, 1)


def _prep_moba_kernel(sl_ref, q_ref, k_ref, bm_ref, qa_ref, ka_ref, *, tm):
    i = pl.program_id(1)
    lane = _iota((tm, LANES), 1)
    lane_f = lane.astype(F32)
    low = lane < HEAD_DIM
    kpos = (i * tm + _iota((tm, LANES), 0)).astype(F32)
    k_aux = jnp.where(lane >= SEL0, jnp.where(lane == SEL0 + i, 1.0, 0.0), _pos_lanes(lane, kpos))
    bm_low = _iota((LANES, LANES), 1) < HEAD_DIM
    for h in range(N_HEADS):
        cs = slice(h * LANES, (h + 1) * LANES)
        q = _head_cols(q_ref, h)
        bm = bm_ref[0, :, (h // 2) * LANES:(h // 2 + 1) * LANES]
        bme = jnp.where(bm_low if h % 2 == 0 else ~bm_low, bm, 0.0).astype(BF16)
        qsel = q_ref[:, (h // 2) * LANES:(h // 2 + 1) * LANES].astype(BF16)
        sel = _topk_mask(_dot_nt(qsel, bme), lane < i, lane_f, A_TOPK)
        penalty = pltpu.roll(jnp.where((sel > 0.5) | (lane == i), 0.0, NEG), SEL0, 1)
        q_aux = jnp.where(lane >= SEL0, penalty, _slope_lanes(lane, sl_ref, h))
        qa_ref[:, cs] = jnp.where(low, q * Q_SCALE, q_aux).astype(BF16)
        ka_ref[:, cs] = jnp.where(low, _head_cols(k_ref, h), k_aux).astype(BF16)


def _prep_moba(p, batch, sl3, bm):
    tm = A_BLOCK
    nt = p.shape[0] // batch // tm
    assert nt <= LANES - SEL0
    row = lambda col: pl.BlockSpec((tm, D_MODEL), lambda b, i: (b * nt + i, col))
    aug = pl.BlockSpec((tm, N_HEADS * LANES), lambda b, i: (b * nt + i, 0))
    shape = jax.ShapeDtypeStruct((p.shape[0], N_HEADS * LANES), BF16)
    return pl.pallas_call(
        functools.partial(_prep_moba_kernel, tm=tm),
        grid=(batch, nt),
        in_specs=[_smem(), row(0), row(1), pl.BlockSpec((1, LANES, D_MODEL), lambda b, i: (b, 0, 0))],
        out_specs=[aug, aug],
        out_shape=[shape, shape],
        compiler_params=_params("parallel", "parallel"),
    )(sl3, p, p, bm)


def _prep_kernel(*refs, tm, mode):
    if mode == "diff":
        sl_ref, q_ref, k_ref, qa_ref, ka_ref = refs
    elif mode == "fox":
        q_ref, k_ref, c_ref, qa_ref, ka_ref = refs
    elif mode == "win":
        sl_ref, q_ref, k_ref, qa_ref, ka_ref = refs
    else:
        sl_ref, q_ref, k_ref, sel_ref, qa_ref, ka_ref = refs
    i = pl.program_id(1)
    lane = _iota((tm, LANES), 1)
    low = lane < HEAD_DIM
    kpos = (i * tm + _iota((tm, LANES), 0)).astype(F32)
    cw = 2 * LANES if mode == "sel" else LANES
    for h in range(N_HEADS):
        if mode == "fox":
            q_aux = _lane_pick(lane, AUX0, [-1.0, -1.0, -1.0])
        else:
            q_aux = _slope_lanes(lane, sl_ref, h)
        qa_ref[:, h * cw:h * cw + LANES] = jnp.where(low, _head_cols(q_ref, h) * Q_SCALE, q_aux).astype(BF16)
        if mode == "sel":
            g = h // C_RATIO
            picked = sel_ref[:, g * LANES:(g + 1) * LANES]
            qa_ref[:, h * cw + LANES:(h + 1) * cw] = jnp.where(picked > 0.5, 0.0, NEG).astype(BF16)
        if mode == "diff":
            ka_ref[:, h * cw:(h + 1) * cw] = jnp.where(low, _head_cols(k_ref, h), _pos_lanes(lane, kpos)).astype(BF16)
        if mode == "fox":
            c = jnp.broadcast_to(c_ref[:, h:h + 1], (tm, LANES)) * LOG2E
            ka_ref[:, h * cw:(h + 1) * cw] = jnp.where(low, _head_cols(k_ref, h), _lane_pick(lane, AUX0, _split3(c))).astype(BF16)
    if mode in ("win", "sel"):
        for g in range(C_GROUPS):
            ka_ref[:, g * cw:g * cw + LANES] = jnp.where(low, _head_cols(k_ref, g), _pos_lanes(lane, kpos)).astype(BF16)
            if mode == "sel":
                own = (i * tm + _iota((tm, LANES), 0)) // C_SEL_BLOCK
                ka_ref[:, g * cw + LANES:(g + 1) * cw] = jnp.where(lane == own, 1.0, 0.0).astype(BF16)


def _prep(p, batch, mode, k_col, extras, tm):
    nt = p.shape[0] // batch // tm
    cw = 2 * LANES if mode == "sel" else LANES
    nk = C_GROUPS if mode in ("win", "sel") else N_HEADS
    kw = LANES if mode in ("win", "sel") else D_MODEL
    row = lambda width, col: pl.BlockSpec((tm, width), lambda b, i: (b * nt + i, col))
    in_specs = ([] if mode == "fox" else [_smem()]) + [row(D_MODEL, 0), row(kw, k_col)]
    if mode == "fox":
        in_specs.append(row(N_HEADS, 0))
    if mode == "sel":
        assert extras[-1].shape[1] == C_GROUPS * LANES
        in_specs.append(row(C_GROUPS * LANES, 0))
    return pl.pallas_call(
        functools.partial(_prep_kernel, tm=tm, mode=mode),
        grid=(batch, nt),
        in_specs=in_specs,
        out_specs=[row(N_HEADS * cw, 0), row(nk * cw, 0)],
        out_shape=[jax.ShapeDtypeStruct((p.shape[0], N_HEADS * cw), BF16),
                   jax.ShapeDtypeStruct((p.shape[0], nk * cw), BF16)],
        compiler_params=_params("parallel", "parallel"),
    )(*(extras[:1] if mode != "fox" else []), p, p, *(extras if mode == "fox" else extras[1:]))


def _sweep_tables(nq, band):
    it, jt, ft, mt = [], [], [], []
    for i in range(nq):
        js = [j for j in (i - 1, i) if j >= 0] if band else list(range(i + 1))
        for n, j in enumerate(js):
            it.append(i)
            jt.append(j)
            ft.append(1 if n == 0 else 0)
            mt.append(1 if j == i else (2 if band else 0))
    return [np.asarray(a, np.int32) for a in (it, jt, ft, mt)]


def _flash_t_kernel(it_ref, jt_ref, ft_ref, mt_ref, *rest, t, cw, kdiv, vdiv, dv, fin, modes, lam_init, gate_idx,
                    has_live):
    if has_live:
        live_ref, rest = rest[0], rest[1:]
    q_ref, k_ref, v_ref, rest = rest[0], rest[1], rest[2], rest[3:]
    if fin == "diff":
        lam_ref, g_ref, o_ref, m_scr, l_scr, acc_scr = rest
    elif fin == "nsa":
        g_ref, prev_ref, o_ref, m_scr, l_scr, acc_scr = rest
    else:
        o_ref, m_scr, l_scr, acc_scr = rest
    st = pl.program_id(1)

    @pl.when(ft_ref[st] == 1)
    def _():
        m_scr[...] = jnp.full(m_scr.shape, NEG, F32)
        l_scr[...] = jnp.zeros(l_scr.shape, F32)
        acc_scr[...] = jnp.zeros(acc_scr.shape, F32)

    def tile(mode):
        if mode:
            diff = _iota((t, t), 0) - _iota((t, t), 1)
            allowed = diff <= 0 if mode == 1 else diff >= 0

        def scores(h):
            kh = k_ref[:, (h // kdiv) * cw:(h // kdiv + 1) * cw]
            return _dot_nt(kh, q_ref[:, h * cw:(h + 1) * cw])

        s_next = scores(0)
        for h in range(N_HEADS):
            s = s_next
            if h + 1 < N_HEADS:
                s_next = scores(h + 1)
            if mode:
                s = jnp.where(allowed, s, NEG)
            m_prev = m_scr[h:h + 1, :]
            m_new = jnp.maximum(m_prev, jnp.max(s, axis=0, keepdims=True))
            p = jnp.exp2(s - m_new)
            corr = jnp.exp2(m_prev - m_new)
            l_scr[h:h + 1, :] = corr * l_scr[h:h + 1, :] + jnp.sum(p, axis=0, keepdims=True)
            m_scr[h:h + 1, :] = m_new
            rows = slice(h * dv, (h + 1) * dv)
            vh = v_ref[(h // vdiv) * dv:(h // vdiv + 1) * dv, :]
            acc_scr[rows, :] = acc_scr[rows, :] * corr + _dot(vh, p.astype(BF16))

    for mode in modes:
        run = mt_ref[st] == mode
        if has_live and mode == 0:
            run = run & (live_ref[pl.program_id(0) * pl.num_programs(1) + st] != 0)
        pl.when(run)(functools.partial(tile, mode))

    @pl.when(it_ref[st] == jt_ref[st])
    def _():
        lo = _iota((t, LANES), 1) < HEAD_DIM
        for hp in range(N_HEADS // 2):
            cs = slice(hp * LANES, (hp + 1) * LANES)
            if fin == "diff":
                lam = _lambda(lam_ref, lam_init)
                o = (acc_scr[2 * hp * dv:(2 * hp + 1) * dv, :] / l_scr[2 * hp:2 * hp + 1, :]
                     - lam * (acc_scr[(2 * hp + 1) * dv:(2 * hp + 2) * dv, :] / l_scr[2 * hp + 1:2 * hp + 2, :]))
                o = o * lax.rsqrt(jnp.mean(o * o, axis=0, keepdims=True) + LN_EPS)
                o_ref[:, cs] = o.T * g_ref[...] * (1.0 - lam_init)
            else:
                o = jnp.concatenate(
                    [acc_scr[(2 * hp + e) * dv:(2 * hp + e + 1) * dv, :] / l_scr[2 * hp + e:2 * hp + e + 1, :]
                     for e in range(2)], axis=0).T
                if fin == "nsa":
                    g0 = jax.nn.sigmoid(g_ref[:, 6 * hp + gate_idx:6 * hp + gate_idx + 1])
                    g1 = jax.nn.sigmoid(g_ref[:, 6 * hp + 3 + gate_idx:6 * hp + 4 + gate_idx])
                    o = prev_ref[:, cs] + o * jnp.where(lo, g0, g1)
                o_ref[:, cs] = o


def _flash_t(qa, ka, vt, batch, *, fin, band=False, extras=(), lam_init=0.0, gate_idx=0, live=None, t=512):
    m = qa.shape[0]
    nq = m // batch // t
    cw = qa.shape[1] // N_HEADS
    kdiv = N_HEADS // (ka.shape[1] // cw)
    dv = LANES if fin == "diff" else HEAD_DIM
    vdiv = N_HEADS // (vt.shape[0] // dv)
    tabs = _sweep_tables(nq, band)
    if live is not None:
        tabs.append(live[:, tabs[0], tabs[1]].reshape(-1).astype(jnp.int32))
    modes = (1, 2) if band else (0, 1)
    assert not band or t == C_WINDOW
    imap = lambda f: (lambda b, s, it_, jt_, *_: f(b, s, it_, jt_))
    qrow = lambda width: pl.BlockSpec((t, width), imap(lambda b, s, it_, jt_: (b * nq + it_[s], 0)))
    in_specs = [qrow(qa.shape[1]),
                pl.BlockSpec((t, ka.shape[1]), imap(lambda b, s, it_, jt_: (b * nq + jt_[s], 0))),
                pl.BlockSpec((vt.shape[0], t), imap(lambda b, s, it_, jt_: (0, b * nq + jt_[s])))]
    if fin == "diff":
        in_specs += [pl.BlockSpec((4, HEAD_DIM), imap(lambda b, s, it_, jt_: (0, 0))),
                     pl.BlockSpec((1, LANES), imap(lambda b, s, it_, jt_: (0, 0)))]
    if fin == "nsa":
        in_specs += [pl.BlockSpec((t, LANES), imap(lambda b, s, it_, jt_: (b * nq + it_[s], C_GATE_COL))),
                     qrow(D_MODEL)]
    return pl.pallas_call(
        functools.partial(_flash_t_kernel, t=t, cw=cw, kdiv=kdiv, vdiv=vdiv, dv=dv, fin=fin, modes=modes,
                          lam_init=lam_init, gate_idx=gate_idx, has_live=live is not None),
        grid_spec=pltpu.PrefetchScalarGridSpec(
            num_scalar_prefetch=len(tabs),
            grid=(batch, len(tabs[0])),
            in_specs=in_specs,
            out_specs=qrow(D_MODEL),
            scratch_shapes=[pltpu.VMEM((N_HEADS, t), F32), pltpu.VMEM((N_HEADS, t), F32),
                            pltpu.VMEM((N_HEADS * dv, t), F32)]),
        out_shape=jax.ShapeDtypeStruct((m, D_MODEL), F32),
        compiler_params=_params("parallel", "arbitrary"),
    )(*tabs, qa, ka, vt, *extras)


def _logf_kernel(f_ref, b_ref, o_ref):
    x = f_ref[:, 0:N_HEADS] + b_ref[...]
    o_ref[...] = jnp.minimum(x, 0.0) - jnp.log1p(jnp.exp(-jnp.abs(x)))


def _logf(p, f_col, b_f, tm):
    m = p.shape[0]
    return pl.pallas_call(
        _logf_kernel,
        grid=(m // tm,),
        in_specs=[pl.BlockSpec((tm, LANES), lambda i: (i, f_col)),
                  pl.BlockSpec((1, N_HEADS), lambda i: (0, 0))],
        out_specs=pl.BlockSpec((tm, N_HEADS), lambda i: (i, 0)),
        out_shape=jax.ShapeDtypeStruct((m, N_HEADS), F32),
        compiler_params=_params("parallel"),
    )(p, b_f.reshape(1, -1))


def _cumsum_kernel(x_ref, o_ref, carry_scr, *, t):
    @pl.when(pl.program_id(1) == 0)
    def _():
        carry_scr[...] = jnp.zeros(carry_scr.shape, F32)

    x = x_ref[0]
    tri = (_iota((t, t), 0) <= _iota((t, t), 1)).astype(F32)
    c = jnp.dot(x, tri, precision=HIGHEST, preferred_element_type=F32) + carry_scr[...]
    o_ref[0] = c
    carry_scr[...] = c[:, t - 1:t]


def _cumsum_lanes(x, t):
    b, h, s = x.shape
    return pl.pallas_call(
        functools.partial(_cumsum_kernel, t=t),
        grid=(b, s // t),
        in_specs=[pl.BlockSpec((1, h, t), lambda bi, n: (bi, 0, n))],
        out_specs=pl.BlockSpec((1, h, t), lambda bi, n: (bi, 0, n)),
        out_shape=jax.ShapeDtypeStruct(x.shape, F32),
        scratch_shapes=[pltpu.VMEM((h, 1), F32)],
        compiler_params=_params("parallel", "arbitrary"),
    )(x)


C_Z_COL = 1
C_KV_COL = 16
C_GATE_COL = 22
C_WIDTH = 3072


def _compress_kernel(kc_ref, vc_ref, wk_ref, wv_ref, ck_ref, cv_ref, *, nch):
    stride = 2 * C_CMP_BLOCK
    for src, w_ref, dst in ((kc_ref, wk_ref, ck_ref), (vc_ref, wv_ref, cv_ref)):
        for parity in range(2):
            acc = jnp.zeros((nch, LANES), F32)
            for tt in range(C_CMP_BLOCK):
                rows = src[pl.ds(parity * C_CMP_BLOCK + tt, nch, stride=stride), :]
                acc = acc + _dot(rows.astype(BF16), w_ref[tt])
            dst[0, parity * nch:(parity + 1) * nch, :] = acc


def _compress_prompt(p, batch, w2k, w2v):
    s = p.shape[0] // batch
    nch = s // (2 * C_CMP_BLOCK)
    col = lambda c: pl.BlockSpec((s, LANES), lambda b: (b, c))
    wspec = pl.BlockSpec((C_CMP_BLOCK, LANES, LANES), lambda b: (0, 0, 0))
    ospec = pl.BlockSpec((1, 2 * nch, LANES), lambda b: (b, 0, 0))
    oshape = jax.ShapeDtypeStruct((batch, 2 * nch, LANES), F32)
    return pl.pallas_call(
        functools.partial(_compress_kernel, nch=nch),
        grid=(batch,),
        in_specs=[col(C_KV_COL), col(C_KV_COL + 1), wspec, wspec],
        out_specs=[ospec, ospec],
        out_shape=[oshape, oshape],
        compiler_params=_params("parallel"),
    )(p, p, w2k, w2v)


def _compress_pool_kernel(x_ref, w_ref, o_ref):
    acc = jnp.zeros(o_ref.shape, F32)
    for d in range(HEAD_DIM):
        acc = acc + _dot(x_ref[:, d, :].astype(BF16), w_ref[d])
    o_ref[...] = acc


def _compress_pool(cache, w_cmp):
    n_pool = cache.shape[0]
    tokens = PAGE // C_CMP_BLOCK
    x = cache.transpose(0, 2, 3, 1).reshape(n_pool * C_GROUPS, HEAD_DIM, PAGE)
    r = jnp.arange(PAGE)
    in_token = (r[:, None] // C_CMP_BLOCK == jnp.arange(tokens)[None, :]).astype(w_cmp.dtype)
    w_rows = w_cmp.reshape(C_CMP_BLOCK, HEAD_DIM, HEAD_DIM)[r % C_CMP_BLOCK]
    wx = (w_rows[:, :, None, :] * in_token[:, None, :, None]).transpose(1, 0, 2, 3)
    wx = wx.reshape(HEAD_DIM, PAGE, tokens * HEAD_DIM).astype(BF16)
    tm = min(256, x.shape[0])
    assert x.shape[0] % tm == 0
    out = pl.pallas_call(
        _compress_pool_kernel,
        grid=(x.shape[0] // tm,),
        in_specs=[pl.BlockSpec((tm, HEAD_DIM, PAGE), lambda i: (i, 0, 0)),
                  pl.BlockSpec(wx.shape, lambda i: (0, 0, 0))],
        out_specs=pl.BlockSpec((tm, tokens * HEAD_DIM), lambda i: (i, 0)),
        out_shape=jax.ShapeDtypeStruct((x.shape[0], tokens * HEAD_DIM), F32),
        compiler_params=_params("parallel"),
    )(x, wx)
    return out.reshape(n_pool, C_GROUPS, tokens, HEAD_DIM).transpose(0, 2, 1, 3).reshape(n_pool, tokens, LANES)


def _group_halves(x, g):
    lo = _iota(x.shape, 1) < HEAD_DIM
    base = jnp.where(lo if g == 0 else ~lo, x, 0.0)
    other = pltpu.roll(base, HEAD_DIM, 1)
    pair = (base, other) if g == 0 else (other, base)
    return pair[0].astype(BF16), pair[1].astype(BF16)


def _nsa_cmp_kernel(sl_ref, q_ref, g_ref, ck_ref, cv_ref, o_ref, sel_ref, *, tq, nch):
    q0 = pl.program_id(1) * tq
    nl = 2 * nch
    lane = _iota((tq, nl), 1)
    qpos = q0 + _iota((tq, nl), 0)
    tok = jnp.where(lane < nch, 2 * lane, 2 * (lane - nch) + 1)
    endp = (tok + 1) * C_CMP_BLOCK - 1
    okc = endp <= qpos
    relc = (endp[0:1, :] - q0).astype(F32)
    lane_s = _iota((tq, nch), 1)
    qblk = (q0 + _iota((tq, nch), 0)) // C_SEL_BLOCK
    lane_sf = lane_s.astype(F32)
    for g in range(C_GROUPS):
        k_lo, k_hi = _group_halves(ck_ref[0], g)
        v_lo, v_hi = _group_halves(cv_ref[0], g)
        imp = jnp.zeros((tq, nl), F32)
        for hp in range(g * C_RATIO // 2, (g + 1) * C_RATIO // 2):
            cs = slice(hp * LANES, (hp + 1) * LANES)
            qp = (q_ref[:, cs] * SCALE).astype(BF16)
            o_pair = None
            for e in range(2):
                h = 2 * hp + e
                s = _dot_nt(qp, k_lo if e == 0 else k_hi) + sl_ref[h] * relc
                s = jnp.where(okc, s, NEG)
                pc = jnp.where(okc, jnp.exp(s - jnp.max(s, axis=1, keepdims=True)), 0.0)
                pc = pc / jnp.maximum(jnp.sum(pc, axis=1, keepdims=True), TINY)
                imp = imp + pc
                gate = jax.nn.sigmoid(g_ref[:, 3 * h:3 * h + 1])
                o = _dot(pc.astype(BF16), v_lo if e == 0 else v_hi) * gate
                o_pair = o if o_pair is None else o_pair + o
            o_ref[:, cs] = o_pair
        imp_sel = imp[:, 0:nch] + imp[:, nch:nl]
        sel = _topk_mask(imp_sel, lane_s < qblk, lane_sf, C_TOPK)
        sel = jnp.where(lane_s == qblk, 1.0, sel)
        if nch < LANES:
            sel = jnp.concatenate([sel, jnp.zeros((tq, LANES - nch), F32)], axis=1)
        sel_ref[:, g * LANES:(g + 1) * LANES] = sel


def _nsa_cmp(p, batch, slopes, ck, cv, tq):
    nq = p.shape[0] // batch // tq
    nch = ck.shape[1] // 2
    assert nch <= LANES
    row = lambda width, col: pl.BlockSpec((tq, width), lambda b, i: (b * nq + i, col))
    cspec = pl.BlockSpec((1, 2 * nch, LANES), lambda b, i: (b, 0, 0))
    return pl.pallas_call(
        functools.partial(_nsa_cmp_kernel, tq=tq, nch=nch),
        grid=(batch, nq),
        in_specs=[_smem(), row(D_MODEL, 0), row(LANES, C_GATE_COL), cspec, cspec],
        out_specs=[row(D_MODEL, 0), row(C_GROUPS * LANES, 0)],
        out_shape=[jax.ShapeDtypeStruct((p.shape[0], D_MODEL), F32),
                   jax.ShapeDtypeStruct((p.shape[0], C_GROUPS * LANES), F32)],
        compiler_params=_params("parallel", "parallel"),
    )(slopes, p, p, ck, cv)


def _decode_kernel(pt_ref, q_ref, kn_ref, vn_ref, *rest, n_pages, mode, lam_init):
    del pt_ref
    gp = PAGES_PER_STEP
    if mode == "moba":
        slc_ref, rest = rest[0], rest[1:]
    elif mode == "diff":
        slc_ref, lam_ref, g_ref, rest = rest[0], rest[1], rest[2], rest[3:]
    else:
        lfn_ref, lft_refs, rest = rest[0], rest[1:1 + gp], rest[1 + gp:]
    kt_refs, v_refs, rest = rest[:gp], rest[gp:2 * gp], rest[2 * gp:]
    if mode == "fox":
        o_ref, qbd_scr, st_scr, p_scr, pn_scr, acc_scr, lf_scr = rest
    else:
        o_ref, qbd_scr, st_scr, p_scr, pn_scr, acc_scr = rest
    s = pl.program_id(1)
    n_steps = n_pages // gp
    past = n_pages * PAGE
    row = _iota((N_HEADS, D_MODEL), 0)
    lane = _iota((N_HEADS, D_MODEL), 1)
    lane_b = _iota((N_HEADS, LANES), 1)

    @pl.when(s == 0)
    def _():
        qbd_scr[...] = jnp.where(lane // HEAD_DIM == row, jnp.broadcast_to(q_ref[0], (N_HEADS, D_MODEL)), 0.0)
        acc_scr[...] = jnp.zeros(acc_scr.shape, F32)

    @pl.when(s < n_steps)
    def _():
        qb = (qbd_scr[...] * SCALE).astype(BF16)
        for g in range(gp):
            off = pl.multiple_of((s * gp + g) * PAGE, PAGE)
            st_scr[:, pl.ds(off, PAGE)] = _dot(qb, kt_refs[g][0].astype(BF16))
            if mode == "fox":
                lf_scr[:, pl.ds(off, PAGE)] = lft_refs[g][0]

    @pl.when(s == n_steps - 1)
    def _():
        qf = qbd_scr[...]
        qk_new = jnp.sum(qf * jnp.broadcast_to(kn_ref[0], qf.shape), axis=1, keepdims=True)
        s_new = qk_new * SCALE
        if mode == "fox":
            c = lf_scr[...]
            kpos = _iota(c.shape, 1)
            shift = 1
            while shift < past:
                c = c + jnp.where(kpos >= shift, pltpu.roll(c, shift, 1), 0.0)
                shift *= 2
            s_all = st_scr[...] - c
            s_new = s_new - (c[:, past - 1:past] + lfn_ref[0])
        else:
            kpos = _iota((1, past), 1)
            s_all = st_scr[...] + slc_ref[...] * (kpos - past).astype(F32)
        if mode == "moba":
            sb = jnp.where(lane_b == past // A_BLOCK, qk_new / A_BLOCK, 0.0)
            for n in range(past // A_BLOCK):
                blk = jnp.sum(st_scr[:, n * A_BLOCK:(n + 1) * A_BLOCK], axis=1, keepdims=True) / (SCALE * A_BLOCK)
                sb = jnp.where(lane_b == n, blk, sb)
            sel = _topk_mask(sb, lane_b < past // A_BLOCK, lane_b.astype(F32), A_TOPK)
            st_scr[...] = s_all
            for n in range(past // A_BLOCK):
                cols = slice(n * A_BLOCK, (n + 1) * A_BLOCK)
                st_scr[:, cols] = st_scr[:, cols] + jnp.where(sel[:, n:n + 1] > 0.5, 0.0, NEG)
            s_all = st_scr[...]
        m = jnp.maximum(jnp.max(s_all, axis=1, keepdims=True), s_new)
        p = jnp.exp(s_all - m)
        pn = jnp.exp(s_new - m)
        l = jnp.sum(p, axis=1, keepdims=True) + pn
        p_scr[...] = p / l
        pn_scr[...] = pn / l

    @pl.when(s >= n_steps)
    def _():
        acc = acc_scr[...]
        for g in range(gp):
            off = pl.multiple_of(((s - n_steps) * gp + g) * PAGE, PAGE)
            pg = p_scr[:, pl.ds(off, PAGE)].astype(BF16)
            if mode == "diff":
                for h in range(B_HEADS):
                    vh = v_refs[g][0, pl.ds(h, PAGE, stride=B_HEADS), :]
                    acc = acc + jnp.where(_iota((N_HEADS, LANES), 0) // 2 == h, _dot(pg, vh.astype(BF16)), 0.0)
            else:
                acc = acc + _dot_nt(pg, v_refs[g][0].astype(BF16))
        acc_scr[...] = acc

    @pl.when(s == 2 * n_steps - 1)
    def _():
        acc = acc_scr[...]
        if mode == "diff":
            row_b = _iota((N_HEADS, LANES), 0)
            acc = acc + pn_scr[...] * vn_ref[0]
            signed = jnp.where(row_b % 2 == 0, acc, -_lambda(lam_ref, lam_init) * acc)
            o = signed + pltpu.roll(signed, N_HEADS - 1, 0)
            o = o * lax.rsqrt(jnp.mean(o * o, axis=1, keepdims=True) + LN_EPS)
            o_ref[0] = o * g_ref[...] * (1.0 - lam_init)
        else:
            acc = acc + pn_scr[...] * jnp.broadcast_to(vn_ref[0], (N_HEADS, D_MODEL))
            o_ref[0] = jnp.sum(jnp.where(lane // HEAD_DIM == row, acc, 0.0), axis=0, keepdims=True)


def _decode(ps3, vn, page_table, kt_pool, v_pool, mode, extras, lam_init=0.0):
    b = ps3.shape[0]
    n_pages = page_table.shape[1]
    gp = PAGES_PER_STEP
    assert n_pages % gp == 0 and (n_pages * PAGE) % A_BLOCK == 0
    n_steps = n_pages // gp
    past = n_pages * PAGE
    pt = page_table.reshape(-1)
    rowspec = lambda col: pl.BlockSpec((1, 1, D_MODEL), lambda bi, s, pt_: (bi, 0, col))
    per_b = lambda shape: pl.BlockSpec((1,) + shape, lambda bi, s, pt_: (bi,) + tuple(0 for _ in shape))
    full = lambda shape: pl.BlockSpec(shape, lambda bi, s, pt_: tuple(0 for _ in shape))
    kpage = lambda g: (lambda bi, s, pt_: (pt_[bi * n_pages + jnp.minimum(s, n_steps - 1) * gp + g], 0, 0))
    vpage = lambda g: (lambda bi, s, pt_: (pt_[bi * n_pages + jnp.maximum(s - n_steps, 0) * gp + g], 0, 0))
    kspecs = [pl.BlockSpec((1,) + kt_pool.shape[1:], kpage(g)) for g in range(gp)]
    vspecs = [pl.BlockSpec((1,) + v_pool.shape[1:], vpage(g)) for g in range(gp)]
    out_tile = (N_HEADS, LANES) if mode == "diff" else (1, D_MODEL)
    scratch = [pltpu.VMEM((N_HEADS, D_MODEL), F32), pltpu.VMEM((N_HEADS, past), F32),
               pltpu.VMEM((N_HEADS, past), F32), pltpu.VMEM((N_HEADS, 1), F32),
               pltpu.VMEM((N_HEADS, LANES if mode == "diff" else D_MODEL), F32)]
    args = list(extras)
    if mode == "moba":
        in_specs = [full((N_HEADS, 1))]
    elif mode == "diff":
        in_specs = [full((N_HEADS, 1)), full((4, HEAD_DIM)), full((1, LANES))]
    else:
        lft, lfn = extras
        in_specs = [per_b((N_HEADS, 1))] + [pl.BlockSpec((1, N_HEADS, PAGE), kpage(g)) for g in range(gp)]
        args = [lfn] + [lft] * gp
        scratch += [pltpu.VMEM((N_HEADS, past), F32)]
    vn_spec, vn_arg = (per_b(vn.shape[1:]), vn) if mode == "diff" else (rowspec(2), ps3)
    out = pl.pallas_call(
        functools.partial(_decode_kernel, n_pages=n_pages, mode=mode, lam_init=lam_init),
        grid_spec=pltpu.PrefetchScalarGridSpec(
            num_scalar_prefetch=1,
            grid=(b, 2 * n_steps),
            in_specs=[rowspec(0), rowspec(1), vn_spec] + in_specs + kspecs + vspecs,
            out_specs=per_b(out_tile),
            scratch_shapes=scratch),
        out_shape=jax.ShapeDtypeStruct((b,) + out_tile, F32),
        compiler_params=_params("parallel", "arbitrary"),
    )(pt, ps3, ps3, vn_arg, *args, *([kt_pool] * gp), *([v_pool] * gp))
    if mode == "diff":
        out = out[:, 0::2, :]
    return out.reshape(b, D_MODEL)


C_HALF = 64


def _softmax_new(s_all, s_new, v_all, v_new, v_transposed):
    m = jnp.maximum(jnp.max(s_all, axis=1, keepdims=True), s_new)
    p = jnp.exp(s_all - m)
    pn = jnp.exp(s_new - m)
    l = jnp.sum(p, axis=1, keepdims=True) + pn
    pv = _dot_nt(p.astype(BF16), v_all.astype(BF16)) if v_transposed else _dot(p.astype(BF16), v_all.astype(BF16))
    return (pv + pn * v_new) / l


def _nsa_decode_kernel(pt_ref, qbd_ref, g_ref, slc_ref, kcn_ref, vcn_ref, ksn_ref, vsn_ref, kwn_ref, vwn_ref,
                       kw_ref, vw_ref, wk0_ref, wv0_ref, *rest, n_pages):
    del pt_ref
    gp = NSA_PAGES_PER_STEP
    ckp_refs, cvp_refs, kst_refs, vst_refs = (rest[i * gp:(i + 1) * gp] for i in range(4))
    o_ref, ck_scr, cv_scr, kst_scr, vst_scr = rest[4 * gp:]
    s = pl.program_id(1)
    past = n_pages * PAGE
    per_page = PAGE // C_CMP_BLOCK // 2

    @pl.when(s == 0)
    def _():
        ck_scr[...] = jnp.zeros(ck_scr.shape, F32)
        cv_scr[...] = jnp.zeros(cv_scr.shape, F32)

    for g in range(gp):
        page = s * gp + g
        off = pl.multiple_of(page * PAGE, PAGE)
        kst_scr[:, pl.ds(off, PAGE)] = kst_refs[g][0]
        vst_scr[:, pl.ds(off, PAGE)] = vst_refs[g][0]
        for u in range(2 * per_page):
            dst = (u % 2) * C_HALF + per_page * page + u // 2
            ck_scr[pl.ds(dst, 1), :] = ckp_refs[g][0, u:u + 1, :]
            cv_scr[pl.ds(dst, 1), :] = cvp_refs[g][0, u:u + 1, :]

    @pl.when(s == n_pages // gp - 1)
    def _():
        qf = qbd_ref[0] * SCALE
        qb = qf.astype(BF16)
        slc = slc_ref[...]
        new_tok = per_page * n_pages
        ck_scr[new_tok:new_tok + 1, :] = _dot(jnp.broadcast_to(kcn_ref[0], (8, LANES)).astype(BF16), wk0_ref[...])[0:1]
        cv_scr[new_tok:new_tok + 1, :] = _dot(jnp.broadcast_to(vcn_ref[0], (8, LANES)).astype(BF16), wv0_ref[...])[0:1]
        lane = _iota((N_HEADS, LANES), 1)
        row = _iota((N_HEADS, LANES), 0)
        tok = jnp.where(lane < C_HALF, 2 * lane, 2 * (lane - C_HALF) + 1)
        endp = (tok + 1) * C_CMP_BLOCK - 1
        okc = endp <= past
        sc = _dot_nt(qb, ck_scr[...].astype(BF16)) + slc * (endp - past).astype(F32)
        sc = jnp.where(okc, sc, NEG)
        pc = jnp.where(okc, jnp.exp(sc - jnp.max(sc, axis=1, keepdims=True)), 0.0)
        pc = pc / jnp.maximum(jnp.sum(pc, axis=1, keepdims=True), TINY)
        o_cmp = _dot(pc.astype(BF16), cv_scr[...].astype(BF16))
        imp = jnp.where(row < C_RATIO, jnp.sum(pc[0:C_RATIO], axis=0, keepdims=True),
                        jnp.sum(pc[C_RATIO:N_HEADS], axis=0, keepdims=True))
        imp = imp + pltpu.roll(imp, C_HALF, 1)
        sel = _topk_mask(imp, lane < past // C_SEL_BLOCK, lane.astype(F32), C_TOPK)
        expand = (_iota((LANES, past), 0) == _iota((LANES, past), 1) // C_SEL_BLOCK).astype(BF16)
        picked = _dot(sel.astype(BF16), expand)
        kpos = _iota((1, past), 1)
        ss = _dot(qb, kst_scr[...].astype(BF16)) + slc * (kpos - past).astype(F32)
        ss = jnp.where(picked > 0.5, ss, NEG)
        ss_new = jnp.sum(qf * ksn_ref[0], axis=1, keepdims=True)
        o_sel = _softmax_new(ss, ss_new, vst_scr[...], vsn_ref[0], True)
        wb = kw_ref.shape[1]
        wpos = past - wb + _iota((1, wb), 1)
        okw = (past - wpos <= C_WINDOW) & (wpos >= 0)
        sw = _dot_nt(qb, kw_ref[0].astype(BF16)) + slc * (wpos - past).astype(F32)
        sw = jnp.where(okw, sw, NEG)
        sw_new = jnp.sum(qf * kwn_ref[0], axis=1, keepdims=True)
        o_win = _softmax_new(sw, sw_new, vw_ref[0], vwn_ref[0], False)
        gate = jax.nn.sigmoid(g_ref[0])
        o_ref[0] = gate[:, 0:1] * o_cmp + gate[:, 1:2] * o_sel + gate[:, 2:3] * o_win


def _nsa_decode(ps3, qbd, gates, slc, page_table, ck_pool, cv_pool, kst_pool, vst_pool, kw_buf, vw_buf, wk0, wv0):
    b = ps3.shape[0]
    n_pages = page_table.shape[1]
    gp = NSA_PAGES_PER_STEP
    past = n_pages * PAGE
    assert 2 * n_pages + 1 <= C_HALF and n_pages % gp == 0
    pt = page_table.reshape(-1)
    per_b = lambda shape: pl.BlockSpec((1,) + shape, lambda bi, s, pt_: (bi,) + tuple(0 for _ in shape))
    full = lambda shape: pl.BlockSpec(shape, lambda bi, s, pt_: tuple(0 for _ in shape))
    newrow = lambda col: pl.BlockSpec((1, 1, LANES), lambda bi, s, pt_: (bi, 0, col))
    paged = lambda rows: [pl.BlockSpec((1, rows, LANES), (lambda g: lambda bi, s, pt_: (pt_[bi * n_pages + s * gp + g], 0, 0))(g))
                          for g in range(gp)]
    tokens = PAGE // C_CMP_BLOCK
    return pl.pallas_call(
        functools.partial(_nsa_decode_kernel, n_pages=n_pages),
        grid_spec=pltpu.PrefetchScalarGridSpec(
            num_scalar_prefetch=1,
            grid=(b, n_pages // gp),
            in_specs=[per_b((N_HEADS, LANES)), per_b((N_HEADS, 3)), full((N_HEADS, 1))]
                     + [newrow(C_KV_COL + c) for c in range(6)]
                     + [per_b(kw_buf.shape[1:]), per_b(vw_buf.shape[1:]), full((LANES, LANES)), full((LANES, LANES))]
                     + paged(tokens) + paged(tokens) + paged(PAGE) + paged(PAGE),
            out_specs=per_b((N_HEADS, LANES)),
            scratch_shapes=[pltpu.VMEM((2 * C_HALF, LANES), F32), pltpu.VMEM((2 * C_HALF, LANES), F32),
                            pltpu.VMEM((LANES, past), F32), pltpu.VMEM((LANES, past), F32)]),
        out_shape=jax.ShapeDtypeStruct((b, N_HEADS, LANES), F32),
        compiler_params=_params("parallel", "arbitrary"),
    )(pt, qbd, gates, slc, ps3, ps3, ps3, ps3, ps3, ps3, kw_buf, vw_buf, wk0, wv0,
      *([ck_pool] * gp), *([cv_pool] * gp), *([kst_pool] * gp), *([vst_pool] * gp))


def _pad_cols(w, width):
    return jnp.pad(w, ((0, 0), (0, width - w.shape[1])))


def _block_diag_cmp(w):
    w3 = w.reshape(C_CMP_BLOCK, HEAD_DIM, HEAD_DIM)
    z = jnp.zeros_like(w3)
    return jnp.concatenate([jnp.concatenate([w3, z], 2), jnp.concatenate([z, w3], 2)], 1).astype(BF16)


def kernel(x_prompt, x_sample, cache_a_k, cache_a_v, cache_b_k, cache_b_v, cache_c_kc, cache_c_vc, cache_c_ks, cache_c_vs, state_c_kw, state_c_vw, cache_d_k, cache_d_v, cache_d_logf, page_table, w_in_0, w_out_0, ln_g_0, ln_b_0, w_in_1, lam_q1_1, lam_k1_1, lam_q2_1, lam_k2_1, subln_g_1, w_out_1, ln_g_1, ln_b_1, w_in_2, w_cmp_k_2, w_cmp_v_2, w_out_2, ln_g_2, ln_b_2, w_in_3, b_f_3, w_out_3, ln_g_3, ln_b_3):
    bp, sp, _ = x_prompt.shape
    bs = x_sample.shape[0]
    assert x_sample.shape[1] == 1 and sp % C_WINDOW == 0
    n_pool = cache_a_k.shape[0]
    mp = bp * sp
    tm = 512
    xp = x_prompt.reshape(mp, D_MODEL)
    xs = x_sample.reshape(bs, D_MODEL)
    heads = lambda a, b, l, h: a.reshape(b, l, h, -1)
    split_slopes = lambda sl: jnp.stack(_split3(sl * LOG2E), axis=1)
    kcols = slice(D_MODEL, 2 * D_MODEL)
    vcols = slice(2 * D_MODEL, 3 * D_MODEL)
    leaf = lambda t3: t3.reshape(bp, N_HEADS, HEAD_DIM, sp).transpose(0, 3, 1, 2)
    rows_last = lambda c: c.transpose(0, 2, 3, 1)
    kt_view = lambda c: rows_last(c).reshape(n_pool, D_MODEL, PAGE)

    w = w_in_0.astype(BF16)
    slopes16 = _alibi_slopes(N_HEADS)
    sl3_16 = split_slopes(slopes16)
    pp = _matmul(xp, w, 2 * tm)
    ps = _matmul(xs, w, bs)
    vt, vt32 = _matmul_t(xp, w[:, vcols].T, tm, bp, want_f32=True)
    a_k_p, a_v_p = leaf(_matmul_t(xp, w[:, kcols].T, tm, bp, want_bf16=False, want_f32=True)[0]), leaf(vt32)
    a_k_s, a_v_s = heads(ps[:, 1024:2048], bs, 1, 16), heads(ps[:, 2048:3072], bs, 1, 16)
    qa, ka = _prep_moba(pp, bp, sl3_16, _block_mean(pp, bp, 1))
    o_p = _flash_t(qa, ka, vt, bp, fin="plain")
    o_s = _decode(ps.reshape(bs, 1, -1), None, page_table, kt_view(cache_a_k), kt_view(cache_a_v), "moba",
                  [slopes16.reshape(-1, 1)])
    w_o = w_out_0.astype(BF16)
    xp = _out_ln(o_p, pp, 3, xp, w_o, ln_g_0, ln_b_0, tm)
    xs = _out_ln(o_s, ps, 3, xs, w_o, ln_g_0, ln_b_0, bs)

    lam_init = 0.8 - 0.6 * math.exp(-0.3 * 1)
    w = w_in_1.astype(BF16)
    slopes8 = jnp.repeat(_alibi_slopes(B_HEADS), 2)
    lamv = jnp.stack([lam_q1_1, lam_k1_1, lam_q2_1, lam_k2_1])
    pp = _matmul(xp, w, 2 * tm)
    ps = _matmul(xs, w, bs)
    b_k_p = leaf(_matmul_t(xp, w[:, kcols].T, tm, bp, want_bf16=False, want_f32=True)[0])
    b_v_p = heads(pp[:, 2048:3072], bp, sp, 8)
    b_k_s, b_v_s = heads(ps[:, 1024:2048], bs, 1, 16), heads(ps[:, 2048:3072], bs, 1, 8)
    qa, ka = _prep(pp, bp, "diff", 1, [split_slopes(slopes8)], tm)
    o_p = _flash_t(qa, ka, _matmul_t(xp, w[:, vcols].T, tm, bp)[0], bp, fin="diff",
                   extras=(lamv, subln_g_1.reshape(1, -1)), lam_init=lam_init)
    o_s = _decode(ps.reshape(bs, 1, -1), jnp.repeat(ps[:, vcols].reshape(bs, B_HEADS, LANES), 2, axis=1), page_table,
                  kt_view(cache_b_k), cache_b_v.reshape(n_pool, PAGE * B_HEADS, LANES), "diff",
                  [slopes8.reshape(-1, 1), lamv, subln_g_1.reshape(1, -1)], lam_init)
    w_o = w_out_1.astype(BF16)
    xp = _out_ln(o_p, pp, 3, xp, w_o, ln_g_1, ln_b_1, tm)
    xs = _out_ln(o_s, ps, 3, xs, w_o, ln_g_1, ln_b_1, bs)

    kv0, z0, g0 = D_MODEL, D_MODEL + 6 * LANES, 2 * D_MODEL + 6 * LANES
    w = _pad_cols(jnp.concatenate([w_in_2[:, :kv0], w_in_2[:, z0:g0], w_in_2[:, kv0:z0], w_in_2[:, g0:]], 1),
                  C_WIDTH).astype(BF16)
    w2k, w2v = _block_diag_cmp(w_cmp_k_2), _block_diag_cmp(w_cmp_v_2)
    pp = _matmul(xp, w, 2 * tm)
    ps = _matmul(xs, w, bs)
    kvcol = lambda p, c: p[:, (C_KV_COL + c) * LANES:(C_KV_COL + c + 1) * LANES]
    c_p = [heads(kvcol(pp, c), bp, sp, 2) for c in range(6)]
    c_s = [heads(kvcol(ps, c), bs, 1, 2) for c in range(6)]
    keep = min(C_WINDOW, sp)
    ck, cv = _compress_prompt(pp, bp, w2k, w2v)
    o_p, sel = _nsa_cmp(pp, bp, slopes16, ck, cv, 256)
    qa, ka = _prep(pp, bp, "sel", C_KV_COL + 2, [sl3_16, sel], tm)
    blocks = tm // C_SEL_BLOCK
    live = (sel.reshape(bp, sp // tm, tm, C_GROUPS, LANES // blocks, blocks).max((2, 3, 5)) > 0.5)
    o_p = _flash_t(qa, ka, _matmul_t(xp, kvcol(w, 3).T, tm, bp)[0], bp, fin="nsa", extras=(pp, o_p), gate_idx=1,
                   live=live)
    qa, ka = _prep(pp, bp, "win", C_KV_COL + 4, [sl3_16], tm)
    o_p = _flash_t(qa, ka, _matmul_t(xp, kvcol(w, 5).T, tm, bp)[0], bp, fin="nsa", band=True, extras=(pp, o_p),
                   gate_idx=2)
    ck_pool = _compress_pool(cache_c_kc, w_cmp_k_2)
    cv_pool = _compress_pool(cache_c_vc, w_cmp_v_2)
    q_s = ps[:, :D_MODEL].reshape(bs, N_HEADS, 1, HEAD_DIM)
    in_group = (jnp.arange(N_HEADS) // C_RATIO)[:, None] == jnp.arange(C_GROUPS)[None, :]
    qbd = jnp.where(in_group[None, :, :, None], q_s, 0.0).reshape(bs, N_HEADS, LANES)
    gates_s = ps[:, C_GATE_COL * LANES:C_GATE_COL * LANES + 3 * N_HEADS].reshape(bs, N_HEADS, 3)
    wb = state_c_kw.shape[1]
    o16 = _nsa_decode(ps.reshape(bs, 1, -1), qbd, gates_s, slopes16.reshape(-1, 1), page_table, ck_pool, cv_pool,
                      rows_last(cache_c_ks).reshape(n_pool, LANES, PAGE), rows_last(cache_c_vs).reshape(n_pool, LANES, PAGE),
                      state_c_kw.reshape(bs, wb, LANES),
                      state_c_vw.reshape(bs, wb, LANES), w2k[0], w2v[0])
    o16 = o16.reshape(bs, N_HEADS, C_GROUPS, HEAD_DIM)
    o_s = jnp.where(in_group[None, :, :, None], o16, 0.0).sum(2).reshape(bs, D_MODEL)
    c_kw_s = jnp.concatenate([state_c_kw, c_s[4]], 1)[:, -min(C_WINDOW, wb + 1):]
    c_vw_s = jnp.concatenate([state_c_vw, c_s[5]], 1)[:, -min(C_WINDOW, wb + 1):]
    w_o = w_out_2.astype(BF16)
    xp = _out_ln(o_p, pp, C_Z_COL, xp, w_o, ln_g_2, ln_b_2, tm)
    xs = _out_ln(o_s, ps, C_Z_COL, xs, w_o, ln_g_2, ln_b_2, bs)

    f_col = 4 * D_MODEL // LANES
    w = _pad_cols(w_in_3, 4 * D_MODEL + LANES).astype(BF16)
    pp = _matmul(xp, w, 2 * tm)
    ps = _matmul(xs, w, bs)
    vt, vt32 = _matmul_t(xp, w[:, vcols].T, tm, bp, want_f32=True)
    d_k_p, d_v_p = leaf(_matmul_t(xp, w[:, kcols].T, tm, bp, want_bf16=False, want_f32=True)[0]), leaf(vt32)
    d_k_s, d_v_s = heads(ps[:, 1024:2048], bs, 1, 16), heads(ps[:, 2048:3072], bs, 1, 16)
    lf_p = _logf(pp, f_col, b_f_3, tm).reshape(bp, sp, N_HEADS)
    lf_s = _logf(ps, f_col, b_f_3, bs).reshape(bs, 1, N_HEADS)
    c_p3 = _cumsum_lanes(lf_p.transpose(0, 2, 1), tm).transpose(0, 2, 1).reshape(mp, N_HEADS)
    qa, ka = _prep(pp, bp, "fox", 1, [c_p3], tm)
    o_p = _flash_t(qa, ka, vt, bp, fin="plain")
    o_s = _decode(ps.reshape(bs, 1, -1), None, page_table, kt_view(cache_d_k), kt_view(cache_d_v), "fox",
                  [cache_d_logf.transpose(0, 2, 1), lf_s.reshape(bs, N_HEADS, 1)])
    w_o = w_out_3.astype(BF16)
    xp = _out_ln(o_p, pp, 3, xp, w_o, ln_g_3, ln_b_3, tm)
    xs = _out_ln(o_s, ps, 3, xs, w_o, ln_g_3, ln_b_3, bs)

    return (xp.reshape(bp, sp, D_MODEL), xs.reshape(bs, 1, D_MODEL),
            a_k_p, a_v_p, a_k_s, a_v_s, b_k_p, b_v_p, b_k_s, b_v_s,
            c_p[0], c_p[1], c_p[2], c_p[3], c_p[4][:, -keep:], c_p[5][:, -keep:],
            c_s[0], c_s[1], c_s[2], c_s[3], c_kw_s, c_vw_s,
            d_k_p, d_v_p, lf_p, d_k_s, d_v_s, lf_s)
```

```python
import functools
import math

import numpy as np
import jax
import jax.numpy as jnp
from jax import lax
from jax.experimental import pallas as pl
from jax.experimental.pallas import tpu as pltpu

F32 = jnp.float32
BF16 = jnp.bfloat16
HIGHEST = lax.Precision.HIGHEST

D_MODEL = 1024
HEAD_DIM = 64
N_HEADS = 16
LANES = 128
SCALE = HEAD_DIM ** -0.5
PAGE = 128
DEPTH = 4
ALPHA = (2 * DEPTH) ** 0.25
LN_EPS = 1e-5
NEG = -1e30
TINY = 1e-30
A_BLOCK = 256
A_TOPK = 3
B_HEADS = 8
C_GROUPS = 2
C_RATIO = 8
C_CMP_BLOCK = 32
C_SEL_BLOCK = 64
C_TOPK = 4
C_WINDOW = 512
PAGES_PER_STEP = 16
NSA_PAGES_PER_STEP = 4
VMEM_LIMIT = 56 * 1024 * 1024


def _dot_nt(a, b):
    return lax.dot_general(a, b, (((1,), (1,)), ((), ())), preferred_element_type=F32)


def _dot(a, b):
    return jnp.dot(a, b, preferred_element_type=F32)


def _iota(shape, dim):
    return lax.broadcasted_iota(jnp.int32, shape, dim)


def _params(*sem):
    return pltpu.CompilerParams(dimension_semantics=sem, vmem_limit_bytes=VMEM_LIMIT)


def _alibi_slopes(n):
    return jnp.exp2(-8.0 * jnp.arange(1, n + 1, dtype=F32) / n)


def _smem():
    return pl.BlockSpec(memory_space=pltpu.SMEM)


def _mm_kernel(x_ref, w_ref, o_ref):
    o_ref[...] = _dot(x_ref[...].astype(BF16), w_ref[...])


def _pick_tn(n):
    best = LANES
    for t in range(LANES, 1536 + 1, LANES):
        if n % t == 0:
            best = t
    return best


def _matmul(x, w, tm):
    m, k = x.shape
    n = w.shape[1]
    tm = min(tm, m)
    assert m % tm == 0
    tn = _pick_tn(n)
    return pl.pallas_call(
        _mm_kernel,
        grid=(m // tm, n // tn),
        in_specs=[pl.BlockSpec((tm, k), lambda i, j: (i, 0)),
                  pl.BlockSpec((k, tn), lambda i, j: (0, j))],
        out_specs=pl.BlockSpec((tm, tn), lambda i, j: (i, j)),
        out_shape=jax.ShapeDtypeStruct((m, n), F32),
        compiler_params=_params("parallel", "arbitrary"),
    )(x, w)


def _vt_kernel(x_ref, w_ref, *o_refs, want_bf16, want_f32):
    r = _dot_nt(w_ref[...], x_ref[...].astype(BF16))
    if want_bf16:
        o_refs[0][...] = r.astype(BF16)
    if want_f32:
        o_refs[-1][0] = r


def _matmul_t(x, wt, tm, batch, want_bf16=True, want_f32=False):
    m, k = x.shape
    n = wt.shape[0]
    nt = m // batch // tm
    out_specs, out_shape = [], []
    if want_bf16:
        out_specs.append(pl.BlockSpec((n, tm), lambda i: (0, i)))
        out_shape.append(jax.ShapeDtypeStruct((n, m), BF16))
    if want_f32:
        out_specs.append(pl.BlockSpec((1, n, tm), lambda i: (i // nt, 0, i % nt)))
        out_shape.append(jax.ShapeDtypeStruct((batch, n, m // batch), F32))
    return pl.pallas_call(
        functools.partial(_vt_kernel, want_bf16=want_bf16, want_f32=want_f32),
        grid=(m // tm,),
        in_specs=[pl.BlockSpec((tm, k), lambda i: (i, 0)), pl.BlockSpec((n, k), lambda i: (0, 0))],
        out_specs=out_specs,
        out_shape=out_shape,
        compiler_params=_params("parallel"),
    )(x, wt)


def _out_ln_kernel(o_ref, z_ref, x_ref, w_ref, g_ref, b_ref, y_ref):
    z = z_ref[...]
    a = (o_ref[...] * (z * jax.nn.sigmoid(z))).astype(BF16)
    h = ALPHA * x_ref[...] + _dot(a, w_ref[...])
    hc = h - jnp.mean(h, axis=-1, keepdims=True)
    var = jnp.mean(hc * hc, axis=-1, keepdims=True)
    y_ref[...] = hc * lax.rsqrt(var + LN_EPS) * g_ref[...] + b_ref[...]


def _out_ln(o, p, z_col, x, w, g, b, tm):
    m = x.shape[0]
    row = lambda i: (i, 0)
    return pl.pallas_call(
        _out_ln_kernel,
        grid=(m // tm,),
        in_specs=[pl.BlockSpec((tm, D_MODEL), row),
                  pl.BlockSpec((tm, D_MODEL), lambda i: (i, z_col)),
                  pl.BlockSpec((tm, D_MODEL), row),
                  pl.BlockSpec((D_MODEL, D_MODEL), lambda i: (0, 0)),
                  pl.BlockSpec((1, D_MODEL), lambda i: (0, 0)),
                  pl.BlockSpec((1, D_MODEL), lambda i: (0, 0))],
        out_specs=pl.BlockSpec((tm, D_MODEL), row),
        out_shape=jax.ShapeDtypeStruct((m, D_MODEL), F32),
        compiler_params=_params("parallel"),
    )(o, p, x, w, g.reshape(1, -1), b.reshape(1, -1))


def _topk_mask(s, cand, lane_f, k):
    s = jnp.where(cand, s, NEG)
    sel = jnp.zeros(s.shape, F32)
    for _ in range(k):
        mx = jnp.max(s, axis=1, keepdims=True)
        idx = jnp.min(jnp.where(s == mx, lane_f, 1e9), axis=1, keepdims=True)
        pick = lane_f == idx
        valid = jnp.where(mx > 0.5 * NEG, 1.0, 0.0)
        sel = jnp.where(pick, valid, sel)
        s = jnp.where(pick, -3e38, s)
    return sel


def _lambda(lam_ref, lam_init):
    a = lam_ref[...]
    return (jnp.exp(jnp.sum(a[0:1] * a[1:2], axis=1, keepdims=True))
            - jnp.exp(jnp.sum(a[2:3] * a[3:4], axis=1, keepdims=True)) + lam_init)


def _block_mean_kernel(k_ref, o_ref):
    n = pl.program_id(1)

    @pl.when(n == 0)
    def _():
        o_ref[...] = jnp.zeros(o_ref.shape, F32)

    o_ref[0, pl.ds(n, 1), :] = jnp.sum(k_ref[...], axis=0, keepdims=True) / A_BLOCK


def _block_mean(p, batch, k_col):
    nb = p.shape[0] // batch // A_BLOCK
    return pl.pallas_call(
        _block_mean_kernel,
        grid=(batch, nb),
        in_specs=[pl.BlockSpec((A_BLOCK, D_MODEL), lambda b, n: (b * nb + n, k_col))],
        out_specs=pl.BlockSpec((1, LANES, D_MODEL), lambda b, n: (b, 0, 0)),
        out_shape=jax.ShapeDtypeStruct((batch, LANES, D_MODEL), F32),
        compiler_params=_params("parallel", "arbitrary"),
    )(p)


LOG2E = math.log2(math.e)
AUX0 = HEAD_DIM
SEL0 = 96
Q_SCALE = SCALE * LOG2E


def _split3(x):
    hi = x.astype(BF16).astype(F32)
    mid = (x - hi).astype(BF16).astype(F32)
    return hi, mid, x - hi - mid


def _lane_pick(lane, base, vals):
    out = jnp.zeros(lane.shape, F32)
    for idx, v in enumerate(vals):
        out = jnp.where(lane == base + idx, v, out)
    return out


def _slope_lanes(lane, sl_ref, h):
    parts = [sl_ref[h, c] for c in range(3)]
    return _lane_pick(lane, AUX0, parts + parts)


def _pos_lanes(lane, kpos):
    hi = kpos.astype(BF16).astype(F32)
    lo = kpos - hi
    return _lane_pick(lane, AUX0, [hi, hi, hi, lo, lo, lo])


def _head_cols(ref, h):
    x = ref[:, (h // 2) * LANES:(h // 2 + 1) * LANES]
    return x if h % 2 == 0 else pltpu.roll(x, HEAD_DIM, 1)


def _prep_moba_kernel(sl_ref, q_ref, k_ref, bm_ref, qa_ref, ka_ref, *, tm):
    i = pl.program_id(1)
    lane = _iota((tm, LANES), 1)
    lane_f = lane.astype(F32)
    low = lane < HEAD_DIM
    kpos = (i * tm + _iota((tm, LANES), 0)).astype(F32)
    k_aux = jnp.where(lane >= SEL0, jnp.where(lane == SEL0 + i, 1.0, 0.0), _pos_lanes(lane, kpos))
    bm_low = _iota((LANES, LANES), 1) < HEAD_DIM
    for h in range(N_HEADS):
        cs = slice(h * LANES, (h + 1) * LANES)
        q = _head_cols(q_ref, h)
        bm = bm_ref[0, :, (h // 2) * LANES:(h // 2 + 1) * LANES]
        bme = jnp.where(bm_low if h % 2 == 0 else ~bm_low, bm, 0.0).astype(BF16)
        qsel = q_ref[:, (h // 2) * LANES:(h // 2 + 1) * LANES].astype(BF16)
        sel = _topk_mask(_dot_nt(qsel, bme), lane < i, lane_f, A_TOPK)
        penalty = pltpu.roll(jnp.where((sel > 0.5) | (lane == i), 0.0, NEG), SEL0, 1)
        q_aux = jnp.where(lane >= SEL0, penalty, _slope_lanes(lane, sl_ref, h))
        qa_ref[:, cs] = jnp.where(low, q * Q_SCALE, q_aux).astype(BF16)
        ka_ref[:, cs] = jnp.where(low, _head_cols(k_ref, h), k_aux).astype(BF16)


def _prep_moba(p, batch, sl3, bm):
    tm = A_BLOCK
    nt = p.shape[0] // batch // tm
    assert nt <= LANES - SEL0
    row = lambda col: pl.BlockSpec((tm, D_MODEL), lambda b, i: (b * nt + i, col))
    aug = pl.BlockSpec((tm, N_HEADS * LANES), lambda b, i: (b * nt + i, 0))
    shape = jax.ShapeDtypeStruct((p.shape[0], N_HEADS * LANES), BF16)
    return pl.pallas_call(
        functools.partial(_prep_moba_kernel, tm=tm),
        grid=(batch, nt),
        in_specs=[_smem(), row(0), row(1), pl.BlockSpec((1, LANES, D_MODEL), lambda b, i: (b, 0, 0))],
        out_specs=[aug, aug],
        out_shape=[shape, shape],
        compiler_params=_params("parallel", "parallel"),
    )(sl3, p, p, bm)


def _prep_kernel(*refs, tm, mode):
    if mode == "diff":
        sl_ref, q_ref, k_ref, qa_ref, ka_ref = refs
    elif mode == "fox":
        q_ref, k_ref, c_ref, qa_ref, ka_ref = refs
    elif mode == "win":
        sl_ref, q_ref, k_ref, qa_ref, ka_ref = refs
    else:
        sl_ref, q_ref, k_ref, sel_ref, qa_ref, ka_ref = refs
    i = pl.program_id(1)
    lane = _iota((tm, LANES), 1)
    low = lane < HEAD_DIM
    kpos = (i * tm + _iota((tm, LANES), 0)).astype(F32)
    cw = 2 * LANES if mode == "sel" else LANES
    for h in range(N_HEADS):
        if mode == "fox":
            q_aux = _lane_pick(lane, AUX0, [-1.0, -1.0, -1.0])
        else:
            q_aux = _slope_lanes(lane, sl_ref, h)
        qa_ref[:, h * cw:h * cw + LANES] = jnp.where(low, _head_cols(q_ref, h) * Q_SCALE, q_aux).astype(BF16)
        if mode == "sel":
            g = h // C_RATIO
            picked = sel_ref[:, g * LANES:(g + 1) * LANES]
            qa_ref[:, h * cw + LANES:(h + 1) * cw] = jnp.where(picked > 0.5, 0.0, NEG).astype(BF16)
        if mode == "diff":
            ka_ref[:, h * cw:(h + 1) * cw] = jnp.where(low, _head_cols(k_ref, h), _pos_lanes(lane, kpos)).astype(BF16)
        if mode == "fox":
            c = jnp.broadcast_to(c_ref[:, h:h + 1], (tm, LANES)) * LOG2E
            ka_ref[:, h * cw:(h + 1) * cw] = jnp.where(low, _head_cols(k_ref, h), _lane_pick(lane, AUX0, _split3(c))).astype(BF16)
    if mode in ("win", "sel"):
        for g in range(C_GROUPS):
            ka_ref[:, g * cw:g * cw + LANES] = jnp.where(low, _head_cols(k_ref, g), _pos_lanes(lane, kpos)).astype(BF16)
            if mode == "sel":
                own = (i * tm + _iota((tm, LANES), 0)) // C_SEL_BLOCK
                ka_ref[:, g * cw + LANES:(g + 1) * cw] = jnp.where(lane == own, 1.0, 0.0).astype(BF16)


def _prep(p, batch, mode, k_col, extras, tm):
    nt = p.shape[0] // batch // tm
    cw = 2 * LANES if mode == "sel" else LANES
    nk = C_GROUPS if mode in ("win", "sel") else N_HEADS
    kw = LANES if mode in ("win", "sel") else D_MODEL
    row = lambda width, col: pl.BlockSpec((tm, width), lambda b, i: (b * nt + i, col))
    in_specs = ([] if mode == "fox" else [_smem()]) + [row(D_MODEL, 0), row(kw, k_col)]
    if mode == "fox":
        in_specs.append(row(N_HEADS, 0))
    if mode == "sel":
        assert extras[-1].shape[1] == C_GROUPS * LANES
        in_specs.append(row(C_GROUPS * LANES, 0))
    return pl.pallas_call(
        functools.partial(_prep_kernel, tm=tm, mode=mode),
        grid=(batch, nt),
        in_specs=in_specs,
        out_specs=[row(N_HEADS * cw, 0), row(nk * cw, 0)],
        out_shape=[jax.ShapeDtypeStruct((p.shape[0], N_HEADS * cw), BF16),
                   jax.ShapeDtypeStruct((p.shape[0], nk * cw), BF16)],
        compiler_params=_params("parallel", "parallel"),
    )(*(extras[:1] if mode != "fox" else []), p, p, *(extras if mode == "fox" else extras[1:]))


def _sweep_tables(nq, band):
    it, jt, ft, mt = [], [], [], []
    for i in range(nq):
        js = [j for j in (i - 1, i) if j >= 0] if band else list(range(i + 1))
        for n, j in enumerate(js):
            it.append(i)
            jt.append(j)
            ft.append(1 if n == 0 else 0)
            mt.append(1 if j == i else (2 if band else 0))
    return [np.asarray(a, np.int32) for a in (it, jt, ft, mt)]


def _flash_t_kernel(it_ref, jt_ref, ft_ref, mt_ref, *rest, t, cw, kdiv, vdiv, dv, fin, modes, lam_init, gate_idx,
                    has_live):
    if has_live:
        live_ref, rest = rest[0], rest[1:]
    q_ref, k_ref, v_ref, rest = rest[0], rest[1], rest[2], rest[3:]
    if fin == "diff":
        lam_ref, g_ref, o_ref, m_scr, l_scr, acc_scr = rest
    elif fin == "nsa":
        g_ref, prev_ref, o_ref, m_scr, l_scr, acc_scr = rest
    else:
        o_ref, m_scr, l_scr, acc_scr = rest
    st = pl.program_id(1)

    @pl.when(ft_ref[st] == 1)
    def _():
        m_scr[...] = jnp.full(m_scr.shape, NEG, F32)
        l_scr[...] = jnp.zeros(l_scr.shape, F32)
        acc_scr[...] = jnp.zeros(acc_scr.shape, F32)

    def tile(mode):
        if mode:
            diff = _iota((t, t), 0) - _iota((t, t), 1)
            allowed = diff <= 0 if mode == 1 else diff >= 0

        def scores(h):
            kh = k_ref[:, (h // kdiv) * cw:(h // kdiv + 1) * cw]
            return _dot_nt(kh, q_ref[:, h * cw:(h + 1) * cw])

        s_next = scores(0)
        for h in range(N_HEADS):
            s = s_next
            if h + 1 < N_HEADS:
                s_next = scores(h + 1)
            if mode:
                s = jnp.where(allowed, s, NEG)
            m_prev = m_scr[h:h + 1, :]
            m_new = jnp.maximum(m_prev, jnp.max(s, axis=0, keepdims=True))
            p = jnp.exp2(s - m_new)
            corr = jnp.exp2(m_prev - m_new)
            l_scr[h:h + 1, :] = corr * l_scr[h:h + 1, :] + jnp.sum(p, axis=0, keepdims=True)
            m_scr[h:h + 1, :] = m_new
            rows = slice(h * dv, (h + 1) * dv)
            vh = v_ref[(h // vdiv) * dv:(h // vdiv + 1) * dv, :]
            acc_scr[rows, :] = acc_scr[rows, :] * corr + _dot(vh, p.astype(BF16))

    for mode in modes:
        run = mt_ref[st] == mode
        if has_live and mode == 0:
            run = run & (live_ref[pl.program_id(0) * pl.num_programs(1) + st] != 0)
        pl.when(run)(functools.partial(tile, mode))

    @pl.when(it_ref[st] == jt_ref[st])
    def _():
        lo = _iota((t, LANES), 1) < HEAD_DIM
        for hp in range(N_HEADS // 2):
            cs = slice(hp * LANES, (hp + 1) * LANES)
            if fin == "diff":
                lam = _lambda(lam_ref, lam_init)
                o = (acc_scr[2 * hp * dv:(2 * hp + 1) * dv, :] / l_scr[2 * hp:2 * hp + 1, :]
                     - lam * (acc_scr[(2 * hp + 1) * dv:(2 * hp + 2) * dv, :] / l_scr[2 * hp + 1:2 * hp + 2, :]))
                o = o * lax.rsqrt(jnp.mean(o * o, axis=0, keepdims=True) + LN_EPS)
                o_ref[:, cs] = o.T * g_ref[...] * (1.0 - lam_init)
            else:
                o = jnp.concatenate(
                    [acc_scr[(2 * hp + e) * dv:(2 * hp + e + 1) * dv, :] / l_scr[2 * hp + e:2 * hp + e + 1, :]
                     for e in range(2)], axis=0).T
                if fin == "nsa":
                    g0 = jax.nn.sigmoid(g_ref[:, 6 * hp + gate_idx:6 * hp + gate_idx + 1])
                    g1 = jax.nn.sigmoid(g_ref[:, 6 * hp + 3 + gate_idx:6 * hp + 4 + gate_idx])
                    o = prev_ref[:, cs] + o * jnp.where(lo, g0, g1)
                o_ref[:, cs] = o


def _flash_t(qa, ka, vt, batch, *, fin, band=False, extras=(), lam_init=0.0, gate_idx=0, live=None, t=512):
    m = qa.shape[0]
    nq = m // batch // t
    cw = qa.shape[1] // N_HEADS
    kdiv = N_HEADS // (ka.shape[1] // cw)
    dv = LANES if fin == "diff" else HEAD_DIM
    vdiv = N_HEADS // (vt.shape[0] // dv)
    tabs = _sweep_tables(nq, band)
    if live is not None:
        tabs.append(live[:, tabs[0], tabs[1]].reshape(-1).astype(jnp.int32))
    modes = (1, 2) if band else (0, 1)
    assert not band or t == C_WINDOW
    imap = lambda f: (lambda b, s, it_, jt_, *_: f(b, s, it_, jt_))
    qrow = lambda width: pl.BlockSpec((t, width), imap(lambda b, s, it_, jt_: (b * nq + it_[s], 0)))
    in_specs = [qrow(qa.shape[1]),
                pl.BlockSpec((t, ka.shape[1]), imap(lambda b, s, it_, jt_: (b * nq + jt_[s], 0))),
                pl.BlockSpec((vt.shape[0], t), imap(lambda b, s, it_, jt_: (0, b * nq + jt_[s])))]
    if fin == "diff":
        in_specs += [pl.BlockSpec((4, HEAD_DIM), imap(lambda b, s, it_, jt_: (0, 0))),
                     pl.BlockSpec((1, LANES), imap(lambda b, s, it_, jt_: (0, 0)))]
    if fin == "nsa":
        in_specs += [pl.BlockSpec((t, LANES), imap(lambda b, s, it_, jt_: (b * nq + it_[s], C_GATE_COL))),
                     qrow(D_MODEL)]
    return pl.pallas_call(
        functools.partial(_flash_t_kernel, t=t, cw=cw, kdiv=kdiv, vdiv=vdiv, dv=dv, fin=fin, modes=modes,
                          lam_init=lam_init, gate_idx=gate_idx, has_live=live is not None),
        grid_spec=pltpu.PrefetchScalarGridSpec(
            num_scalar_prefetch=len(tabs),
            grid=(batch, len(tabs[0])),
            in_specs=in_specs,
            out_specs=qrow(D_MODEL),
            scratch_shapes=[pltpu.VMEM((N_HEADS, t), F32), pltpu.VMEM((N_HEADS, t), F32),
                            pltpu.VMEM((N_HEADS * dv, t), F32)]),
        out_shape=jax.ShapeDtypeStruct((m, D_MODEL), F32),
        compiler_params=_params("parallel", "arbitrary"),
    )(*tabs, qa, ka, vt, *extras)


def _logf_kernel(f_ref, b_ref, o_ref):
    x = f_ref[:, 0:N_HEADS] + b_ref[...]
    o_ref[...] = jnp.minimum(x, 0.0) - jnp.log1p(jnp.exp(-jnp.abs(x)))


def _logf(p, f_col, b_f, tm):
    m = p.shape[0]
    return pl.pallas_call(
        _logf_kernel,
        grid=(m // tm,),
        in_specs=[pl.BlockSpec((tm, LANES), lambda i: (i, f_col)),
                  pl.BlockSpec((1, N_HEADS), lambda i: (0, 0))],
        out_specs=pl.BlockSpec((tm, N_HEADS), lambda i: (i, 0)),
        out_shape=jax.ShapeDtypeStruct((m, N_HEADS), F32),
        compiler_params=_params("parallel"),
    )(p, b_f.reshape(1, -1))


def _cumsum_kernel(x_ref, o_ref, carry_scr, *, t):
    @pl.when(pl.program_id(1) == 0)
    def _():
        carry_scr[...] = jnp.zeros(carry_scr.shape, F32)

    x = x_ref[0]
    tri = (_iota((t, t), 0) <= _iota((t, t), 1)).astype(F32)
    c = jnp.dot(x, tri, precision=HIGHEST, preferred_element_type=F32) + carry_scr[...]
    o_ref[0] = c
    carry_scr[...] = c[:, t - 1:t]


def _cumsum_lanes(x, t):
    b, h, s = x.shape
    return pl.pallas_call(
        functools.partial(_cumsum_kernel, t=t),
        grid=(b, s // t),
        in_specs=[pl.BlockSpec((1, h, t), lambda bi, n: (bi, 0, n))],
        out_specs=pl.BlockSpec((1, h, t), lambda bi, n: (bi, 0, n)),
        out_shape=jax.ShapeDtypeStruct(x.shape, F32),
        scratch_shapes=[pltpu.VMEM((h, 1), F32)],
        compiler_params=_params("parallel", "arbitrary"),
    )(x)


C_Z_COL = 1
C_KV_COL = 16
C_GATE_COL = 22
C_WIDTH = 3072


def _compress_kernel(kc_ref, vc_ref, wk_ref, wv_ref, ck_ref, cv_ref, *, nch):
    stride = 2 * C_CMP_BLOCK
    for src, w_ref, dst in ((kc_ref, wk_ref, ck_ref), (vc_ref, wv_ref, cv_ref)):
        for parity in range(2):
            acc = jnp.zeros((nch, LANES), F32)
            for tt in range(C_CMP_BLOCK):
                rows = src[pl.ds(parity * C_CMP_BLOCK + tt, nch, stride=stride), :]
                acc = acc + _dot(rows.astype(BF16), w_ref[tt])
            dst[0, parity * nch:(parity + 1) * nch, :] = acc


def _compress_prompt(p, batch, w2k, w2v):
    s = p.shape[0] // batch
    nch = s // (2 * C_CMP_BLOCK)
    col = lambda c: pl.BlockSpec((s, LANES), lambda b: (b, c))
    wspec = pl.BlockSpec((C_CMP_BLOCK, LANES, LANES), lambda b: (0, 0, 0))
    ospec = pl.BlockSpec((1, 2 * nch, LANES), lambda b: (b, 0, 0))
    oshape = jax.ShapeDtypeStruct((batch, 2 * nch, LANES), F32)
    return pl.pallas_call(
        functools.partial(_compress_kernel, nch=nch),
        grid=(batch,),
        in_specs=[col(C_KV_COL), col(C_KV_COL + 1), wspec, wspec],
        out_specs=[ospec, ospec],
        out_shape=[oshape, oshape],
        compiler_params=_params("parallel"),
    )(p, p, w2k, w2v)


def _compress_pool_kernel(x_ref, w_ref, o_ref):
    acc = jnp.zeros(o_ref.shape, F32)
    for d in range(HEAD_DIM):
        acc = acc + _dot(x_ref[:, d, :].astype(BF16), w_ref[d])
    o_ref[...] = acc


def _compress_pool(cache, w_cmp):
    n_pool = cache.shape[0]
    tokens = PAGE // C_CMP_BLOCK
    x = cache.transpose(0, 2, 3, 1).reshape(n_pool * C_GROUPS, HEAD_DIM, PAGE)
    r = jnp.arange(PAGE)
    in_token = (r[:, None] // C_CMP_BLOCK == jnp.arange(tokens)[None, :]).astype(w_cmp.dtype)
    w_rows = w_cmp.reshape(C_CMP_BLOCK, HEAD_DIM, HEAD_DIM)[r % C_CMP_BLOCK]
    wx = (w_rows[:, :, None, :] * in_token[:, None, :, None]).transpose(1, 0, 2, 3)
    wx = wx.reshape(HEAD_DIM, PAGE, tokens * HEAD_DIM).astype(BF16)
    tm = min(256, x.shape[0])
    assert x.shape[0] % tm == 0
    out = pl.pallas_call(
        _compress_pool_kernel,
        grid=(x.shape[0] // tm,),
        in_specs=[pl.BlockSpec((tm, HEAD_DIM, PAGE), lambda i: (i, 0, 0)),
                  pl.BlockSpec(wx.shape, lambda i: (0, 0, 0))],
        out_specs=pl.BlockSpec((tm, tokens * HEAD_DIM), lambda i: (i, 0)),
        out_shape=jax.ShapeDtypeStruct((x.shape[0], tokens * HEAD_DIM), F32),
        compiler_params=_params("parallel"),
    )(x, wx)
    return out.reshape(n_pool, C_GROUPS, tokens, HEAD_DIM).transpose(0, 2, 1, 3).reshape(n_pool, tokens, LANES)


def _group_halves(x, g):
    lo = _iota(x.shape, 1) < HEAD_DIM
    base = jnp.where(lo if g == 0 else ~lo, x, 0.0)
    other = pltpu.roll(base, HEAD_DIM, 1)
    pair = (base, other) if g == 0 else (other, base)
    return pair[0].astype(BF16), pair[1].astype(BF16)


def _nsa_cmp_kernel(sl_ref, q_ref, g_ref, ck_ref, cv_ref, o_ref, sel_ref, *, tq, nch):
    q0 = pl.program_id(1) * tq
    nl = 2 * nch
    lane = _iota((tq, nl), 1)
    qpos = q0 + _iota((tq, nl), 0)
    tok = jnp.where(lane < nch, 2 * lane, 2 * (lane - nch) + 1)
    endp = (tok + 1) * C_CMP_BLOCK - 1
    okc = endp <= qpos
    relc = (endp[0:1, :] - q0).astype(F32)
    lane_s = _iota((tq, nch), 1)
    qblk = (q0 + _iota((tq, nch), 0)) // C_SEL_BLOCK
    lane_sf = lane_s.astype(F32)
    for g in range(C_GROUPS):
        k_lo, k_hi = _group_halves(ck_ref[0], g)
        v_lo, v_hi = _group_halves(cv_ref[0], g)
        imp = jnp.zeros((tq, nl), F32)
        for hp in range(g * C_RATIO // 2, (g + 1) * C_RATIO // 2):
            cs = slice(hp * LANES, (hp + 1) * LANES)
            qp = (q_ref[:, cs] * SCALE).astype(BF16)
            o_pair = None
            for e in range(2):
                h = 2 * hp + e
                s = _dot_nt(qp, k_lo if e == 0 else k_hi) + sl_ref[h] * relc
                s = jnp.where(okc, s, NEG)
                pc = jnp.where(okc, jnp.exp(s - jnp.max(s, axis=1, keepdims=True)), 0.0)
                pc = pc / jnp.maximum(jnp.sum(pc, axis=1, keepdims=True), TINY)
                imp = imp + pc
                gate = jax.nn.sigmoid(g_ref[:, 3 * h:3 * h + 1])
                o = _dot(pc.astype(BF16), v_lo if e == 0 else v_hi) * gate
                o_pair = o if o_pair is None else o_pair + o
            o_ref[:, cs] = o_pair
        imp_sel = imp[:, 0:nch] + imp[:, nch:nl]
        sel = _topk_mask(imp_sel, lane_s < qblk, lane_sf, C_TOPK)
        sel = jnp.where(lane_s == qblk, 1.0, sel)
        if nch < LANES:
            sel = jnp.concatenate([sel, jnp.zeros((tq, LANES - nch), F32)], axis=1)
        sel_ref[:, g * LANES:(g + 1) * LANES] = sel


def _nsa_cmp(p, batch, slopes, ck, cv, tq):
    nq = p.shape[0] // batch // tq
    nch = ck.shape[1] // 2
    assert nch <= LANES
    row = lambda width, col: pl.BlockSpec((tq, width), lambda b, i: (b * nq + i, col))
    cspec = pl.BlockSpec((1, 2 * nch, LANES), lambda b, i: (b, 0, 0))
    return pl.pallas_call(
        functools.partial(_nsa_cmp_kernel, tq=tq, nch=nch),
        grid=(batch, nq),
        in_specs=[_smem(), row(D_MODEL, 0), row(LANES, C_GATE_COL), cspec, cspec],
        out_specs=[row(D_MODEL, 0), row(C_GROUPS * LANES, 0)],
        out_shape=[jax.ShapeDtypeStruct((p.shape[0], D_MODEL), F32),
                   jax.ShapeDtypeStruct((p.shape[0], C_GROUPS * LANES), F32)],
        compiler_params=_params("parallel", "parallel"),
    )(slopes, p, p, ck, cv)


def _decode_kernel(pt_ref, q_ref, kn_ref, vn_ref, *rest, n_pages, mode, lam_init):
    del pt_ref
    gp = PAGES_PER_STEP
    if mode == "moba":
        slc_ref, rest = rest[0], rest[1:]
    elif mode == "diff":
        slc_ref, lam_ref, g_ref, rest = rest[0], rest[1], rest[2], rest[3:]
    else:
        lfn_ref, lft_refs, rest = rest[0], rest[1:1 + gp], rest[1 + gp:]
    kt_refs, v_refs, rest = rest[:gp], rest[gp:2 * gp], rest[2 * gp:]
    if mode == "fox":
        o_ref, qbd_scr, st_scr, p_scr, pn_scr, acc_scr, lf_scr = rest
    else:
        o_ref, qbd_scr, st_scr, p_scr, pn_scr, acc_scr = rest
    s = pl.program_id(1)
    n_steps = n_pages // gp
    past = n_pages * PAGE
    row = _iota((N_HEADS, D_MODEL), 0)
    lane = _iota((N_HEADS, D_MODEL), 1)
    lane_b = _iota((N_HEADS, LANES), 1)

    @pl.when(s == 0)
    def _():
        qbd_scr[...] = jnp.where(lane // HEAD_DIM == row, jnp.broadcast_to(q_ref[0], (N_HEADS, D_MODEL)), 0.0)
        acc_scr[...] = jnp.zeros(acc_scr.shape, F32)

    @pl.when(s < n_steps)
    def _():
        qb = (qbd_scr[...] * SCALE).astype(BF16)
        for g in range(gp):
            off = pl.multiple_of((s * gp + g) * PAGE, PAGE)
            st_scr[:, pl.ds(off, PAGE)] = _dot(qb, kt_refs[g][0].astype(BF16))
            if mode == "fox":
                lf_scr[:, pl.ds(off, PAGE)] = lft_refs[g][0]

    @pl.when(s == n_steps - 1)
    def _():
        qf = qbd_scr[...]
        qk_new = jnp.sum(qf * jnp.broadcast_to(kn_ref[0], qf.shape), axis=1, keepdims=True)
        s_new = qk_new * SCALE
        if mode == "fox":
            c = lf_scr[...]
            kpos = _iota(c.shape, 1)
            shift = 1
            while shift < past:
                c = c + jnp.where(kpos >= shift, pltpu.roll(c, shift, 1), 0.0)
                shift *= 2
            s_all = st_scr[...] - c
            s_new = s_new - (c[:, past - 1:past] + lfn_ref[0])
        else:
            kpos = _iota((1, past), 1)
            s_all = st_scr[...] + slc_ref[...] * (kpos - past).astype(F32)
        if mode == "moba":
            sb = jnp.where(lane_b == past // A_BLOCK, qk_new / A_BLOCK, 0.0)
            for n in range(past // A_BLOCK):
                blk = jnp.sum(st_scr[:, n * A_BLOCK:(n + 1) * A_BLOCK], axis=1, keepdims=True) / (SCALE * A_BLOCK)
                sb = jnp.where(lane_b == n, blk, sb)
            sel = _topk_mask(sb, lane_b < past // A_BLOCK, lane_b.astype(F32), A_TOPK)
            st_scr[...] = s_all
            for n in range(past // A_BLOCK):
                cols = slice(n * A_BLOCK, (n + 1) * A_BLOCK)
                st_scr[:, cols] = st_scr[:, cols] + jnp.where(sel[:, n:n + 1] > 0.5, 0.0, NEG)
            s_all = st_scr[...]
        m = jnp.maximum(jnp.max(s_all, axis=1, keepdims=True), s_new)
        p = jnp.exp(s_all - m)
        pn = jnp.exp(s_new - m)
        l = jnp.sum(p, axis=1, keepdims=True) + pn
        p_scr[...] = p / l
        pn_scr[...] = pn / l

    @pl.when(s >= n_steps)
    def _():
        acc = acc_scr[...]
        for g in range(gp):
            off = pl.multiple_of(((s - n_steps) * gp + g) * PAGE, PAGE)
            pg = p_scr[:, pl.ds(off, PAGE)].astype(BF16)
            if mode == "diff":
                for h in range(B_HEADS):
                    vh = v_refs[g][0, pl.ds(h, PAGE, stride=B_HEADS), :]
                    acc = acc + jnp.where(_iota((N_HEADS, LANES), 0) // 2 == h, _dot(pg, vh.astype(BF16)), 0.0)
            else:
                acc = acc + _dot_nt(pg, v_refs[g][0].astype(BF16))
        acc_scr[...] = acc

    @pl.when(s == 2 * n_steps - 1)
    def _():
        acc = acc_scr[...]
        if mode == "diff":
            row_b = _iota((N_HEADS, LANES), 0)
            acc = acc + pn_scr[...] * vn_ref[0]
            signed = jnp.where(row_b % 2 == 0, acc, -_lambda(lam_ref, lam_init) * acc)
            o = signed + pltpu.roll(signed, N_HEADS - 1, 0)
            o = o * lax.rsqrt(jnp.mean(o * o, axis=1, keepdims=True) + LN_EPS)
            o_ref[0] = o * g_ref[...] * (1.0 - lam_init)
        else:
            acc = acc + pn_scr[...] * jnp.broadcast_to(vn_ref[0], (N_HEADS, D_MODEL))
            o_ref[0] = jnp.sum(jnp.where(lane // HEAD_DIM == row, acc, 0.0), axis=0, keepdims=True)


def _decode(ps3, vn, page_table, kt_pool, v_pool, mode, extras, lam_init=0.0):
    b = ps3.shape[0]
    n_pages = page_table.shape[1]
    gp = PAGES_PER_STEP
    assert n_pages % gp == 0 and (n_pages * PAGE) % A_BLOCK == 0
    n_steps = n_pages // gp
    past = n_pages * PAGE
    pt = page_table.reshape(-1)
    rowspec = lambda col: pl.BlockSpec((1, 1, D_MODEL), lambda bi, s, pt_: (bi, 0, col))
    per_b = lambda shape: pl.BlockSpec((1,) + shape, lambda bi, s, pt_: (bi,) + tuple(0 for _ in shape))
    full = lambda shape: pl.BlockSpec(shape, lambda bi, s, pt_: tuple(0 for _ in shape))
    kpage = lambda g: (lambda bi, s, pt_: (pt_[bi * n_pages + jnp.minimum(s, n_steps - 1) * gp + g], 0, 0))
    vpage = lambda g: (lambda bi, s, pt_: (pt_[bi * n_pages + jnp.maximum(s - n_steps, 0) * gp + g], 0, 0))
    kspecs = [pl.BlockSpec((1,) + kt_pool.shape[1:], kpage(g)) for g in range(gp)]
    vspecs = [pl.BlockSpec((1,) + v_pool.shape[1:], vpage(g)) for g in range(gp)]
    out_tile = (N_HEADS, LANES) if mode == "diff" else (1, D_MODEL)
    scratch = [pltpu.VMEM((N_HEADS, D_MODEL), F32), pltpu.VMEM((N_HEADS, past), F32),
               pltpu.VMEM((N_HEADS, past), F32), pltpu.VMEM((N_HEADS, 1), F32),
               pltpu.VMEM((N_HEADS, LANES if mode == "diff" else D_MODEL), F32)]
    args = list(extras)
    if mode == "moba":
        in_specs = [full((N_HEADS, 1))]
    elif mode == "diff":
        in_specs = [full((N_HEADS, 1)), full((4, HEAD_DIM)), full((1, LANES))]
    else:
        lft, lfn = extras
        in_specs = [per_b((N_HEADS, 1))] + [pl.BlockSpec((1, N_HEADS, PAGE), kpage(g)) for g in range(gp)]
        args = [lfn] + [lft] * gp
        scratch += [pltpu.VMEM((N_HEADS, past), F32)]
    vn_spec, vn_arg = (per_b(vn.shape[1:]), vn) if mode == "diff" else (rowspec(2), ps3)
    out = pl.pallas_call(
        functools.partial(_decode_kernel, n_pages=n_pages, mode=mode, lam_init=lam_init),
        grid_spec=pltpu.PrefetchScalarGridSpec(
            num_scalar_prefetch=1,
            grid=(b, 2 * n_steps),
            in_specs=[rowspec(0), rowspec(1), vn_spec] + in_specs + kspecs + vspecs,
            out_specs=per_b(out_tile),
            scratch_shapes=scratch),
        out_shape=jax.ShapeDtypeStruct((b,) + out_tile, F32),
        compiler_params=_params("parallel", "arbitrary"),
    )(pt, ps3, ps3, vn_arg, *args, *([kt_pool] * gp), *([v_pool] * gp))
    if mode == "diff":
        out = out[:, 0::2, :]
    return out.reshape(b, D_MODEL)


C_HALF = 64


def _softmax_new(s_all, s_new, v_all, v_new, v_transposed):
    m = jnp.maximum(jnp.max(s_all, axis=1, keepdims=True), s_new)
    p = jnp.exp(s_all - m)
    pn = jnp.exp(s_new - m)
    l = jnp.sum(p, axis=1, keepdims=True) + pn
    pv = _dot_nt(p.astype(BF16), v_all.astype(BF16)) if v_transposed else _dot(p.astype(BF16), v_all.astype(BF16))
    return (pv + pn * v_new) / l


def _nsa_decode_kernel(pt_ref, qbd_ref, g_ref, slc_ref, kcn_ref, vcn_ref, ksn_ref, vsn_ref, kwn_ref, vwn_ref,
                       kw_ref, vw_ref, wk0_ref, wv0_ref, *rest, n_pages):
    del pt_ref
    gp = NSA_PAGES_PER_STEP
    ckp_refs, cvp_refs, kst_refs, vst_refs = (rest[i * gp:(i + 1) * gp] for i in range(4))
    o_ref, ck_scr, cv_scr, kst_scr, vst_scr = rest[4 * gp:]
    s = pl.program_id(1)
    past = n_pages * PAGE
    per_page = PAGE // C_CMP_BLOCK // 2

    @pl.when(s == 0)
    def _():
        ck_scr[...] = jnp.zeros(ck_scr.shape, F32)
        cv_scr[...] = jnp.zeros(cv_scr.shape, F32)

    for g in range(gp):
        page = s * gp + g
        off = pl.multiple_of(page * PAGE, PAGE)
        kst_scr[:, pl.ds(off, PAGE)] = kst_refs[g][0]
        vst_scr[:, pl.ds(off, PAGE)] = vst_refs[g][0]
        for u in range(2 * per_page):
            dst = (u % 2) * C_HALF + per_page * page + u // 2
            ck_scr[pl.ds(dst, 1), :] = ckp_refs[g][0, u:u + 1, :]
            cv_scr[pl.ds(dst, 1), :] = cvp_refs[g][0, u:u + 1, :]

    @pl.when(s == n_pages // gp - 1)
    def _():
        qf = qbd_ref[0] * SCALE
        qb = qf.astype(BF16)
        slc = slc_ref[...]
        new_tok = per_page * n_pages
        ck_scr[new_tok:new_tok + 1, :] = _dot(jnp.broadcast_to(kcn_ref[0], (8, LANES)).astype(BF16), wk0_ref[...])[0:1]
        cv_scr[new_tok:new_tok + 1, :] = _dot(jnp.broadcast_to(vcn_ref[0], (8, LANES)).astype(BF16), wv0_ref[...])[0:1]
        lane = _iota((N_HEADS, LANES), 1)
        row = _iota((N_HEADS, LANES), 0)
        tok = jnp.where(lane < C_HALF, 2 * lane, 2 * (lane - C_HALF) + 1)
        endp = (tok + 1) * C_CMP_BLOCK - 1
        okc = endp <= past
        sc = _dot_nt(qb, ck_scr[...].astype(BF16)) + slc * (endp - past).astype(F32)
        sc = jnp.where(okc, sc, NEG)
        pc = jnp.where(okc, jnp.exp(sc - jnp.max(sc, axis=1, keepdims=True)), 0.0)
        pc = pc / jnp.maximum(jnp.sum(pc, axis=1, keepdims=True), TINY)
        o_cmp = _dot(pc.astype(BF16), cv_scr[...].astype(BF16))
        imp = jnp.where(row < C_RATIO, jnp.sum(pc[0:C_RATIO], axis=0, keepdims=True),
                        jnp.sum(pc[C_RATIO:N_HEADS], axis=0, keepdims=True))
        imp = imp + pltpu.roll(imp, C_HALF, 1)
        sel = _topk_mask(imp, lane < past // C_SEL_BLOCK, lane.astype(F32), C_TOPK)
        expand = (_iota((LANES, past), 0) == _iota((LANES, past), 1) // C_SEL_BLOCK).astype(BF16)
        picked = _dot(sel.astype(BF16), expand)
        kpos = _iota((1, past), 1)
        ss = _dot(qb, kst_scr[...].astype(BF16)) + slc * (kpos - past).astype(F32)
        ss = jnp.where(picked > 0.5, ss, NEG)
        ss_new = jnp.sum(qf * ksn_ref[0], axis=1, keepdims=True)
        o_sel = _softmax_new(ss, ss_new, vst_scr[...], vsn_ref[0], True)
        wb = kw_ref.shape[1]
        wpos = past - wb + _iota((1, wb), 1)
        okw = (past - wpos <= C_WINDOW) & (wpos >= 0)
        sw = _dot_nt(qb, kw_ref[0].astype(BF16)) + slc * (wpos - past).astype(F32)
        sw = jnp.where(okw, sw, NEG)
        sw_new = jnp.sum(qf * kwn_ref[0], axis=1, keepdims=True)
        o_win = _softmax_new(sw, sw_new, vw_ref[0], vwn_ref[0], False)
        gate = jax.nn.sigmoid(g_ref[0])
        o_ref[0] = gate[:, 0:1] * o_cmp + gate[:, 1:2] * o_sel + gate[:, 2:3] * o_win


def _nsa_decode(ps3, qbd, gates, slc, page_table, ck_pool, cv_pool, kst_pool, vst_pool, kw_buf, vw_buf, wk0, wv0):
    b = ps3.shape[0]
    n_pages = page_table.shape[1]
    gp = NSA_PAGES_PER_STEP
    past = n_pages * PAGE
    assert 2 * n_pages + 1 <= C_HALF and n_pages % gp == 0
    pt = page_table.reshape(-1)
    per_b = lambda shape: pl.BlockSpec((1,) + shape, lambda bi, s, pt_: (bi,) + tuple(0 for _ in shape))
    full = lambda shape: pl.BlockSpec(shape, lambda bi, s, pt_: tuple(0 for _ in shape))
    newrow = lambda col: pl.BlockSpec((1, 1, LANES), lambda bi, s, pt_: (bi, 0, col))
    paged = lambda rows: [pl.BlockSpec((1, rows, LANES), (lambda g: lambda bi, s, pt_: (pt_[bi * n_pages + s * gp + g], 0, 0))(g))
                          for g in range(gp)]
    tokens = PAGE // C_CMP_BLOCK
    return pl.pallas_call(
        functools.partial(_nsa_decode_kernel, n_pages=n_pages),
        grid_spec=pltpu.PrefetchScalarGridSpec(
            num_scalar_prefetch=1,
            grid=(b, n_pages // gp),
            in_specs=[per_b((N_HEADS, LANES)), per_b((N_HEADS, 3)), full((N_HEADS, 1))]
                     + [newrow(C_KV_COL + c) for c in range(6)]
                     + [per_b(kw_buf.shape[1:]), per_b(vw_buf.shape[1:]), full((LANES, LANES)), full((LANES, LANES))]
                     + paged(tokens) + paged(tokens) + paged(PAGE) + paged(PAGE),
            out_specs=per_b((N_HEADS, LANES)),
            scratch_shapes=[pltpu.VMEM((2 * C_HALF, LANES), F32), pltpu.VMEM((2 * C_HALF, LANES), F32),
                            pltpu.VMEM((LANES, past), F32), pltpu.VMEM((LANES, past), F32)]),
        out_shape=jax.ShapeDtypeStruct((b, N_HEADS, LANES), F32),
        compiler_params=_params("parallel", "arbitrary"),
    )(pt, qbd, gates, slc, ps3, ps3, ps3, ps3, ps3, ps3, kw_buf, vw_buf, wk0, wv0,
      *([ck_pool] * gp), *([cv_pool] * gp), *([kst_pool] * gp), *([vst_pool] * gp))


def _pad_cols(w, width):
    return jnp.pad(w, ((0, 0), (0, width - w.shape[1])))


def _block_diag_cmp(w):
    w3 = w.reshape(C_CMP_BLOCK, HEAD_DIM, HEAD_DIM)
    z = jnp.zeros_like(w3)
    return jnp.concatenate([jnp.concatenate([w3, z], 2), jnp.concatenate([z, w3], 2)], 1).astype(BF16)


def kernel(x_prompt, x_sample, cache_a_k, cache_a_v, cache_b_k, cache_b_v, cache_c_kc, cache_c_vc, cache_c_ks, cache_c_vs, state_c_kw, state_c_vw, cache_d_k, cache_d_v, cache_d_logf, page_table, w_in_0, w_out_0, ln_g_0, ln_b_0, w_in_1, lam_q1_1, lam_k1_1, lam_q2_1, lam_k2_1, subln_g_1, w_out_1, ln_g_1, ln_b_1, w_in_2, w_cmp_k_2, w_cmp_v_2, w_out_2, ln_g_2, ln_b_2, w_in_3, b_f_3, w_out_3, ln_g_3, ln_b_3):
    bp, sp, _ = x_prompt.shape
    bs = x_sample.shape[0]
    assert x_sample.shape[1] == 1 and sp % C_WINDOW == 0
    n_pool = cache_a_k.shape[0]
    mp = bp * sp
    tm = 512
    xp = x_prompt.reshape(mp, D_MODEL)
    xs = x_sample.reshape(bs, D_MODEL)
    heads = lambda a, b, l, h: a.reshape(b, l, h, -1)
    split_slopes = lambda sl: jnp.stack(_split3(sl * LOG2E), axis=1)
    kcols = slice(D_MODEL, 2 * D_MODEL)
    vcols = slice(2 * D_MODEL, 3 * D_MODEL)
    leaf = lambda t3: t3.reshape(bp, N_HEADS, HEAD_DIM, sp).transpose(0, 3, 1, 2)
    rows_last = lambda c: c.transpose(0, 2, 3, 1)
    kt_view = lambda c: rows_last(c).reshape(n_pool, D_MODEL, PAGE)

    w = w_in_0.astype(BF16)
    slopes16 = _alibi_slopes(N_HEADS)
    sl3_16 = split_slopes(slopes16)
    pp = _matmul(xp, w, 2 * tm)
    ps = _matmul(xs, w, bs)
    vt, vt32 = _matmul_t(xp, w[:, vcols].T, tm, bp, want_f32=True)
    a_k_p, a_v_p = leaf(_matmul_t(xp, w[:, kcols].T, tm, bp, want_bf16=False, want_f32=True)[0]), leaf(vt32)
    a_k_s, a_v_s = heads(ps[:, 1024:2048], bs, 1, 16), heads(ps[:, 2048:3072], bs, 1, 16)
    qa, ka = _prep_moba(pp, bp, sl3_16, _block_mean(pp, bp, 1))
    o_p = _flash_t(qa, ka, vt, bp, fin="plain")
    o_s = _decode(ps.reshape(bs, 1, -1), None, page_table, kt_view(cache_a_k), kt_view(cache_a_v), "moba",
                  [slopes16.reshape(-1, 1)])
    w_o = w_out_0.astype(BF16)
    xp = _out_ln(o_p, pp, 3, xp, w_o, ln_g_0, ln_b_0, tm)
    xs = _out_ln(o_s, ps, 3, xs, w_o, ln_g_0, ln_b_0, bs)

    lam_init = 0.8 - 0.6 * math.exp(-0.3 * 1)
    w = w_in_1.astype(BF16)
    slopes8 = jnp.repeat(_alibi_slopes(B_HEADS), 2)
    lamv = jnp.stack([lam_q1_1, lam_k1_1, lam_q2_1, lam_k2_1])
    pp = _matmul(xp, w, 2 * tm)
    ps = _matmul(xs, w, bs)
    b_k_p = leaf(_matmul_t(xp, w[:, kcols].T, tm, bp, want_bf16=False, want_f32=True)[0])
    b_v_p = heads(pp[:, 2048:3072], bp, sp, 8)
    b_k_s, b_v_s = heads(ps[:, 1024:2048], bs, 1, 16), heads(ps[:, 2048:3072], bs, 1, 8)
    qa, ka = _prep(pp, bp, "diff", 1, [split_slopes(slopes8)], tm)
    o_p = _flash_t(qa, ka, _matmul_t(xp, w[:, vcols].T, tm, bp)[0], bp, fin="diff",
                   extras=(lamv, subln_g_1.reshape(1, -1)), lam_init=lam_init)
    o_s = _decode(ps.reshape(bs, 1, -1), jnp.repeat(ps[:, vcols].reshape(bs, B_HEADS, LANES), 2, axis=1), page_table,
                  kt_view(cache_b_k), cache_b_v.reshape(n_pool, PAGE * B_HEADS, LANES), "diff",
                  [slopes8.reshape(-1, 1), lamv, subln_g_1.reshape(1, -1)], lam_init)
    w_o = w_out_1.astype(BF16)
    xp = _out_ln(o_p, pp, 3, xp, w_o, ln_g_1, ln_b_1, tm)
    xs = _out_ln(o_s, ps, 3, xs, w_o, ln_g_1, ln_b_1, bs)

    kv0, z0, g0 = D_MODEL, D_MODEL + 6 * LANES, 2 * D_MODEL + 6 * LANES
    w = _pad_cols(jnp.concatenate([w_in_2[:, :kv0], w_in_2[:, z0:g0], w_in_2[:, kv0:z0], w_in_2[:, g0:]], 1),
                  C_WIDTH).astype(BF16)
    w2k, w2v = _block_diag_cmp(w_cmp_k_2), _block_diag_cmp(w_cmp_v_2)
    pp = _matmul(xp, w, 2 * tm)
    ps = _matmul(xs, w, bs)
    kvcol = lambda p, c: p[:, (C_KV_COL + c) * LANES:(C_KV_COL + c + 1) * LANES]
    c_p = [heads(kvcol(pp, c), bp, sp, 2) for c in range(6)]
    c_s = [heads(kvcol(ps, c), bs, 1, 2) for c in range(6)]
    keep = min(C_WINDOW, sp)
    ck, cv = _compress_prompt(pp, bp, w2k, w2v)
    o_p, sel = _nsa_cmp(pp, bp, slopes16, ck, cv, 256)
    qa, ka = _prep(pp, bp, "sel", C_KV_COL + 2, [sl3_16, sel], tm)
    blocks = tm // C_SEL_BLOCK
    live = (sel.reshape(bp, sp // tm, tm, C_GROUPS, LANES // blocks, blocks).max((2, 3, 5)) > 0.5)
    o_p = _flash_t(qa, ka, _matmul_t(xp, kvcol(w, 3).T, tm, bp)[0], bp, fin="nsa", extras=(pp, o_p), gate_idx=1,
                   live=live)
    qa, ka = _prep(pp, bp, "win", C_KV_COL + 4, [sl3_16], tm)
    o_p = _flash_t(qa, ka, _matmul_t(xp, kvcol(w, 5).T, tm, bp)[0], bp, fin="nsa", band=True, extras=(pp, o_p),
                   gate_idx=2)
    ck_pool = _compress_pool(cache_c_kc, w_cmp_k_2)
    cv_pool = _compress_pool(cache_c_vc, w_cmp_v_2)
    q_s = ps[:, :D_MODEL].reshape(bs, N_HEADS, 1, HEAD_DIM)
    in_group = (jnp.arange(N_HEADS) // C_RATIO)[:, None] == jnp.arange(C_GROUPS)[None, :]
    qbd = jnp.where(in_group[None, :, :, None], q_s, 0.0).reshape(bs, N_HEADS, LANES)
    gates_s = ps[:, C_GATE_COL * LANES:C_GATE_COL * LANES + 3 * N_HEADS].reshape(bs, N_HEADS, 3)
    wb = state_c_kw.shape[1]
    o16 = _nsa_decode(ps.reshape(bs, 1, -1), qbd, gates_s, slopes16.reshape(-1, 1), page_table, ck_pool, cv_pool,
                      rows_last(cache_c_ks).reshape(n_pool, LANES, PAGE), rows_last(cache_c_vs).reshape(n_pool, LANES, PAGE),
                      state_c_kw.reshape(bs, wb, LANES),
                      state_c_vw.reshape(bs, wb, LANES), w2k[0], w2v[0])
    o16 = o16.reshape(bs, N_HEADS, C_GROUPS, HEAD_DIM)
    o_s = jnp.where(in_group[None, :, :, None], o16, 0.0).sum(2).reshape(bs, D_MODEL)
    c_kw_s = jnp.concatenate([state_c_kw, c_s[4]], 1)[:, -min(C_WINDOW, wb + 1):]
    c_vw_s = jnp.concatenate([state_c_vw, c_s[5]], 1)[:, -min(C_WINDOW, wb + 1):]
    w_o = w_out_2.astype(BF16)
    xp = _out_ln(o_p, pp, C_Z_COL, xp, w_o, ln_g_2, ln_b_2, tm)
    xs = _out_ln(o_s, ps, C_Z_COL, xs, w_o, ln_g_2, ln_b_2, bs)

    f_col = 4 * D_MODEL // LANES
    w = _pad_cols(w_in_3, 4 * D_MODEL + LANES).astype(BF16)
    pp = _matmul(xp, w, 2 * tm)
    ps = _matmul(xs, w, bs)
    vt, vt32 = _matmul_t(xp, w[:, vcols].T, tm, bp, want_f32=True)
    d_k_p, d_v_p = leaf(_matmul_t(xp, w[:, kcols].T, tm, bp, want_bf16=False, want_f32=True)[0]), leaf(vt32)
    d_k_s, d_v_s = heads(ps[:, 1024:2048], bs, 1, 16), heads(ps[:, 2048:3072], bs, 1, 16)
    lf_p = _logf(pp, f_col, b_f_3, tm).reshape(bp, sp, N_HEADS)
    lf_s = _logf(ps, f_col, b_f_3, bs).reshape(bs, 1, N_HEADS)
    c_p3 = _cumsum_lanes(lf_p.transpose(0, 2, 1), tm).transpose(0, 2, 1).reshape(mp, N_HEADS)
    qa, ka = _prep(pp, bp, "fox", 1, [c_p3], tm)
    o_p = _flash_t(qa, ka, vt, bp, fin="plain")
    o_s = _decode(ps.reshape(bs, 1, -1), None, page_table, kt_view(cache_d_k), kt_view(cache_d_v), "fox",
                  [cache_d_logf.transpose(0, 2, 1), lf_s.reshape(bs, N_HEADS, 1)])
    w_o = w_out_3.astype(BF16)
    xp = _out_ln(o_p, pp, 3, xp, w_o, ln_g_3, ln_b_3, tm)
    xs = _out_ln(o_s, ps, 3, xs, w_o, ln_g_3, ln_b_3, bs)

    return (xp.reshape(bp, sp, D_MODEL), xs.reshape(bs, 1, D_MODEL),
            a_k_p, a_v_p, a_k_s, a_v_s, b_k_p, b_v_p, b_k_s, b_v_s,
            c_p[0], c_p[1], c_p[2], c_p[3], c_p[4][:, -keep:], c_p[5][:, -keep:],
            c_s[0], c_s[1], c_s[2], c_s[3], c_kw_s, c_vw_s,
            d_k_p, d_v_p, lf_p, d_k_s, d_v_s, lf_s)
```
